```python
import math
import jax, jax.numpy as jnp
from jax import lax
import numpy as np


D_MODEL = 2048
BATCH = 4
SEQ = 4096
DEPTH = 1

HG_HEADS = 8
HG_DK = 128
HG_DV = 128
HG_CHUNK = 64
HG_WIDTH = HG_HEADS * HG_DV
DA_HEADS = 8
DA_DH = 64
DA_DV = 2 * DA_DH
DA_QBLOCK = 128
DA_WIDTH = DA_HEADS * DA_DV
RP_BUCKETS = 32
RP_MAX_EXACT = 16
RP_MAX_DIST = 128
IN_SPLITS = [HG_HEADS * HG_DK, HG_HEADS * HG_DK, HG_WIDTH, HG_WIDTH,
             DA_HEADS * 2 * DA_DH, DA_HEADS * 2 * DA_DH, DA_WIDTH, D_MODEL, D_MODEL]
IN_COLS = sum(IN_SPLITS)
N_EXPERTS = 64
N_GROUPS = 8
TOPK_GROUPS = 4
TOP_K = 8
D_EXPERT = 512
ROUTE_SCALE = 2.5
MOE_BLOCK = 256
EPS = 1e-6

kernel_name = 'hybrid_hgrn2_diffattn_moe_block'

F32 = jnp.float32


def rms_norm(x, g):
    xf = x.astype(F32)
    return xf * lax.rsqrt(jnp.mean(xf * xf, axis=-1, keepdims=True) + EPS) * g.astype(F32)


def modulate(hn, shift, scale):
    return hn * (1.0 + scale[:, None, :].astype(F32)) + shift[:, None, :].astype(F32)


def t5_bucket(n):
    nf = jnp.maximum(n, 1).astype(F32)
    large = RP_MAX_EXACT + (jnp.log(nf / RP_MAX_EXACT) / math.log(RP_MAX_DIST / RP_MAX_EXACT)
                            * (RP_BUCKETS - RP_MAX_EXACT)).astype(jnp.int32)
    large = jnp.minimum(large, RP_BUCKETS - 1)
    return jnp.where(n < RP_MAX_EXACT, n, large)


def hgrn2_branch(q, f_logit, i, g, lb, norm_g):
    bsz, seq, _ = q.shape
    n_chunks = seq // HG_CHUNK
    qs = jax.nn.silu(q.astype(F32)) * HG_DK ** -0.5
    fgate = lb + (1.0 - lb) * jax.nn.sigmoid(f_logit.astype(F32))
    k = 1.0 - fgate
    log_f = jnp.log(fgate)

    def to_chunks(t, d):
        return t.reshape(bsz, n_chunks, HG_CHUNK, HG_HEADS, d).transpose(1, 0, 3, 2, 4)

    qc, kc, gc = to_chunks(qs, HG_DK), to_chunks(k, HG_DK), to_chunks(log_f, HG_DK)
    vc = to_chunks(i.astype(F32), HG_DV)
    causal = jnp.tril(jnp.ones((HG_CHUNK, HG_CHUNK), bool))[:, :, None]

    def step(state, xs):
        qb, kb, vb, gb = xs
        b = jnp.cumsum(gb, axis=2)
        rel = b[:, :, :, None, :] - b[:, :, None, :, :]
        decay = jnp.exp(jnp.where(causal, rel, -jnp.inf))
        scores = jnp.einsum('bhtk,bhtsk,bhsk->bhts', qb, decay, kb)
        o = (jnp.einsum('bhts,bhsv->bhtv', scores, vb)
             + jnp.einsum('bhtk,bhkv->bhtv', qb * jnp.exp(b), state))
        b_last = b[:, :, -1:, :]
        state = (jnp.exp(b_last[:, :, 0, :, None]) * state
                 + jnp.einsum('bhsk,bhsv->bhkv', kb * jnp.exp(b_last - b), vb))
        return state, o

    state0 = jnp.zeros((bsz, HG_HEADS, HG_DK, HG_DV), F32)
    _, o = lax.scan(step, state0, (qc, kc, vc, gc))
    o = o.transpose(1, 0, 3, 2, 4).reshape(bsz, seq, HG_HEADS, HG_DV)
    gate = jax.nn.silu(g.astype(F32)).reshape(bsz, seq, HG_HEADS, HG_DV)
    return (rms_norm(o, norm_g) * gate).reshape(bsz, seq, HG_WIDTH)


def diff_attention_branch(q, k, v, q_g, k_g, lq1, lk1, lq2, lk2, norm_g, rel_table, lambda_init):
    bsz, seq, _ = q.shape

    def heads(t):
        return t.reshape(bsz, seq, DA_HEADS, 2, DA_DH).transpose(0, 2, 3, 1, 4)

    qh = rms_norm(heads(q), q_g) * DA_DH ** -0.5
    kh = rms_norm(heads(k), k_g)
    vh = v.reshape(bsz, seq, DA_HEADS, DA_DV).transpose(0, 2, 1, 3).astype(F32)
    lam = (jnp.exp(jnp.sum(lq1.astype(F32) * lk1.astype(F32)))
           - jnp.exp(jnp.sum(lq2.astype(F32) * lk2.astype(F32))) + lambda_init)
    n_blocks = seq // DA_QBLOCK
    q_blocks = qh.reshape(bsz, DA_HEADS, 2, n_blocks, DA_QBLOCK, DA_DH).transpose(3, 0, 1, 2, 4, 5)
    k_pos = jnp.arange(seq)
    table = rel_table.astype(F32)

    def attend(args):
        qb, blk = args
        q_pos = blk * DA_QBLOCK + jnp.arange(DA_QBLOCK)
        dist = q_pos[:, None] - k_pos[None, :]
        bias = table[t5_bucket(jnp.maximum(dist, 0))].transpose(2, 0, 1)
        logits = jnp.einsum('bhcqd,bhckd->bhcqk', qb, kh) + bias[None, :, None]
        logits = jnp.where(dist >= 0, logits, -jnp.inf)
        p = jax.nn.softmax(logits, axis=-1)
        a = p[:, :, 0] - lam * p[:, :, 1]
        return jnp.einsum('bhqk,bhkv->bhqv', a, vh)

    o = lax.map(attend, (q_blocks, jnp.arange(n_blocks)))
    o = o.transpose(1, 0, 3, 2, 4).reshape(bsz, seq, DA_HEADS, DA_DV)
    o = rms_norm(o, norm_g) * (1.0 - lambda_init)
    return o.reshape(bsz, seq, DA_WIDTH)


def hybrid_mixer(h, w_in, lb, hg_norm_g, q_g, k_g, lq1, lk1, lq2, lk2, da_norm_g, rel_table,
                 w_branch_a, w_branch_b, w_out, lambda_init):
    proj = jnp.einsum('bsd,dn->bsn', h, w_in)
    split_at = np.cumsum(IN_SPLITS)[:-1].tolist()
    hq, hf, hi, hg, dq, dk, dv, ga, gb = jnp.split(proj, split_at, axis=-1)
    ya = jnp.einsum('bsn,nd->bsd', hgrn2_branch(hq, hf, hi, hg, lb, hg_norm_g).astype(h.dtype), w_branch_a)
    yb = jnp.einsum('bsn,nd->bsd', diff_attention_branch(dq, dk, dv, q_g, k_g, lq1, lk1, lq2, lk2,
                                                        da_norm_g, rel_table, lambda_init).astype(h.dtype),
                    w_branch_b)
    merged = (jax.nn.sigmoid(ga.astype(F32)) * ya.astype(F32)
              + jax.nn.sigmoid(gb.astype(F32)) * yb.astype(F32)).astype(h.dtype)
    return jnp.einsum('bsd,de->bse', merged, w_out)


def swiglu(x, wg, wu, wd):
    return (jax.nn.silu(x @ wg) * (x @ wu)) @ wd


def routed_experts(hf, top_e, gate_w, eg, eu, ed):
    n_tok, d = hf.shape
    n_assign = n_tok * TOP_K
    n_blocks = n_assign // MOE_BLOCK + N_EXPERTS + 1
    n_slots = n_blocks * MOE_BLOCK
    flat_e = top_e.reshape(-1)
    flat_tok = jnp.repeat(jnp.arange(n_tok, dtype=jnp.int32), TOP_K)
    flat_w = gate_w.reshape(-1)
    order = jnp.argsort(flat_e, stable=True)
    e_sorted = flat_e[order]
    counts = jnp.bincount(flat_e, length=N_EXPERTS)
    padded = (counts + MOE_BLOCK - 1) // MOE_BLOCK * MOE_BLOCK
    pad_end = jnp.cumsum(padded)
    pad_start = pad_end - padded
    first = jnp.cumsum(counts) - counts
    dest = pad_start[e_sorted] + jnp.arange(n_assign, dtype=jnp.int32) - first[e_sorted]
    slot_tok = jnp.full((n_slots,), n_tok, jnp.int32).at[dest].set(flat_tok[order])
    slot_w = jnp.zeros((n_slots,), F32).at[dest].set(flat_w[order])
    block_e = jnp.minimum(jnp.searchsorted(pad_end, jnp.arange(n_blocks) * MOE_BLOCK, side='right'),
                          N_EXPERTS - 1)
    h_pad = jnp.concatenate([hf, jnp.zeros((1, d), hf.dtype)], axis=0)

    def block(acc, xs):
        e, tok, w = xs
        y = swiglu(h_pad[tok], eg[e], eu[e], ed[e])
        return acc.at[tok].add(y.astype(F32) * w[:, None]), None

    acc, _ = lax.scan(block, jnp.zeros((n_tok + 1, d), F32),
                      (block_e, slot_tok.reshape(n_blocks, MOE_BLOCK), slot_w.reshape(n_blocks, MOE_BLOCK)))
    return acc[:n_tok]


def moe_ffn(h, w_r, b_r, eg, eu, ed, sg, su, sd):
    bsz, seq, d = h.shape
    n_tok = bsz * seq
    hf = h.reshape(n_tok, d)
    scores = jax.nn.sigmoid(jnp.einsum('td,de->te', hf.astype(F32), w_r.astype(F32)))
    sel = scores + b_r.astype(F32)
    group_score = lax.top_k(sel.reshape(n_tok, N_GROUPS, N_EXPERTS // N_GROUPS), 2)[0].sum(-1)
    _, top_groups = lax.top_k(group_score, TOPK_GROUPS)
    group_mask = jax.nn.one_hot(top_groups, N_GROUPS, dtype=F32).sum(1) > 0
    expert_mask = jnp.repeat(group_mask, N_EXPERTS // N_GROUPS, axis=1)
    _, top_e = lax.top_k(jnp.where(expert_mask, sel, -jnp.inf), TOP_K)
    gate_w = jnp.take_along_axis(scores, top_e, axis=1)
    gate_w = gate_w / jnp.sum(gate_w, axis=-1, keepdims=True) * ROUTE_SCALE
    routed = routed_experts(hf, top_e, gate_w, eg, eu, ed)
    out = swiglu(hf, sg, su, sd).astype(F32) + routed
    return out.reshape(bsz, seq, d)


def setup_inputs(seed: int = 0) -> dict:
    key = jax.random.key(seed)
    ks = jax.random.split(key, 32)
    L, D = DEPTH, D_MODEL
    nrm = lambda k, shape, s: jax.random.normal(k, shape, F32) * s
    return {
        'x': nrm(ks[0], (BATCH, SEQ, D), 1.0),
        'c': nrm(ks[1], (BATCH, D), 1.0),
        'ada_w': nrm(ks[2], (L, D, 6 * D), 0.5 * D ** -0.5),
        'ada_b': nrm(ks[3], (L, 6 * D), 0.02),
        'norm1_g': 1.0 + nrm(ks[4], (L, D), 0.02),
        'w_in': nrm(ks[5], (L, D, IN_COLS), D ** -0.5),
        'lb_logits': nrm(ks[6], (L + 1, HG_HEADS * HG_DK), 1.0),
        'hg_norm_g': 1.0 + nrm(ks[7], (L, HG_DV), 0.02),
        'q_norm_g': 1.0 + nrm(ks[8], (L, DA_DH), 0.02),
        'k_norm_g': 1.0 + nrm(ks[9], (L, DA_DH), 0.02),
        'lambda_q1': nrm(ks[10], (L, DA_DH), 0.1),
        'lambda_k1': nrm(ks[11], (L, DA_DH), 0.1),
        'lambda_q2': nrm(ks[12], (L, DA_DH), 0.1),
        'lambda_k2': nrm(ks[13], (L, DA_DH), 0.1),
        'da_norm_g': 1.0 + nrm(ks[14], (L, DA_DV), 0.02),
        'rel_bias': nrm(ks[15], (RP_BUCKETS, DA_HEADS), 0.5),
        'w_branch_a': nrm(ks[16], (L, HG_WIDTH, D), HG_WIDTH ** -0.5),
        'w_branch_b': nrm(ks[17], (L, DA_WIDTH, D), DA_WIDTH ** -0.5),
        'w_out': nrm(ks[18], (L, D, D), D ** -0.5),
        'norm2_g': 1.0 + nrm(ks[19], (L, D), 0.02),
        'router_w': nrm(ks[20], (L, D, N_EXPERTS), D ** -0.5),
        'router_bias': nrm(ks[21], (L, N_EXPERTS), 0.01),
        'w_exp_gate': nrm(ks[22], (L, N_EXPERTS, D, D_EXPERT), D ** -0.5),
        'w_exp_up': nrm(ks[23], (L, N_EXPERTS, D, D_EXPERT), D ** -0.5),
        'w_exp_down': nrm(ks[24], (L, N_EXPERTS, D_EXPERT, D), D_EXPERT ** -0.5),
        'w_sh_gate': nrm(ks[25], (L, D, D_EXPERT), D ** -0.5),
        'w_sh_up': nrm(ks[26], (L, D, D_EXPERT), D ** -0.5),
        'w_sh_down': nrm(ks[27], (L, D_EXPERT, D), D_EXPERT ** -0.5),
    }


def reference(x, c, ada_w, ada_b, norm1_g, w_in, lb_logits, hg_norm_g, q_norm_g, k_norm_g,
              lambda_q1, lambda_k1, lambda_q2, lambda_k2, da_norm_g, rel_bias, w_branch_a,
              w_branch_b, w_out, norm2_g, router_w, router_bias, w_exp_gate, w_exp_up,
              w_exp_down, w_sh_gate, w_sh_up, w_sh_down):
    lower_bounds = jnp.cumsum(jax.nn.softmax(lb_logits.astype(F32), axis=0), axis=0)
    c_act = jax.nn.silu(c)
    for l in range(DEPTH):
        lambda_init = 0.8 - 0.6 * math.exp(-0.3 * l)
        mod = jnp.einsum('bd,dn->bn', c_act, ada_w[l]) + ada_b[l]
        shift1, scale1, gate1, shift2, scale2, gate2 = jnp.split(mod, 6, axis=-1)
        h = modulate(rms_norm(x, norm1_g[l]), shift1, scale1).astype(x.dtype)
        y = hybrid_mixer(h, w_in[l], lower_bounds[l], hg_norm_g[l], q_norm_g[l], k_norm_g[l],
                         lambda_q1[l], lambda_k1[l], lambda_q2[l], lambda_k2[l], da_norm_g[l],
                         rel_bias, w_branch_a[l], w_branch_b[l], w_out[l], lambda_init)
        x = (x.astype(F32) + gate1[:, None, :].astype(F32) * y.astype(F32)).astype(x.dtype)
        h = modulate(rms_norm(x, norm2_g[l]), shift2, scale2).astype(x.dtype)
        y = moe_ffn(h, router_w[l], router_bias[l], w_exp_gate[l], w_exp_up[l], w_exp_down[l],
                    w_sh_gate[l], w_sh_up[l], w_sh_down[l])
        x = (x.astype(F32) + gate2[:, None, :].astype(F32) * y).astype(x.dtype)
    return x
```

```python
import functools
import math

import numpy as np
import jax
import jax.numpy as jnp
from jax import lax
from jax.experimental import pallas as pl
from jax.experimental.pallas import tpu as pltpu

F32 = jnp.float32
BF16 = jnp.bfloat16
I32 = jnp.int32

HG_HEADS = 8
HG_DK = 128
HG_DV = 128
DA_HEADS = 8
DA_DH = 64
DA_DV = 128
RP_BUCKETS = 32
RP_MAX_EXACT = 16
RP_MAX_DIST = 128
N_EXPERTS = 64
N_GROUPS = 8
TOPK_GROUPS = 4
TOP_K = 8
ROUTE_SCALE = 2.5
EPS = 1e-6
LAMBDA_INIT = 0.8 - 0.6 * math.exp(-0.3 * 0)

HEAD_W = 128
VMEM_LIMIT_BYTES = 56 * 1024 * 1024
NEG_BIG = -1e30

NT_DIMS = (((1,), (1,)), ((), ()))
TN_DIMS = (((0,), (0,)), ((), ()))


def _cparams(sem):
    return pltpu.CompilerParams(dimension_semantics=sem, vmem_limit_bytes=VMEM_LIMIT_BYTES)


def _silu(x):
    return x * jax.nn.sigmoid(x)


def _t5_bucket_starts():
    n = np.arange(0, RP_MAX_DIST + 1)
    nf = np.maximum(n, 1).astype(np.float32)
    large = RP_MAX_EXACT + (np.log(nf / np.float32(RP_MAX_EXACT)) / np.float32(math.log(RP_MAX_DIST / RP_MAX_EXACT))
                            * np.float32(RP_BUCKETS - RP_MAX_EXACT)).astype(np.int32)
    large = np.minimum(large, RP_BUCKETS - 1)
    bucket = np.where(n < RP_MAX_EXACT, n, large)
    assert np.all(np.diff(bucket) >= 0) and bucket[-1] == RP_BUCKETS - 1
    return [int(np.argmax(bucket >= j)) for j in range(RP_BUCKETS)]


BUCKET_START = _t5_bucket_starts()


def _ada_kernel(c_ref, w_ref, b_ref, o_ref):
    ca = _silu(c_ref[...]).astype(BF16)
    o_ref[...] = jnp.dot(ca, w_ref[...].astype(BF16), preferred_element_type=F32) + b_ref[...]


def _ada_mod(c, ada_w, ada_b):
    bsz, d = c.shape
    n = ada_w.shape[1]
    rows = 8
    cp = jnp.zeros((rows, d), F32).at[:bsz].set(c)
    tn = 1024
    out = pl.pallas_call(
        _ada_kernel,
        grid=(n // tn,),
        in_specs=[pl.BlockSpec((rows, d), lambda j: (0, 0)),
                  pl.BlockSpec((d, tn), lambda j: (0, j)),
                  pl.BlockSpec((1, tn), lambda j: (0, j))],
        out_specs=pl.BlockSpec((rows, tn), lambda j: (0, j)),
        out_shape=jax.ShapeDtypeStruct((rows, n), F32),
        compiler_params=_cparams(("arbitrary",)),
    )(cp, ada_w, ada_b.reshape(1, n))
    return out[:bsz]


def _inproj_kernel(x_ref, sc_ref, sh_ref, g_ref, w_ref, o_ref, h_scr):
    @pl.when(pl.program_id(1) == 0)
    def _():
        x = x_ref[...]
        ms = jnp.mean(x * x, axis=-1, keepdims=True)
        hn = x * lax.rsqrt(ms + EPS) * g_ref[...]
        h_scr[...] = (hn * (1.0 + sc_ref[...]) + sh_ref[...]).astype(BF16)

    o_ref[...] = jnp.dot(h_scr[...], w_ref[...], preferred_element_type=F32).astype(o_ref.dtype)


def _in_projection(x2, scale, shift, g, w_bf, seq):
    ntok, d = x2.shape
    n = w_bf.shape[1]
    tm = min(1024, seq)
    tn = 1024
    per_b = seq // tm
    return pl.pallas_call(
        _inproj_kernel,
        grid=(ntok // tm, n // tn),
        in_specs=[pl.BlockSpec((tm, d), lambda i, j: (i, 0)),
                  pl.BlockSpec((None, 1, d), lambda i, j: (i // per_b, 0, 0)),
                  pl.BlockSpec((None, 1, d), lambda i, j: (i // per_b, 0, 0)),
                  pl.BlockSpec((1, d), lambda i, j: (0, 0)),
                  pl.BlockSpec((d, tn), lambda i, j: (0, j))],
        out_specs=pl.BlockSpec((tm, tn), lambda i, j: (i, j)),
        out_shape=jax.ShapeDtypeStruct((ntok, n), BF16),
        scratch_shapes=[pltpu.VMEM((tm, d), BF16)],
        compiler_params=_cparams(("arbitrary", "arbitrary")),
    )(x2, scale[:, None, :], shift[:, None, :], g.reshape(1, d), w_bf)


def _hgrn_kernel(q_ref, f_ref, i_ref, g_ref, lbl_ref, ng_ref, o_ref, state_ref, *, tile):
    @pl.when(pl.program_id(1) == 0)
    def _():
        state_ref[...] = jnp.zeros_like(state_ref)

    n_levels = tile.bit_length() - 1
    row = lax.broadcasted_iota(I32, (tile, tile), 0)
    col = lax.broadcasted_iota(I32, (tile, tile), 1)
    lev = jnp.where(row >= col, 31 - lax.clz(row ^ col), -2)
    rowk = lax.broadcasted_iota(I32, (tile, HG_DK), 0)

    for h in range(HG_HEADS):
        sl = slice(h * HEAD_W, (h + 1) * HEAD_W)
        q = _silu(q_ref[:, sl].astype(F32)) * (HG_DK ** -0.5)
        ll = lbl_ref[:, sl]
        el = jnp.exp(ll - jnp.max(ll, axis=0, keepdims=True))
        lb = el[0:1, :] / jnp.sum(el, axis=0, keepdims=True)
        fg = lb + (1.0 - lb) * jax.nn.sigmoid(f_ref[:, sl].astype(F32))
        k = 1.0 - fg
        g = jnp.log(fg)
        v = i_ref[:, sl]

        scores = jnp.where(lev == -1,
                           lax.dot_general(q.astype(BF16), k.astype(BF16), NT_DIMS, preferred_element_type=F32),
                           0.0)
        c = g
        e = g
        for lvl in range(n_levels):
            blk = 1 << lvl
            qd = (q * jnp.exp(c)).astype(BF16)
            kd = (k * jnp.exp(e - c)).astype(BF16)
            s_l = lax.dot_general(qd, kd, NT_DIMS, preferred_element_type=F32)
            scores = jnp.where(lev == lvl, s_l, scores)
            odd = (rowk & blk) != 0
            e_prev = pltpu.roll(e, blk, axis=0)
            e_next = pltpu.roll(e, tile - blk, axis=0)
            c = c + jnp.where(odd, e_prev, 0.0)
            e = e + jnp.where(odd, e_prev, e_next)
        st = state_ref[h]
        qb = (q * jnp.exp(c)).astype(BF16)
        o = (jnp.dot(scores.astype(BF16), v, preferred_element_type=F32)
             + lax.dot_general(qb, st.astype(BF16), NT_DIMS, preferred_element_type=F32))
        kd = (k * jnp.exp(e - c)).astype(BF16)
        state_ref[h] = st * jnp.exp(e[0:1, :]) + lax.dot_general(v, kd, TN_DIMS, preferred_element_type=F32)

        ms = jnp.mean(o * o, axis=-1, keepdims=True)
        gate = _silu(g_ref[:, sl].astype(F32))
        o_ref[:, sl] = (o * lax.rsqrt(ms + EPS) * ng_ref[...] * gate).astype(o_ref.dtype)


def _hgrn_branch(proj, lb_logits, norm_g, bsz, seq, col0):
    ntok = proj.shape[0]
    tile = min(256, seq)
    nt = seq // tile
    width = HG_HEADS * HEAD_W
    cb = col0 // width

    def spec(off):
        return pl.BlockSpec((tile, width), lambda b, t, off=off: (b * nt + t, cb + off))

    return pl.pallas_call(
        functools.partial(_hgrn_kernel, tile=tile),
        grid=(bsz, nt),
        in_specs=[spec(0), spec(1), spec(2), spec(3),
                  pl.BlockSpec(lb_logits.shape, lambda b, t: (0, 0)),
                  pl.BlockSpec((1, HG_DV), lambda b, t: (0, 0))],
        out_specs=pl.BlockSpec((tile, width), lambda b, t: (b * nt + t, 0)),
        out_shape=jax.ShapeDtypeStruct((ntok, width), BF16),
        scratch_shapes=[pltpu.VMEM((HG_HEADS, HG_DV, HG_DK), F32)],
        compiler_params=_cparams(("arbitrary", "arbitrary")),
    )(proj, proj, proj, proj, lb_logits, norm_g.reshape(1, HG_DV))


def _qknorm_kernel(q_ref, k_ref, qg_ref, kg_ref, qo_ref, ko_ref):
    lane = lax.broadcasted_iota(I32, (q_ref.shape[0], HEAD_W), 1)
    lo = lane < DA_DH

    def norm(x_ref, g_ref, o_ref, scale):
        for h in range(DA_HEADS):
            sl = slice(h * HEAD_W, (h + 1) * HEAD_W)
            x = x_ref[:, sl].astype(F32)
            xx = x * x
            s0 = jnp.sum(jnp.where(lo, xx, 0.0), axis=-1, keepdims=True)
            s1 = jnp.sum(jnp.where(lo, 0.0, xx), axis=-1, keepdims=True)
            ms = jnp.where(lo, s0, s1) * (1.0 / DA_DH)
            o_ref[:, sl] = (x * lax.rsqrt(ms + EPS) * g_ref[...] * scale).astype(o_ref.dtype)

    norm(q_ref, qg_ref, qo_ref, DA_DH ** -0.5)
    norm(k_ref, kg_ref, ko_ref, 1.0)


def _qk_norm(proj, q_g, k_g, col_q, col_k):
    ntok = proj.shape[0]
    width = DA_HEADS * HEAD_W
    tm = min(512, ntok)
    qg = jnp.tile(q_g, 2).reshape(1, HEAD_W)
    kg = jnp.tile(k_g, 2).reshape(1, HEAD_W)
    return pl.pallas_call(
        _qknorm_kernel,
        grid=(ntok // tm,),
        in_specs=[pl.BlockSpec((tm, width), lambda i: (i, col_q // width)),
                  pl.BlockSpec((tm, width), lambda i: (i, col_k // width)),
                  pl.BlockSpec((1, HEAD_W), lambda i: (0, 0)),
                  pl.BlockSpec((1, HEAD_W), lambda i: (0, 0))],
        out_specs=[pl.BlockSpec((tm, width), lambda i: (i, 0)),
                   pl.BlockSpec((tm, width), lambda i: (i, 0))],
        out_shape=[jax.ShapeDtypeStruct((ntok, width), BF16)] * 2,
        compiler_params=_cparams(("arbitrary",)),
    )(proj, proj, qg, kg)


def _attn_kernel(qi_ref, ki_ref, q_ref, k_ref, v_ref, rel_ref, lamp_ref, ng_ref, o_ref,
                 m_scr, l_scr, acc_scr, bias_scr, *, tile):
    b = pl.program_id(0)
    p = pl.program_id(1)
    qi = qi_ref[p]
    ki = ki_ref[p]
    diff = qi - ki

    row = lax.broadcasted_iota(I32, (tile, tile), 0)
    col = lax.broadcasted_iota(I32, (tile, tile), 1)

    @pl.when((b == 0) & (p == 0))
    def _():
        def per_head(h, carry):
            for d in range(2):
                dist = row - col + d * tile
                bias = jnp.full((tile, tile), rel_ref[0, h], F32)
                for j in range(1, RP_BUCKETS):
                    bias = jnp.where(dist >= BUCKET_START[j], rel_ref[j, h], bias)
                bias_scr[d, h] = bias
            return carry
        lax.fori_loop(0, DA_HEADS, per_head, 0)

    @pl.when(ki == 0)
    def _():
        m_scr[...] = jnp.full_like(m_scr, NEG_BIG)
        l_scr[...] = jnp.zeros_like(l_scr)
        acc_scr[...] = jnp.zeros_like(acc_scr)

    lane = lax.broadcasted_iota(I32, (tile, HEAD_W), 1)
    lo = lane < DA_DH

    def step(near):
        for h in range(DA_HEADS):
            sl = slice(h * HEAD_W, (h + 1) * HEAD_W)
            q = q_ref[:, sl]
            k = k_ref[:, sl]
            v = v_ref[:, sl]
            if near:
                bias = bias_scr[jnp.minimum(diff, 1), h]
            else:
                bias = rel_ref[RP_BUCKETS - 1, h]
            for c in range(2):
                qc = jnp.where(lo, q, jnp.zeros_like(q)) if c == 0 else jnp.where(lo, jnp.zeros_like(q), q)
                s = lax.dot_general(qc, k, NT_DIMS, preferred_element_type=F32) + bias
                if near:
                    s = jnp.where((diff > 0) | (row >= col), s, NEG_BIG)
                idx = 2 * h + c
                m_old = m_scr[idx]
                m_new = jnp.maximum(m_old, jnp.max(s, axis=-1, keepdims=True))
                alpha = jnp.exp(m_old - m_new)
                pr = jnp.exp(s - m_new)
                l_scr[idx] = alpha * l_scr[idx] + jnp.sum(pr, axis=-1, keepdims=True)
                acc_scr[idx] = alpha * acc_scr[idx] + jnp.dot(pr.astype(BF16), v, preferred_element_type=F32)
                m_scr[idx] = m_new

    @pl.when(diff <= 1)
    def _():
        step(True)

    @pl.when(diff > 1)
    def _():
        step(False)

    @pl.when(diff == 0)
    def _():
        lp = lamp_ref[...]
        lam = (jnp.exp(jnp.sum(lp[0:1] * lp[1:2], axis=-1, keepdims=True))
               - jnp.exp(jnp.sum(lp[2:3] * lp[3:4], axis=-1, keepdims=True)) + LAMBDA_INIT)
        for h in range(DA_HEADS):
            sl = slice(h * HEAD_W, (h + 1) * HEAD_W)
            o = acc_scr[2 * h] / l_scr[2 * h] - lam * (acc_scr[2 * h + 1] / l_scr[2 * h + 1])
            ms = jnp.mean(o * o, axis=-1, keepdims=True)
            o_ref[:, sl] = (o * lax.rsqrt(ms + EPS) * ng_ref[...] * (1.0 - LAMBDA_INIT)).astype(o_ref.dtype)


def _diff_attention(qn, kn, proj, col_v, rel_bias, lam_params, norm_g, bsz, seq):
    ntok = qn.shape[0]
    tile = min(256, seq)
    nq = seq // tile
    width = DA_HEADS * HEAD_W
    qi_list, ki_list = [], []
    for a in range(nq):
        for c in range(a + 1):
            qi_list.append(a)
            ki_list.append(c)
    qi_arr = jnp.asarray(qi_list, I32)
    ki_arr = jnp.asarray(ki_list, I32)
    cv = col_v // width
    grid_spec = pltpu.PrefetchScalarGridSpec(
        num_scalar_prefetch=2,
        grid=(bsz, len(qi_list)),
        in_specs=[pl.BlockSpec((tile, width), lambda b, p, qi, ki: (b * nq + qi[p], 0)),
                  pl.BlockSpec((tile, width), lambda b, p, qi, ki: (b * nq + ki[p], 0)),
                  pl.BlockSpec((tile, width), lambda b, p, qi, ki: (b * nq + ki[p], cv)),
                  pl.BlockSpec(memory_space=pltpu.SMEM),
                  pl.BlockSpec((4, DA_DH), lambda b, p, qi, ki: (0, 0)),
                  pl.BlockSpec((1, DA_DV), lambda b, p, qi, ki: (0, 0))],
        out_specs=pl.BlockSpec((tile, width), lambda b, p, qi, ki: (b * nq + qi[p], 0)),
        scratch_shapes=[pltpu.VMEM((2 * DA_HEADS, tile, 1), F32),
                        pltpu.VMEM((2 * DA_HEADS, tile, 1), F32),
                        pltpu.VMEM((2 * DA_HEADS, tile, DA_DV), F32),
                        pltpu.VMEM((2, DA_HEADS, tile, tile), F32)],
    )
    return pl.pallas_call(
        functools.partial(_attn_kernel, tile=tile),
        grid_spec=grid_spec,
        out_shape=jax.ShapeDtypeStruct((ntok, width), BF16),
        compiler_params=_cparams(("arbitrary", "arbitrary")),
    )(qi_arr, ki_arr, qn, kn, proj, rel_bias, lam_params, norm_g.reshape(1, DA_DV))


def _merge_kernel(oa_ref, ob_ref, ga_ref, gb_ref, x_ref, g1_ref, sc_ref, sh_ref, n2_ref,
                  wa_ref, wb_ref, wo_ref, wrh_ref, wrl_ref, x1_ref, h2_ref, lg_ref):
    ya = jnp.dot(oa_ref[...], wa_ref[...], preferred_element_type=F32)
    yb = jnp.dot(ob_ref[...], wb_ref[...], preferred_element_type=F32)
    merged = (jax.nn.sigmoid(ga_ref[...].astype(F32)) * ya
              + jax.nn.sigmoid(gb_ref[...].astype(F32)) * yb).astype(BF16)
    y = jnp.dot(merged, wo_ref[...], preferred_element_type=F32)
    x1 = x_ref[...] + g1_ref[...] * y
    x1_ref[...] = x1
    ms = jnp.mean(x1 * x1, axis=-1, keepdims=True)
    h2 = (x1 * lax.rsqrt(ms + EPS) * n2_ref[...]) * (1.0 + sc_ref[...]) + sh_ref[...]
    h2_ref[...] = h2
    hh = h2.astype(BF16)
    hl = (h2 - hh.astype(F32)).astype(BF16)
    lg_ref[...] = (lax.dot_general(wrh_ref[...], hh, NT_DIMS, preferred_element_type=F32)
                   + lax.dot_general(wrh_ref[...], hl, NT_DIMS, preferred_element_type=F32)
                   + lax.dot_general(wrl_ref[...], hh, NT_DIMS, preferred_element_type=F32))


def _merge_out(o_hg, o_da, proj, col_ga, col_gb, x2, gate1, scale2, shift2, norm2_g,
               wa, wb, wo, wr_hi, wr_lo, seq):
    ntok, d = x2.shape
    tm = min(256, seq)
    per_b = seq // tm
    wa_w = o_hg.shape[1]
    wb_w = o_da.shape[1]

    def const(shape):
        return pl.BlockSpec(shape, lambda i: (0,) * len(shape), pipeline_mode=pl.Buffered(1))

    def perb():
        return pl.BlockSpec((None, 1, d), lambda i: (i // per_b, 0, 0))

    return pl.pallas_call(
        _merge_kernel,
        grid=(ntok // tm,),
        in_specs=[pl.BlockSpec((tm, wa_w), lambda i: (i, 0)),
                  pl.BlockSpec((tm, wb_w), lambda i: (i, 0)),
                  pl.BlockSpec((tm, d), lambda i: (i, col_ga // d)),
                  pl.BlockSpec((tm, d), lambda i: (i, col_gb // d)),
                  pl.BlockSpec((tm, d), lambda i: (i, 0)),
                  perb(), perb(), perb(),
                  const((1, d)),
                  const((wa_w, d)), const((wb_w, d)), const((d, d)),
                  const((N_EXPERTS, d)), const((N_EXPERTS, d))],
        out_specs=[pl.BlockSpec((tm, d), lambda i: (i, 0)),
                   pl.BlockSpec((tm, d), lambda i: (i, 0)),
                   pl.BlockSpec((N_EXPERTS, tm), lambda i: (0, i))],
        out_shape=[jax.ShapeDtypeStruct((ntok, d), F32),
                   jax.ShapeDtypeStruct((ntok, d), F32),
                   jax.ShapeDtypeStruct((N_EXPERTS, ntok), F32)],
        compiler_params=_cparams(("arbitrary",)),
    )(o_hg, o_da, proj, proj, x2, gate1[:, None, :], scale2[:, None, :], shift2[:, None, :],
      norm2_g.reshape(1, d), wa, wb, wo, wr_hi, wr_lo)


def _route_kernel(lg_ref, rb_ref, te_ref, gw_ref, rk_ref, cnt_ref, carry_scr, *, tt):
    @pl.when(pl.program_id(0) == 0)
    def _():
        carry_scr[...] = jnp.zeros_like(carry_scr)

    per_g = N_EXPERTS // N_GROUPS
    scores = jax.nn.sigmoid(lg_ref[...])
    sel = scores + rb_ref[...]
    sel3 = sel.reshape(N_GROUPS, per_g, tt)
    j_io = lax.broadcasted_iota(I32, (N_GROUPS, per_g, tt), 1)
    m1 = jnp.max(sel3, axis=1, keepdims=True)
    i1 = jnp.min(jnp.where(sel3 == m1, j_io, per_g), axis=1, keepdims=True)
    m2 = jnp.max(jnp.where(j_io == i1, -jnp.inf, sel3), axis=1, keepdims=True)
    gs = (m1 + m2).reshape(N_GROUPS, tt)
    g_io = lax.broadcasted_iota(I32, (N_GROUPS, tt), 0)
    gmask = jnp.zeros((N_GROUPS, tt), jnp.bool_)
    for _ in range(TOPK_GROUPS):
        gm = jnp.max(gs, axis=0, keepdims=True)
        gi = jnp.min(jnp.where(gs == gm, g_io, N_GROUPS), axis=0, keepdims=True)
        hit = g_io == gi
        gmask = gmask | hit
        gs = jnp.where(hit, -jnp.inf, gs)
    emask = jnp.broadcast_to(gmask.reshape(N_GROUPS, 1, tt), (N_GROUPS, per_g, tt)).reshape(N_EXPERTS, tt)
    cand = jnp.where(emask, sel, -jnp.inf)
    e_io = lax.broadcasted_iota(I32, (N_EXPERTS, tt), 0)
    chosen = jnp.zeros((N_EXPERTS, tt), jnp.bool_)
    picks = []
    for _ in range(TOP_K):
        em = jnp.max(cand, axis=0, keepdims=True)
        ei = jnp.min(jnp.where(cand == em, e_io, N_EXPERTS), axis=0, keepdims=True)
        hit = e_io == ei
        chosen = chosen | hit
        cand = jnp.where(hit, -jnp.inf, cand)
        picks.append((ei, hit))
    gsel = jnp.where(chosen, scores, 0.0)
    wnorm = gsel / jnp.sum(gsel, axis=0, keepdims=True) * ROUTE_SCALE
    ch = jnp.where(chosen, 1.0, 0.0)
    tri = (lax.broadcasted_iota(I32, (tt, tt), 0) <= lax.broadcasted_iota(I32, (tt, tt), 1))
    incl = jnp.dot(ch.astype(BF16), jnp.where(tri, 1.0, 0.0).astype(BF16), preferred_element_type=F32)
    carry = carry_scr[...]
    excl = incl - ch + carry
    carry_new = carry + incl[:, tt - 1:tt]
    carry_scr[...] = carry_new
    for r, (ei, hit) in enumerate(picks):
        te_ref[r:r + 1, :] = ei
        gw_ref[r:r + 1, :] = jnp.sum(jnp.where(hit, wnorm, 0.0), axis=0, keepdims=True)
        rk_ref[r:r + 1, :] = jnp.sum(jnp.where(hit, excl, 0.0), axis=0, keepdims=True).astype(I32)
    cnt_ref[...] = jnp.broadcast_to(carry_new, cnt_ref.shape).astype(I32)


def _route(logits_t, router_bias):
    ntok = logits_t.shape[1]
    tt = min(512, ntok)
    return pl.pallas_call(
        functools.partial(_route_kernel, tt=tt),
        grid=(ntok // tt,),
        in_specs=[pl.BlockSpec((N_EXPERTS, tt), lambda i: (0, i)),
                  pl.BlockSpec((N_EXPERTS, 1), lambda i: (0, 0))],
        out_specs=[pl.BlockSpec((TOP_K, tt), lambda i: (0, i)),
                   pl.BlockSpec((TOP_K, tt), lambda i: (0, i)),
                   pl.BlockSpec((TOP_K, tt), lambda i: (0, i)),
                   pl.BlockSpec((N_EXPERTS, 128), lambda i: (0, 0))],
        out_shape=[jax.ShapeDtypeStruct((TOP_K, ntok), I32),
                   jax.ShapeDtypeStruct((TOP_K, ntok), F32),
                   jax.ShapeDtypeStruct((TOP_K, ntok), I32),
                   jax.ShapeDtypeStruct((N_EXPERTS, 128), I32)],
        scratch_shapes=[pltpu.VMEM((N_EXPERTS, 1), F32)],
        compiler_params=_cparams(("arbitrary",)),
    )(logits_t, router_bias.reshape(N_EXPERTS, 1))


def _expert_kernel(be_ref, nu_ref, idx_ref, idxn_ref, w_ref, h_hbm, wg_ref, wu_ref, wd_ref, o_ref,
                   xs, wgb, wub, wdb, sem, *, tm):
    i = pl.program_id(0)
    n_used = nu_ref[0]
    slot = i % 2

    def row_copy(tok, s, r):
        return pltpu.make_async_copy(h_hbm.at[pl.ds(tok, 1)], xs.at[s, pl.ds(r, 1)], sem.at[s])

    def issue(idx_r, s):
        def body(r, carry):
            row_copy(idx_r[0, r], s, r).start()
            return carry
        lax.fori_loop(0, tm, body, 0)

    @pl.when(i == 0)
    def _():
        issue(idx_ref, 0)

    @pl.when(i + 1 < n_used)
    def _():
        issue(idxn_ref, 1 - slot)

    @pl.when(i < n_used)
    def _():
        e = be_ref[i]
        e_prev = be_ref[jnp.maximum(i - 1, 0)]

        @pl.when((i == 0) | (e != e_prev))
        def _():
            wgb[...] = wg_ref[...].astype(BF16)
            wub[...] = wu_ref[...].astype(BF16)
            wdb[...] = wd_ref[...].astype(BF16)

        def wait_body(r, carry):
            row_copy(0, slot, r).wait()
            return carry
        lax.fori_loop(0, tm, wait_body, 0)

        x = xs[slot].astype(BF16)
        a = (_silu(jnp.dot(x, wgb[...], preferred_element_type=F32))
             * jnp.dot(x, wub[...], preferred_element_type=F32)).astype(BF16)
        y = jnp.dot(a, wdb[...], preferred_element_type=F32)
        o_ref[...] = y * w_ref[...]

    @pl.when(i >= n_used)
    def _():
        o_ref[...] = jnp.zeros_like(o_ref)


def _routed_experts(h2, block_e, n_used, slot_tok, slot_w, w_gate, w_up, w_down, tm):
    ntok, d = h2.shape
    n_blocks = block_e.shape[0]
    de = w_gate.shape[-1]
    idx3 = slot_tok.reshape(n_blocks, 1, tm)
    grid_spec = pltpu.PrefetchScalarGridSpec(
        num_scalar_prefetch=2,
        grid=(n_blocks,),
        in_specs=[pl.BlockSpec((None, 1, tm), lambda i, be, nu: (i, 0, 0), memory_space=pltpu.SMEM),
                  pl.BlockSpec((None, 1, tm), lambda i, be, nu: (jnp.minimum(i + 1, n_blocks - 1), 0, 0),
                               memory_space=pltpu.SMEM),
                  pl.BlockSpec((tm, 1), lambda i, be, nu: (i, 0)),
                  pl.BlockSpec(memory_space=pl.ANY),
                  pl.BlockSpec((None, None, d, de), lambda i, be, nu: (0, be[i], 0, 0)),
                  pl.BlockSpec((None, None, d, de), lambda i, be, nu: (0, be[i], 0, 0)),
                  pl.BlockSpec((None, None, de, d), lambda i, be, nu: (0, be[i], 0, 0))],
        out_specs=pl.BlockSpec((tm, d), lambda i, be, nu: (i, 0)),
        scratch_shapes=[pltpu.VMEM((2, tm, d), F32),
                        pltpu.VMEM((d, de), BF16), pltpu.VMEM((d, de), BF16), pltpu.VMEM((de, d), BF16),
                        pltpu.SemaphoreType.DMA((2,))],
    )
    return pl.pallas_call(
        functools.partial(_expert_kernel, tm=tm),
        grid_spec=grid_spec,
        out_shape=jax.ShapeDtypeStruct((n_blocks * tm, d), F32),
        compiler_params=_cparams(("arbitrary",)),
    )(block_e, n_used, idx3, idx3, slot_w.reshape(n_blocks * tm, 1), h2, w_gate, w_up, w_down)


def _shared_kernel(h_ref, x1_ref, g2_ref, sg_ref, su_ref, sd_ref, o_ref):
    h = h_ref[...].astype(BF16)
    a = (_silu(jnp.dot(h, sg_ref[...], preferred_element_type=F32))
         * jnp.dot(h, su_ref[...], preferred_element_type=F32)).astype(BF16)
    o_ref[...] = x1_ref[...] + g2_ref[...] * jnp.dot(a, sd_ref[...], preferred_element_type=F32)


def _shared_expert(h2, x1, gate2, sg, su, sd, seq):
    ntok, d = h2.shape
    de = sg.shape[1]
    tm = min(512, seq)
    per_b = seq // tm
    return pl.pallas_call(
        _shared_kernel,
        grid=(ntok // tm,),
        in_specs=[pl.BlockSpec((tm, d), lambda i: (i, 0)),
                  pl.BlockSpec((tm, d), lambda i: (i, 0)),
                  pl.BlockSpec((None, 1, d), lambda i: (i // per_b, 0, 0)),
                  pl.BlockSpec((d, de), lambda i: (0, 0)),
                  pl.BlockSpec((d, de), lambda i: (0, 0)),
                  pl.BlockSpec((de, d), lambda i: (0, 0))],
        out_specs=pl.BlockSpec((tm, d), lambda i: (i, 0)),
        out_shape=jax.ShapeDtypeStruct((ntok, d), F32),
        compiler_params=_cparams(("arbitrary",)),
    )(h2, x1, gate2[:, None, :], sg, su, sd)


def _combine_kernel(d_ref, dn_ref, ys_hbm, base_ref, g2_ref, o_ref, buf, sem, *, tm):
    i = pl.program_id(0)
    n = pl.num_programs(0)
    slot = i % 2

    def row_copy(src, s, r, t):
        return pltpu.make_async_copy(ys_hbm.at[pl.ds(src, 1)], buf.at[s, r, pl.ds(t, 1)], sem.at[s])

    def issue(d_r, s):
        def body(t, carry):
            for r in range(TOP_K):
                row_copy(d_r[r, t], s, r, t).start()
            return carry
        lax.fori_loop(0, tm, body, 0)

    @pl.when(i == 0)
    def _():
        issue(d_ref, 0)

    @pl.when(i + 1 < n)
    def _():
        issue(dn_ref, 1 - slot)

    def wait_body(t, carry):
        for r in range(TOP_K):
            row_copy(0, slot, r, t).wait()
        return carry
    lax.fori_loop(0, tm, wait_body, 0)

    routed = buf[slot, 0]
    for r in range(1, TOP_K):
        routed = routed + buf[slot, r]
    o_ref[...] = base_ref[...] + g2_ref[...] * routed


def _combine(ys, dest, base, gate2, seq):
    ntok, d = base.shape
    tm = min(128, seq)
    nt = ntok // tm
    per_b = seq // tm
    d3 = dest.reshape(TOP_K, nt, tm).transpose(1, 0, 2)
    return pl.pallas_call(
        functools.partial(_combine_kernel, tm=tm),
        grid=(nt,),
        in_specs=[pl.BlockSpec((None, TOP_K, tm), lambda i: (i, 0, 0), memory_space=pltpu.SMEM),
                  pl.BlockSpec((None, TOP_K, tm), lambda i: (jnp.minimum(i + 1, nt - 1), 0, 0),
                               memory_space=pltpu.SMEM),
                  pl.BlockSpec(memory_space=pl.ANY),
                  pl.BlockSpec((tm, d), lambda i: (i, 0)),
                  pl.BlockSpec((None, 1, d), lambda i: (i // per_b, 0, 0))],
        out_specs=pl.BlockSpec((tm, d), lambda i: (i, 0)),
        out_shape=jax.ShapeDtypeStruct((ntok, d), F32),
        scratch_shapes=[pltpu.VMEM((2, TOP_K, tm, d), F32), pltpu.SemaphoreType.DMA((2,))],
        compiler_params=_cparams(("arbitrary",)),
    )(d3, d3, ys, base, gate2[:, None, :])


def _dispatch_plan(top_e, gate_w, rank, counts, tm):
    ntok = top_e.shape[1]
    n_blocks = (ntok * TOP_K) // tm + N_EXPERTS
    nblk = (counts + tm - 1) // tm
    blk_end = jnp.cumsum(nblk)
    pad_start = (blk_end - nblk) * tm
    n_used = blk_end[-1:].astype(I32)
    block_e = jnp.minimum(jnp.searchsorted(blk_end, jnp.arange(n_blocks, dtype=I32), side='right'),
                          N_EXPERTS - 1).astype(I32)
    dest = pad_start[top_e] + rank
    tok = jnp.broadcast_to(jnp.arange(ntok, dtype=I32)[None, :], dest.shape)
    flat = dest.reshape(-1)
    slot_tok = jnp.zeros((n_blocks * tm,), I32).at[flat].set(tok.reshape(-1))
    slot_w = jnp.zeros((n_blocks * tm,), F32).at[flat].set(gate_w.reshape(-1))
    return block_e, n_used, dest, slot_tok, slot_w


def kernel(x, c, ada_w, ada_b, norm1_g, w_in, lb_logits, hg_norm_g, q_norm_g, k_norm_g, lambda_q1, lambda_k1,
           lambda_q2, lambda_k2, da_norm_g, rel_bias, w_branch_a, w_branch_b, w_out, norm2_g, router_w,
           router_bias, w_exp_gate, w_exp_up, w_exp_down, w_sh_gate, w_sh_up, w_sh_down):
    bsz, seq, d = x.shape
    ntok = bsz * seq
    l = 0
    x2 = x.reshape(ntok, d)

    mod = _ada_mod(c, ada_w[l], ada_b[l])
    shift1, scale1, gate1, shift2, scale2, gate2 = jnp.split(mod, 6, axis=-1)

    n_gate = 2 * d
    split = w_in.shape[2] - n_gate
    w_in_bf = jnp.concatenate([w_in[l][:, split:], w_in[l][:, :split]], axis=1).astype(BF16)
    proj = _in_projection(x2, scale1, shift1, norm1_g[l], w_in_bf, seq)
    col_hg = n_gate
    col_q = col_hg + 4 * HG_HEADS * HEAD_W
    col_k = col_q + DA_HEADS * HEAD_W
    col_v = col_k + DA_HEADS * HEAD_W

    o_hg = _hgrn_branch(proj, lb_logits, hg_norm_g[l], bsz, seq, col_hg)
    qn, kn = _qk_norm(proj, q_norm_g[l], k_norm_g[l], col_q, col_k)
    lam_params = jnp.stack([lambda_q1[l], lambda_k1[l], lambda_q2[l], lambda_k2[l]])
    o_da = _diff_attention(qn, kn, proj, col_v, rel_bias, lam_params, da_norm_g[l], bsz, seq)

    wr_t = router_w[l].T
    wr_hi = wr_t.astype(BF16)
    wr_lo = (wr_t - wr_hi.astype(F32)).astype(BF16)
    x1, h2, logits_t = _merge_out(o_hg, o_da, proj, 0, d, x2, gate1, scale2, shift2, norm2_g[l],
                                  w_branch_a[l].astype(BF16), w_branch_b[l].astype(BF16),
                                  w_out[l].astype(BF16), wr_hi, wr_lo, seq)

    top_e, gate_w, rank, counts = _route(logits_t, router_bias[l])
    tm_e = 256
    block_e, n_used, dest, slot_tok, slot_w = _dispatch_plan(top_e, gate_w, rank, counts[:, 0], tm_e)
    ys = _routed_experts(h2, block_e, n_used, slot_tok, slot_w, w_exp_gate, w_exp_up, w_exp_down, tm_e)
    base = _shared_expert(h2, x1, gate2, w_sh_gate[l].astype(BF16), w_sh_up[l].astype(BF16),
                          w_sh_down[l].astype(BF16), seq)
    out = _combine(ys, dest, base, gate2, seq)
    return out.reshape(bsz, seq, d)
```

```python
import functools
import math

import numpy as np
import jax
import jax.numpy as jnp
from jax import lax
from jax.experimental import pallas as pl
from jax.experimental.pallas import tpu as pltpu

F32 = jnp.float32
BF16 = jnp.bfloat16
I32 = jnp.int32
U32 = jnp.uint32

HG_HEADS = 8
HG_DK = 128
HG_DV = 128
DA_HEADS = 8
DA_DH = 64
DA_DV = 128
RP_BUCKETS = 32
RP_MAX_EXACT = 16
RP_MAX_DIST = 128
N_EXPERTS = 64
N_GROUPS = 8
TOPK_GROUPS = 4
TOP_K = 8
ROUTE_SCALE = 2.5
EPS = 1e-6
LAMBDA_INIT = 0.8 - 0.6 * math.exp(-0.3 * 0)

HEAD_W = 128
VMEM_LIMIT_BYTES = 56 * 1024 * 1024
NEG_BIG = -1e30
LOG2E = math.log2(math.e)

ATTN_TILE = 512
HGRN_TILE = 256
MERGE_TILE = 256
EXPERT_TILE = 256
COMBINE_TILE = 256
ROUTE_TILE = 512

NT_DIMS = (((1,), (1,)), ((), ()))
TN_DIMS = (((0,), (0,)), ((), ()))


def _cparams(sem, **kw):
    return pltpu.CompilerParams(dimension_semantics=sem, vmem_limit_bytes=VMEM_LIMIT_BYTES, **kw)


def _silu(x):
    return x * jax.nn.sigmoid(x)


HI_MASK = np.uint32(0xFFFF0000)


def _pack_bf16_pair(a, b):
    ua = lax.bitcast_convert_type(a.astype(BF16).astype(F32), U32)
    ub = lax.bitcast_convert_type(b.astype(BF16).astype(F32), U32)
    return (ua >> 16) | (ub & HI_MASK)


def _unpack_bf16_pair(w):
    lo = lax.bitcast_convert_type(w << 16, F32)
    hi = lax.bitcast_convert_type(w & HI_MASK, F32)
    return lo, hi


def _t5_bucket_starts():
    n = np.arange(0, RP_MAX_DIST + 1)
    nf = np.maximum(n, 1).astype(np.float32)
    large = RP_MAX_EXACT + (np.log(nf / np.float32(RP_MAX_EXACT)) / np.float32(math.log(RP_MAX_DIST / RP_MAX_EXACT))
                            * np.float32(RP_BUCKETS - RP_MAX_EXACT)).astype(np.int32)
    large = np.minimum(large, RP_BUCKETS - 1)
    bucket = np.where(n < RP_MAX_EXACT, n, large)
    assert np.all(np.diff(bucket) >= 0) and bucket[-1] == RP_BUCKETS - 1
    return [int(np.argmax(bucket >= j)) for j in range(RP_BUCKETS)]


BUCKET_START = _t5_bucket_starts()


def _ada_kernel(c_ref, w_ref, b_ref, o_ref):
    ca = _silu(c_ref[...]).astype(BF16)
    o_ref[...] = jnp.dot(ca, w_ref[...].astype(BF16), preferred_element_type=F32) + b_ref[...]


def _ada_mod(c, ada_w, ada_b):
    bsz, d = c.shape
    n = ada_w.shape[1]
    rows = 8
    cp = jnp.zeros((rows, d), F32).at[:bsz].set(c)
    tn = 1024
    out = pl.pallas_call(
        _ada_kernel,
        grid=(n // tn,),
        in_specs=[pl.BlockSpec((rows, d), lambda j: (0, 0)),
                  pl.BlockSpec((d, tn), lambda j: (0, j)),
                  pl.BlockSpec((1, tn), lambda j: (0, j))],
        out_specs=pl.BlockSpec((rows, tn), lambda j: (0, j)),
        out_shape=jax.ShapeDtypeStruct((rows, n), F32),
        compiler_params=_cparams(("arbitrary",)),
    )(cp, ada_w, ada_b.reshape(1, n))
    return out[:bsz]


def _inproj_kernel(x_ref, sc_ref, sh_ref, g_ref, w_ref, o_ref, h_scr):
    @pl.when(pl.program_id(1) == 0)
    def _():
        x = x_ref[...]
        ms = jnp.mean(x * x, axis=-1, keepdims=True)
        hn = x * lax.rsqrt(ms + EPS) * g_ref[...]
        h_scr[...] = (hn * (1.0 + sc_ref[...]) + sh_ref[...]).astype(BF16)

    o_ref[...] = jnp.dot(h_scr[...], w_ref[...], preferred_element_type=F32).astype(o_ref.dtype)


def _in_projection(x2, scale, shift, g, w_bf, seq):
    ntok, d = x2.shape
    n = w_bf.shape[1]
    tm = min(1024, seq)
    tn = 1024
    per_b = seq // tm
    return pl.pallas_call(
        _inproj_kernel,
        grid=(ntok // tm, n // tn),
        in_specs=[pl.BlockSpec((tm, d), lambda i, j: (i, 0)),
                  pl.BlockSpec((None, 1, d), lambda i, j: (i // per_b, 0, 0)),
                  pl.BlockSpec((None, 1, d), lambda i, j: (i // per_b, 0, 0)),
                  pl.BlockSpec((1, d), lambda i, j: (0, 0)),
                  pl.BlockSpec((d, tn), lambda i, j: (0, j))],
        out_specs=pl.BlockSpec((tm, tn), lambda i, j: (i, j)),
        out_shape=jax.ShapeDtypeStruct((ntok, n), BF16),
        scratch_shapes=[pltpu.VMEM((tm, d), BF16)],
        compiler_params=_cparams(("arbitrary", "arbitrary")),
    )(x2, scale[:, None, :], shift[:, None, :], g.reshape(1, d), w_bf)


def _hgrn_kernel(q_ref, f_ref, i_ref, g_ref, lbl_ref, ng_ref, o_ref, state_ref, *, tile):
    @pl.when(pl.program_id(1) == 0)
    def _():
        state_ref[...] = jnp.zeros_like(state_ref)

    n_levels = tile.bit_length() - 1
    row = lax.broadcasted_iota(I32, (tile, tile), 0)
    col = lax.broadcasted_iota(I32, (tile, tile), 1)
    lev = jnp.where(row >= col, 31 - lax.clz(row ^ col), -2)
    rowk = lax.broadcasted_iota(I32, (tile, HG_DK), 0)

    for h in range(HG_HEADS):
        sl = slice(h * HEAD_W, (h + 1) * HEAD_W)
        q = _silu(q_ref[:, sl].astype(F32)) * (HG_DK ** -0.5)
        ll = lbl_ref[:, sl]
        el = jnp.exp(ll - jnp.max(ll, axis=0, keepdims=True))
        lb = el[0:1, :] / jnp.sum(el, axis=0, keepdims=True)
        fg = lb + (1.0 - lb) * jax.nn.sigmoid(f_ref[:, sl].astype(F32))
        k = 1.0 - fg
        g = jnp.log(fg)
        v = i_ref[:, sl]

        scores = jnp.where(lev == -1,
                           lax.dot_general(q.astype(BF16), k.astype(BF16), NT_DIMS, preferred_element_type=F32),
                           0.0)
        c = g
        e = g
        for lvl in range(n_levels):
            blk = 1 << lvl
            qd = (q * jnp.exp(c)).astype(BF16)
            kd = (k * jnp.exp(e - c)).astype(BF16)
            s_l = lax.dot_general(qd, kd, NT_DIMS, preferred_element_type=F32)
            scores = jnp.where(lev == lvl, s_l, scores)
            odd = (rowk & blk) != 0
            e_prev = pltpu.roll(e, blk, axis=0)
            e_next = pltpu.roll(e, tile - blk, axis=0)
            c = c + jnp.where(odd, e_prev, 0.0)
            e = e + jnp.where(odd, e_prev, e_next)
        st = state_ref[h]
        qb = (q * jnp.exp(c)).astype(BF16)
        o = (jnp.dot(scores.astype(BF16), v, preferred_element_type=F32)
             + lax.dot_general(qb, st.astype(BF16), NT_DIMS, preferred_element_type=F32))
        kd = (k * jnp.exp(e - c)).astype(BF16)
        state_ref[h] = st * jnp.exp(e[0:1, :]) + lax.dot_general(v, kd, TN_DIMS, preferred_element_type=F32)

        ms = jnp.mean(o * o, axis=-1, keepdims=True)
        gate = _silu(g_ref[:, sl].astype(F32))
        o_ref[:, sl] = (o * lax.rsqrt(ms + EPS) * ng_ref[...] * gate).astype(o_ref.dtype)


def _hgrn_branch(proj, lb_logits, norm_g, bsz, seq, col0):
    ntok = proj.shape[0]
    tile = min(HGRN_TILE, seq)
    nt = seq // tile
    width = HG_HEADS * HEAD_W
    cb = col0 // width

    def spec(off):
        return pl.BlockSpec((tile, width), lambda b, t, off=off: (b * nt + t, cb + off))

    return pl.pallas_call(
        functools.partial(_hgrn_kernel, tile=tile),
        grid=(bsz, nt),
        in_specs=[spec(0), spec(1), spec(2), spec(3),
                  pl.BlockSpec(lb_logits.shape, lambda b, t: (0, 0)),
                  pl.BlockSpec((1, HG_DV), lambda b, t: (0, 0))],
        out_specs=pl.BlockSpec((tile, width), lambda b, t: (b * nt + t, 0)),
        out_shape=jax.ShapeDtypeStruct((ntok, width), BF16),
        scratch_shapes=[pltpu.VMEM((HG_HEADS, HG_DV, HG_DK), F32)],
        compiler_params=_cparams(("arbitrary", "arbitrary")),
    )(proj, proj, proj, proj, lb_logits, norm_g.reshape(1, HG_DV))


def _qknorm_kernel(q_ref, k_ref, qg_ref, kg_ref, qo_ref, ko_ref):
    lane = lax.broadcasted_iota(I32, (q_ref.shape[0], HEAD_W), 1)
    lo = lane < DA_DH

    def norm(x_ref, g_ref, o_ref, scale):
        for h in range(DA_HEADS):
            sl = slice(h * HEAD_W, (h + 1) * HEAD_W)
            x = x_ref[:, sl].astype(F32)
            xx = x * x
            s0 = jnp.sum(jnp.where(lo, xx, 0.0), axis=-1, keepdims=True)
            s1 = jnp.sum(jnp.where(lo, 0.0, xx), axis=-1, keepdims=True)
            ms = jnp.where(lo, s0, s1) * (1.0 / DA_DH)
            o_ref[:, sl] = (x * lax.rsqrt(ms + EPS) * g_ref[...] * scale).astype(o_ref.dtype)

    norm(q_ref, qg_ref, qo_ref, DA_DH ** -0.5 * LOG2E)
    norm(k_ref, kg_ref, ko_ref, 1.0)


def _qk_norm(proj, q_g, k_g, col_q, col_k):
    ntok = proj.shape[0]
    width = DA_HEADS * HEAD_W
    tm = min(512, ntok)
    qg = jnp.tile(q_g, 2).reshape(1, HEAD_W)
    kg = jnp.tile(k_g, 2).reshape(1, HEAD_W)
    return pl.pallas_call(
        _qknorm_kernel,
        grid=(ntok // tm,),
        in_specs=[pl.BlockSpec((tm, width), lambda i: (i, col_q // width)),
                  pl.BlockSpec((tm, width), lambda i: (i, col_k // width)),
                  pl.BlockSpec((1, HEAD_W), lambda i: (0, 0)),
                  pl.BlockSpec((1, HEAD_W), lambda i: (0, 0))],
        out_specs=[pl.BlockSpec((tm, width), lambda i: (i, 0)),
                   pl.BlockSpec((tm, width), lambda i: (i, 0))],
        out_shape=[jax.ShapeDtypeStruct((ntok, width), BF16)] * 2,
        compiler_params=_cparams(("arbitrary",)),
    )(proj, proj, qg, kg)


def _attn_kernel(qi_ref, ki_ref, q_ref, k_ref, v_ref, rel_ref, lamp_ref, ng_ref, o_ref,
                 m_scr, acc_scr, bias_scr, *, tile):
    b = pl.program_id(0)
    p = pl.program_id(1)
    qi = qi_ref[p]
    ki = ki_ref[p]
    diff = qi - ki

    row = lax.broadcasted_iota(I32, (tile, tile), 0)
    col = lax.broadcasted_iota(I32, (tile, tile), 1)

    @pl.when((b == 0) & (p == 0))
    def _():
        def per_head(h, carry):
            for d in range(2):
                dist = row - col + d * tile
                bias = jnp.full((tile, tile), rel_ref[0, h], F32)
                for j in range(1, RP_BUCKETS):
                    bias = jnp.where(dist >= BUCKET_START[j], rel_ref[j, h], bias)
                bias_scr[d, h] = bias * LOG2E
            return carry
        lax.fori_loop(0, DA_HEADS, per_head, 0)

    @pl.when(ki == 0)
    def _():
        m_scr[...] = jnp.full_like(m_scr, NEG_BIG)
        acc_scr[...] = jnp.zeros_like(acc_scr)

    lane = lax.broadcasted_iota(I32, (tile, HEAD_W), 1)
    lo = lane < DA_DH
    ones = jnp.ones((tile, HEAD_W), BF16)
    reps = tile // HEAD_W

    def step(kind):
        for h in range(DA_HEADS):
            sl = slice(h * HEAD_W, (h + 1) * HEAD_W)
            q = q_ref[:, sl]
            k = k_ref[:, sl]
            v_aug = jnp.concatenate([v_ref[:, sl], ones], axis=1)
            for c in range(2):
                qc = jnp.where(lo, q, jnp.zeros_like(q)) if c == 0 else jnp.where(lo, jnp.zeros_like(q), q)
                s = lax.dot_general(qc, k, NT_DIMS, preferred_element_type=F32)
                if kind == 2:
                    s = s + rel_ref[RP_BUCKETS - 1, h] * LOG2E
                else:
                    s = s + bias_scr[kind, h]
                if kind == 0:
                    s = jnp.where(row >= col, s, NEG_BIG)
                idx = 2 * h + c
                m_old = m_scr[idx]
                m_cur = jnp.broadcast_to(jnp.max(s, axis=-1, keepdims=True), (tile, HEAD_W))
                m_new = jnp.maximum(m_old, m_cur)
                alpha = jnp.exp2(m_old - m_new)
                pr = jnp.exp2(s - jnp.concatenate([m_new] * reps, axis=1))
                pv = jnp.dot(pr.astype(BF16), v_aug, preferred_element_type=F32)
                acc_scr[idx] = jnp.concatenate([alpha, alpha], axis=1) * acc_scr[idx] + pv
                m_scr[idx] = m_new

    for kind, cond in ((0, diff == 0), (1, diff == 1), (2, diff > 1)):
        pl.when(cond)(functools.partial(step, kind))

    @pl.when(diff == 0)
    def _():
        lp = lamp_ref[...]
        lam = (jnp.exp(jnp.sum(lp[0:1] * lp[1:2], axis=-1, keepdims=True))
               - jnp.exp(jnp.sum(lp[2:3] * lp[3:4], axis=-1, keepdims=True)) + LAMBDA_INIT)
        for h in range(DA_HEADS):
            sl = slice(h * HEAD_W, (h + 1) * HEAD_W)
            a0 = acc_scr[2 * h]
            a1 = acc_scr[2 * h + 1]
            o = a0[:, :DA_DV] / a0[:, DA_DV:] - lam * (a1[:, :DA_DV] / a1[:, DA_DV:])
            ms = jnp.mean(o * o, axis=-1, keepdims=True)
            o_ref[:, sl] = (o * lax.rsqrt(ms + EPS) * ng_ref[...] * (1.0 - LAMBDA_INIT)).astype(o_ref.dtype)


def _diff_attention(qn, kn, proj, col_v, rel_bias, lam_params, norm_g, bsz, seq):
    ntok = qn.shape[0]
    tile = min(ATTN_TILE, seq)
    nq = seq // tile
    width = DA_HEADS * HEAD_W
    qi_list, ki_list = [], []
    for a in range(nq):
        for c in range(a + 1):
            qi_list.append(a)
            ki_list.append(c)
    qi_arr = jnp.asarray(qi_list, I32)
    ki_arr = jnp.asarray(ki_list, I32)
    cv = col_v // width
    grid_spec = pltpu.PrefetchScalarGridSpec(
        num_scalar_prefetch=2,
        grid=(bsz, len(qi_list)),
        in_specs=[pl.BlockSpec((tile, width), lambda b, p, qi, ki: (b * nq + qi[p], 0)),
                  pl.BlockSpec((tile, width), lambda b, p, qi, ki: (b * nq + ki[p], 0)),
                  pl.BlockSpec((tile, width), lambda b, p, qi, ki: (b * nq + ki[p], cv)),
                  pl.BlockSpec(memory_space=pltpu.SMEM),
                  pl.BlockSpec((4, DA_DH), lambda b, p, qi, ki: (0, 0)),
                  pl.BlockSpec((1, DA_DV), lambda b, p, qi, ki: (0, 0))],
        out_specs=pl.BlockSpec((tile, width), lambda b, p, qi, ki: (b * nq + qi[p], 0)),
        scratch_shapes=[pltpu.VMEM((2 * DA_HEADS, tile, HEAD_W), F32),
                        pltpu.VMEM((2 * DA_HEADS, tile, 2 * DA_DV), F32),
                        pltpu.VMEM((2, DA_HEADS, tile, tile), F32)],
    )
    return pl.pallas_call(
        functools.partial(_attn_kernel, tile=tile),
        grid_spec=grid_spec,
        out_shape=jax.ShapeDtypeStruct((ntok, width), BF16),
        compiler_params=_cparams(("arbitrary", "arbitrary")),
    )(qi_arr, ki_arr, qn, kn, proj, rel_bias, lam_params, norm_g.reshape(1, DA_DV))


def _merge_kernel(oa_ref, ob_ref, ga_ref, gb_ref, x_ref, g1_ref, sc_ref, sh_ref, g2_ref, n2_ref,
                  wa_ref, wb_ref, wo_ref, wrh_ref, wrl_ref, sg_ref, su_ref, sd_ref,
                  base_ref, hp_ref, lg_ref):
    ya = jnp.dot(oa_ref[...], wa_ref[...], preferred_element_type=F32)
    yb = jnp.dot(ob_ref[...], wb_ref[...], preferred_element_type=F32)
    merged = (jax.nn.sigmoid(ga_ref[...].astype(F32)) * ya
              + jax.nn.sigmoid(gb_ref[...].astype(F32)) * yb).astype(BF16)
    y = jnp.dot(merged, wo_ref[...], preferred_element_type=F32)
    x1 = x_ref[...] + g1_ref[...] * y
    ms = jnp.mean(x1 * x1, axis=-1, keepdims=True)
    h2 = (x1 * lax.rsqrt(ms + EPS) * n2_ref[...]) * (1.0 + sc_ref[...]) + sh_ref[...]
    half = h2.shape[1] // 2
    hp_ref[...] = _pack_bf16_pair(h2[:, :half], h2[:, half:])
    hh = h2.astype(BF16)
    a = (_silu(jnp.dot(hh, sg_ref[...], preferred_element_type=F32))
         * jnp.dot(hh, su_ref[...], preferred_element_type=F32)).astype(BF16)
    base_ref[...] = x1 + g2_ref[...] * jnp.dot(a, sd_ref[...], preferred_element_type=F32)
    hl = (h2 - hh.astype(F32)).astype(BF16)
    lg_ref[...] = (lax.dot_general(wrh_ref[...], hh, NT_DIMS, preferred_element_type=F32)
                   + lax.dot_general(wrh_ref[...], hl, NT_DIMS, preferred_element_type=F32)
                   + lax.dot_general(wrl_ref[...], hh, NT_DIMS, preferred_element_type=F32))


def _merge_out(o_hg, o_da, proj, col_ga, col_gb, x2, gate1, scale2, shift2, gate2, norm2_g,
               wa, wb, wo, wr_hi, wr_lo, sg, su, sd, seq):
    ntok, d = x2.shape
    tm = min(MERGE_TILE, seq)
    per_b = seq // tm
    wa_w = o_hg.shape[1]
    wb_w = o_da.shape[1]
    de = sg.shape[1]

    def const(shape):
        return pl.BlockSpec(shape, lambda i: (0,) * len(shape), pipeline_mode=pl.Buffered(1))

    def perb():
        return pl.BlockSpec((None, 1, d), lambda i: (i // per_b, 0, 0))

    return pl.pallas_call(
        _merge_kernel,
        grid=(ntok // tm,),
        in_specs=[pl.BlockSpec((tm, wa_w), lambda i: (i, 0)),
                  pl.BlockSpec((tm, wb_w), lambda i: (i, 0)),
                  pl.BlockSpec((tm, d), lambda i: (i, col_ga // d)),
                  pl.BlockSpec((tm, d), lambda i: (i, col_gb // d)),
                  pl.BlockSpec((tm, d), lambda i: (i, 0)),
                  perb(), perb(), perb(), perb(),
                  const((1, d)),
                  const((wa_w, d)), const((wb_w, d)), const((d, d)),
                  const((N_EXPERTS, d)), const((N_EXPERTS, d)),
                  const((d, de)), const((d, de)), const((de, d))],
        out_specs=[pl.BlockSpec((tm, d), lambda i: (i, 0)),
                   pl.BlockSpec((tm, d // 2), lambda i: (i, 0)),
                   pl.BlockSpec((N_EXPERTS, tm), lambda i: (0, i))],
        out_shape=[jax.ShapeDtypeStruct((ntok, d), F32),
                   jax.ShapeDtypeStruct((ntok, d // 2), U32),
                   jax.ShapeDtypeStruct((N_EXPERTS, ntok), F32)],
        compiler_params=_cparams(("arbitrary",)),
    )(o_hg, o_da, proj, proj, x2, gate1[:, None, :], scale2[:, None, :], shift2[:, None, :],
      gate2[:, None, :], norm2_g.reshape(1, d), wa, wb, wo, wr_hi, wr_lo, sg, su, sd)


def _route_kernel(lg_ref, rb_ref, te_ref, gw_ref, rk_ref, cnt_ref, carry_scr, *, tt):
    @pl.when(pl.program_id(0) == 0)
    def _():
        carry_scr[...] = jnp.zeros_like(carry_scr)

    per_g = N_EXPERTS // N_GROUPS
    scores = jax.nn.sigmoid(lg_ref[...])
    sel = scores + rb_ref[...]
    sel3 = sel.reshape(N_GROUPS, per_g, tt)
    j_io = lax.broadcasted_iota(I32, (N_GROUPS, per_g, tt), 1)
    m1 = jnp.max(sel3, axis=1, keepdims=True)
    i1 = jnp.min(jnp.where(sel3 == m1, j_io, per_g), axis=1, keepdims=True)
    m2 = jnp.max(jnp.where(j_io == i1, -jnp.inf, sel3), axis=1, keepdims=True)
    gs = (m1 + m2).reshape(N_GROUPS, tt)
    g_io = lax.broadcasted_iota(I32, (N_GROUPS, tt), 0)
    gmask = jnp.zeros((N_GROUPS, tt), jnp.bool_)
    for _ in range(TOPK_GROUPS):
        gm = jnp.max(gs, axis=0, keepdims=True)
        gi = jnp.min(jnp.where(gs == gm, g_io, N_GROUPS), axis=0, keepdims=True)
        hit = g_io == gi
        gmask = gmask | hit
        gs = jnp.where(hit, -jnp.inf, gs)
    emask = jnp.broadcast_to(gmask.reshape(N_GROUPS, 1, tt), (N_GROUPS, per_g, tt)).reshape(N_EXPERTS, tt)
    cand = jnp.where(emask, sel, -jnp.inf)
    e_io = lax.broadcasted_iota(I32, (N_EXPERTS, tt), 0)
    chosen = jnp.zeros((N_EXPERTS, tt), jnp.bool_)
    picks = []
    for _ in range(TOP_K):
        em = jnp.max(cand, axis=0, keepdims=True)
        ei = jnp.min(jnp.where(cand == em, e_io, N_EXPERTS), axis=0, keepdims=True)
        hit = e_io == ei
        chosen = chosen | hit
        cand = jnp.where(hit, -jnp.inf, cand)
        picks.append((ei, hit))
    gsel = jnp.where(chosen, scores, 0.0)
    wnorm = gsel / jnp.sum(gsel, axis=0, keepdims=True) * ROUTE_SCALE
    ch = jnp.where(chosen, 1.0, 0.0)
    tri = (lax.broadcasted_iota(I32, (tt, tt), 0) <= lax.broadcasted_iota(I32, (tt, tt), 1))
    incl = jnp.dot(ch.astype(BF16), jnp.where(tri, 1.0, 0.0).astype(BF16), preferred_element_type=F32)
    carry = carry_scr[...]
    excl = incl - ch + carry
    carry_new = carry + incl[:, tt - 1:tt]
    carry_scr[...] = carry_new
    for r, (ei, hit) in enumerate(picks):
        te_ref[r:r + 1, :] = ei
        gw_ref[r:r + 1, :] = jnp.sum(jnp.where(hit, wnorm, 0.0), axis=0, keepdims=True)
        rk_ref[r:r + 1, :] = jnp.sum(jnp.where(hit, excl, 0.0), axis=0, keepdims=True).astype(I32)
    cnt_ref[...] = jnp.broadcast_to(carry_new, cnt_ref.shape).astype(I32)


def _route(logits_t, router_bias):
    ntok = logits_t.shape[1]
    tt = min(ROUTE_TILE, ntok)
    return pl.pallas_call(
        functools.partial(_route_kernel, tt=tt),
        grid=(ntok // tt,),
        in_specs=[pl.BlockSpec((N_EXPERTS, tt), lambda i: (0, i)),
                  pl.BlockSpec((N_EXPERTS, 1), lambda i: (0, 0))],
        out_specs=[pl.BlockSpec((TOP_K, tt), lambda i: (0, i)),
                   pl.BlockSpec((TOP_K, tt), lambda i: (0, i)),
                   pl.BlockSpec((TOP_K, tt), lambda i: (0, i)),
                   pl.BlockSpec((N_EXPERTS, 128), lambda i: (0, 0))],
        out_shape=[jax.ShapeDtypeStruct((TOP_K, ntok), I32),
                   jax.ShapeDtypeStruct((TOP_K, ntok), F32),
                   jax.ShapeDtypeStruct((TOP_K, ntok), I32),
                   jax.ShapeDtypeStruct((N_EXPERTS, 128), I32)],
        scratch_shapes=[pltpu.VMEM((N_EXPERTS, 1), F32)],
        compiler_params=_cparams(("arbitrary",)),
    )(logits_t, router_bias.reshape(N_EXPERTS, 1))


def _slots_kernel(ps_ref, te_ref, rk_ref, d_ref):
    te = te_ref[...]
    dest = rk_ref[...]
    for e in range(N_EXPERTS):
        dest = dest + jnp.where(te == e, ps_ref[e], 0)
    d_ref[...] = dest


def _slots(pad_start, top_e, rank):
    ntok = top_e.shape[1]
    tt = min(2048, ntok)
    return pl.pallas_call(
        _slots_kernel,
        grid=(ntok // tt,),
        in_specs=[pl.BlockSpec(memory_space=pltpu.SMEM),
                  pl.BlockSpec((TOP_K, tt), lambda i: (0, i)),
                  pl.BlockSpec((TOP_K, tt), lambda i: (0, i))],
        out_specs=pl.BlockSpec((TOP_K, tt), lambda i: (0, i)),
        out_shape=jax.ShapeDtypeStruct((TOP_K, ntok), I32),
        compiler_params=_cparams(("arbitrary",)),
    )(pad_start, top_e, rank)


GATHER_UNROLL = 8


def _expert_kernel(be_ref, nu_ref, idx_ref, idxn_ref, h_hbm, wg_ref, wu_ref, wd_ref, o_ref,
                   xs, wgb, wub, wdb, sem, *, tm):
    i = pl.program_id(0)
    n_used = nu_ref[0]
    slot = i % 2

    def row_copy(tok, s, r):
        return pltpu.make_async_copy(h_hbm.at[pl.ds(tok, 1)], xs.at[s, pl.ds(r, 1)], sem.at[s])

    def issue(idx_r, s):
        def body(g, carry):
            for u in range(GATHER_UNROLL):
                r = g * GATHER_UNROLL + u
                row_copy(idx_r[0, r], s, r).start(priority=u % 2)
            return carry
        lax.fori_loop(0, tm // GATHER_UNROLL, body, 0)

    @pl.when(i == 0)
    def _():
        issue(idx_ref, 0)

    @pl.when(i + 1 < n_used)
    def _():
        issue(idxn_ref, 1 - slot)

    @pl.when(i < n_used)
    def _():
        e = be_ref[i]
        e_prev = be_ref[jnp.maximum(i - 1, 0)]

        @pl.when((i == 0) | (e != e_prev))
        def _():
            wgb[...] = wg_ref[...].astype(BF16)
            wub[...] = wu_ref[...].astype(BF16)
            wdb[...] = wd_ref[...].astype(BF16)

        def wait_body(g, carry):
            for u in range(GATHER_UNROLL):
                row_copy(0, slot, g * GATHER_UNROLL + u).wait()
            return carry
        lax.fori_loop(0, tm // GATHER_UNROLL, wait_body, 0)

        xlo, xhi = _unpack_bf16_pair(xs[slot])
        xlo = xlo.astype(BF16)
        xhi = xhi.astype(BF16)
        half = xlo.shape[1]

        def proj_in(w):
            return (jnp.dot(xlo, w[:half, :], preferred_element_type=F32)
                    + jnp.dot(xhi, w[half:, :], preferred_element_type=F32))

        a = (_silu(proj_in(wgb)) * proj_in(wub)).astype(BF16)
        y = jnp.dot(a, wdb[...], preferred_element_type=F32)
        o_ref[...] = _pack_bf16_pair(y[:, :half], y[:, half:])

    @pl.when(i >= n_used)
    def _():
        o_ref[...] = jnp.zeros_like(o_ref)


def _routed_experts(h2p, slot_tok, block_e, n_used, w_gate, w_up, w_down, tm):
    n_blocks = block_e.shape[0]
    d, de = w_gate.shape[-2:]
    dw = h2p.shape[1]
    idx3 = slot_tok.reshape(n_blocks, 1, tm)

    def wspec(shape):
        return pl.BlockSpec((None, None) + shape, lambda i, be, nu: (0, be[i], 0, 0))

    grid_spec = pltpu.PrefetchScalarGridSpec(
        num_scalar_prefetch=2,
        grid=(n_blocks,),
        in_specs=[pl.BlockSpec((None, 1, tm), lambda i, be, nu: (i, 0, 0), memory_space=pltpu.SMEM),
                  pl.BlockSpec((None, 1, tm), lambda i, be, nu: (jnp.minimum(i + 1, n_blocks - 1), 0, 0),
                               memory_space=pltpu.SMEM),
                  pl.BlockSpec(memory_space=pl.ANY),
                  wspec((d, de)), wspec((d, de)), wspec((de, d))],
        out_specs=pl.BlockSpec((tm, dw), lambda i, be, nu: (i, 0)),
        scratch_shapes=[pltpu.VMEM((2, tm, dw), U32),
                        pltpu.VMEM((d, de), BF16), pltpu.VMEM((d, de), BF16), pltpu.VMEM((de, d), BF16),
                        pltpu.SemaphoreType.DMA((2,))],
    )
    return pl.pallas_call(
        functools.partial(_expert_kernel, tm=tm),
        grid_spec=grid_spec,
        out_shape=jax.ShapeDtypeStruct((n_blocks * tm, dw), U32),
        compiler_params=_cparams(("arbitrary",), disable_bounds_checks=True),
    )(block_e, n_used, idx3, idx3, h2p, w_gate, w_up, w_down)


def _combine_kernel(d_ref, dn_ref, ys_hbm, gw_ref, base_ref, g2_ref, o_ref, buf, sem, *, tm):
    i = pl.program_id(0)
    n = pl.num_programs(0)
    slot = i % 2

    def row_copy(src, s, r, t):
        return pltpu.make_async_copy(ys_hbm.at[pl.ds(src, 1)], buf.at[s, r, pl.ds(t, 1)], sem.at[s])

    def issue(d_r, s):
        def body(t, carry):
            for r in range(TOP_K):
                row_copy(d_r[r, t], s, r, t).start(priority=r % 2)
            return carry
        lax.fori_loop(0, tm, body, 0, unroll=4)

    @pl.when(i == 0)
    def _():
        issue(d_ref, 0)

    @pl.when(i + 1 < n)
    def _():
        issue(dn_ref, 1 - slot)

    def wait_body(t, carry):
        for r in range(TOP_K):
            row_copy(0, slot, r, t).wait()
        return carry
    lax.fori_loop(0, tm, wait_body, 0, unroll=8)

    half = buf.shape[-1]
    acc_lo = jnp.zeros((tm, half), F32)
    acc_hi = jnp.zeros((tm, half), F32)
    for r in range(TOP_K):
        lo, hi = _unpack_bf16_pair(buf[slot, r])
        w = gw_ref[:, r:r + 1]
        acc_lo = acc_lo + w * lo
        acc_hi = acc_hi + w * hi
    o_ref[:, :half] = base_ref[:, :half] + g2_ref[:, :half] * acc_lo
    o_ref[:, half:] = base_ref[:, half:] + g2_ref[:, half:] * acc_hi


def _combine(ys, dest, gate_w, base, gate2, seq):
    ntok, d = base.shape
    dw = ys.shape[1]
    tm = min(COMBINE_TILE, seq)
    nt = ntok // tm
    per_b = seq // tm
    d3 = dest.reshape(TOP_K, nt, tm).transpose(1, 0, 2)
    return pl.pallas_call(
        functools.partial(_combine_kernel, tm=tm),
        grid=(nt,),
        in_specs=[pl.BlockSpec((None, TOP_K, tm), lambda i: (i, 0, 0), memory_space=pltpu.SMEM),
                  pl.BlockSpec((None, TOP_K, tm), lambda i: (jnp.minimum(i + 1, nt - 1), 0, 0),
                               memory_space=pltpu.SMEM),
                  pl.BlockSpec(memory_space=pl.ANY),
                  pl.BlockSpec((tm, TOP_K), lambda i: (i, 0)),
                  pl.BlockSpec((tm, d), lambda i: (i, 0)),
                  pl.BlockSpec((None, 1, d), lambda i: (i // per_b, 0, 0))],
        out_specs=pl.BlockSpec((tm, d), lambda i: (i, 0)),
        out_shape=jax.ShapeDtypeStruct((ntok, d), F32),
        scratch_shapes=[pltpu.VMEM((2, TOP_K, tm, dw), U32), pltpu.SemaphoreType.DMA((2,))],
        compiler_params=_cparams(("arbitrary",), disable_bounds_checks=True),
    )(d3, d3, ys, gate_w.T, base, gate2[:, None, :])


def kernel(x, c, ada_w, ada_b, norm1_g, w_in, lb_logits, hg_norm_g, q_norm_g, k_norm_g, lambda_q1, lambda_k1,
           lambda_q2, lambda_k2, da_norm_g, rel_bias, w_branch_a, w_branch_b, w_out, norm2_g, router_w,
           router_bias, w_exp_gate, w_exp_up, w_exp_down, w_sh_gate, w_sh_up, w_sh_down):
    bsz, seq, d = x.shape
    ntok = bsz * seq
    l = 0
    x2 = x.reshape(ntok, d)

    mod = _ada_mod(c, ada_w[l], ada_b[l])
    shift1, scale1, gate1, shift2, scale2, gate2 = jnp.split(mod, 6, axis=-1)

    n_gate = 2 * d
    split = w_in.shape[2] - n_gate
    w_in_bf = jnp.concatenate([w_in[l][:, split:], w_in[l][:, :split]], axis=1).astype(BF16)
    proj = _in_projection(x2, scale1, shift1, norm1_g[l], w_in_bf, seq)
    col_hg = n_gate
    col_q = col_hg + 4 * HG_HEADS * HEAD_W
    col_k = col_q + DA_HEADS * HEAD_W
    col_v = col_k + DA_HEADS * HEAD_W

    o_hg = _hgrn_branch(proj, lb_logits, hg_norm_g[l], bsz, seq, col_hg)
    qn, kn = _qk_norm(proj, q_norm_g[l], k_norm_g[l], col_q, col_k)
    lam_params = jnp.stack([lambda_q1[l], lambda_k1[l], lambda_q2[l], lambda_k2[l]])
    o_da = _diff_attention(qn, kn, proj, col_v, rel_bias, lam_params, da_norm_g[l], bsz, seq)

    wr_t = router_w[l].T
    wr_hi = wr_t.astype(BF16)
    wr_lo = (wr_t - wr_hi.astype(F32)).astype(BF16)
    base, h2p, logits_t = _merge_out(
        o_hg, o_da, proj, 0, d, x2, gate1, scale2, shift2, gate2, norm2_g[l],
        w_branch_a[l].astype(BF16), w_branch_b[l].astype(BF16), w_out[l].astype(BF16), wr_hi, wr_lo,
        w_sh_gate[l].astype(BF16), w_sh_up[l].astype(BF16), w_sh_down[l].astype(BF16), seq)

    top_e, gate_w, rank, counts = _route(logits_t, router_bias[l])

    tm_e = EXPERT_TILE
    n_blocks = (ntok * TOP_K) // tm_e + N_EXPERTS
    nblk = (counts[:, 0] + tm_e - 1) // tm_e
    blk_end = jnp.cumsum(nblk)
    pad_start = ((blk_end - nblk) * tm_e).astype(I32)
    n_used = blk_end[-1:].astype(I32)
    block_e = jnp.minimum(jnp.sum(blk_end[None, :] <= jnp.arange(n_blocks, dtype=I32)[:, None], axis=1),
                          N_EXPERTS - 1).astype(I32)

    dest = _slots(pad_start, top_e, rank)
    tok = jnp.broadcast_to(jnp.arange(ntok, dtype=I32)[None, :], dest.shape)
    slot_tok = jnp.zeros((n_blocks * tm_e,), I32).at[dest.reshape(-1)].set(tok.reshape(-1), unique_indices=True)
    ys = _routed_experts(h2p, slot_tok, block_e, n_used, w_exp_gate, w_exp_up, w_exp_down, tm_e)
    out = _combine(ys, dest, gate_w, base, gate2, seq)
    return out.reshape(bsz, seq, d)
```

```python
import functools
import math

import numpy as np
import jax
import jax.numpy as jnp
from jax import lax
from jax.experimental import pallas as pl
from jax.experimental.pallas import tpu as pltpu

F32 = jnp.float32
BF16 = jnp.bfloat16
I32 = jnp.int32
U32 = jnp.uint32

HG_HEADS = 8
HG_DK = 128
HG_DV = 128
DA_HEADS = 8
DA_DH = 64
DA_DV = 128
RP_BUCKETS = 32
RP_MAX_EXACT = 16
RP_MAX_DIST = 128
N_EXPERTS = 64
N_GROUPS = 8
TOPK_GROUPS = 4
TOP_K = 8
ROUTE_SCALE = 2.5
EPS = 1e-6
LAMBDA_INIT = 0.8 - 0.6 * math.exp(-0.3 * 0)

HEAD_W = 128
VMEM_LIMIT_BYTES = 56 * 1024 * 1024
NEG_BIG = -1e30
LOG2E = math.log2(math.e)

ATTN_TILE = 512
HGRN_TILE = 256
MERGE_TILE = 256
EXPERT_TILE = 256
COMBINE_TILE = 128
ROUTE_TILE = 512

NT_DIMS = (((1,), (1,)), ((), ()))
TN_DIMS = (((0,), (0,)), ((), ()))


def _cparams(sem, **kw):
    return pltpu.CompilerParams(dimension_semantics=sem, vmem_limit_bytes=VMEM_LIMIT_BYTES, **kw)


def _silu(x):
    return x * jax.nn.sigmoid(x)


HI_MASK = np.uint32(0xFFFF0000)


def _pack_bf16_pair(a, b):
    ua = lax.bitcast_convert_type(a.astype(BF16).astype(F32), U32)
    ub = lax.bitcast_convert_type(b.astype(BF16).astype(F32), U32)
    return (ua >> 16) | (ub & HI_MASK)


def _unpack_bf16_pair(w):
    lo = lax.bitcast_convert_type(w << 16, F32)
    hi = lax.bitcast_convert_type(w & HI_MASK, F32)
    return lo, hi


ROW_TILE = 8


def _store_token_major(ref, x):
    rows = x.shape[0]
    for j in range(ROW_TILE):
        ref[pl.ds(j, rows, stride=ROW_TILE), :] = x[:, j * HEAD_W:(j + 1) * HEAD_W]


def _load_token_major(ref, first, rows):
    return [ref[pl.ds(first * ROW_TILE + j, rows, stride=ROW_TILE), :] for j in range(ROW_TILE)]


def _t5_bucket_starts():
    n = np.arange(0, RP_MAX_DIST + 1)
    nf = np.maximum(n, 1).astype(np.float32)
    large = RP_MAX_EXACT + (np.log(nf / np.float32(RP_MAX_EXACT)) / np.float32(math.log(RP_MAX_DIST / RP_MAX_EXACT))
                            * np.float32(RP_BUCKETS - RP_MAX_EXACT)).astype(np.int32)
    large = np.minimum(large, RP_BUCKETS - 1)
    bucket = np.where(n < RP_MAX_EXACT, n, large)
    assert np.all(np.diff(bucket) >= 0) and bucket[-1] == RP_BUCKETS - 1
    return [int(np.argmax(bucket >= j)) for j in range(RP_BUCKETS)]


BUCKET_START = _t5_bucket_starts()


def _ada_kernel(c_ref, w_ref, b_ref, o_ref):
    ca = _silu(c_ref[...]).astype(BF16)
    o_ref[...] = jnp.dot(ca, w_ref[...].astype(BF16), preferred_element_type=F32) + b_ref[...]


def _ada_mod(c, ada_w, ada_b):
    bsz, d = c.shape
    n = ada_w.shape[1]
    rows = 8
    cp = jnp.zeros((rows, d), F32).at[:bsz].set(c)
    tn = 1024
    out = pl.pallas_call(
        _ada_kernel,
        grid=(n // tn,),
        in_specs=[pl.BlockSpec((rows, d), lambda j: (0, 0)),
                  pl.BlockSpec((d, tn), lambda j: (0, j)),
                  pl.BlockSpec((1, tn), lambda j: (0, j))],
        out_specs=pl.BlockSpec((rows, tn), lambda j: (0, j)),
        out_shape=jax.ShapeDtypeStruct((rows, n), F32),
        compiler_params=_cparams(("arbitrary",)),
    )(cp, ada_w, ada_b.reshape(1, n))
    return out[:bsz]


def _inproj_kernel(x_ref, sc_ref, sh_ref, g_ref, w_ref, o_ref, h_scr):
    @pl.when(pl.program_id(1) == 0)
    def _():
        x = x_ref[...]
        ms = jnp.mean(x * x, axis=-1, keepdims=True)
        hn = x * lax.rsqrt(ms + EPS) * g_ref[...]
        h_scr[...] = (hn * (1.0 + sc_ref[...]) + sh_ref[...]).astype(BF16)

    o_ref[...] = jnp.dot(h_scr[...], w_ref[...], preferred_element_type=F32).astype(o_ref.dtype)


def _in_projection(x2, scale, shift, g, w_bf, seq):
    ntok, d = x2.shape
    n = w_bf.shape[1]
    tm = min(1024, seq)
    tn = 1024
    per_b = seq // tm
    return pl.pallas_call(
        _inproj_kernel,
        grid=(ntok // tm, n // tn),
        in_specs=[pl.BlockSpec((tm, d), lambda i, j: (i, 0)),
                  pl.BlockSpec((None, 1, d), lambda i, j: (i // per_b, 0, 0)),
                  pl.BlockSpec((None, 1, d), lambda i, j: (i // per_b, 0, 0)),
                  pl.BlockSpec((1, d), lambda i, j: (0, 0)),
                  pl.BlockSpec((d, tn), lambda i, j: (0, j))],
        out_specs=pl.BlockSpec((tm, tn), lambda i, j: (i, j)),
        out_shape=jax.ShapeDtypeStruct((ntok, n), BF16),
        scratch_shapes=[pltpu.VMEM((tm, d), BF16)],
        compiler_params=_cparams(("arbitrary", "arbitrary")),
    )(x2, scale[:, None, :], shift[:, None, :], g.reshape(1, d), w_bf)


def _hgrn_kernel(q_ref, f_ref, i_ref, g_ref, lbl_ref, ng_ref, o_ref, state_ref, *, tile):
    @pl.when(pl.program_id(1) == 0)
    def _():
        state_ref[...] = jnp.zeros_like(state_ref)

    n_levels = tile.bit_length() - 1
    row = lax.broadcasted_iota(I32, (tile, tile), 0)
    col = lax.broadcasted_iota(I32, (tile, tile), 1)
    lev = jnp.where(row >= col, 31 - lax.clz(row ^ col), -2)
    rowk = lax.broadcasted_iota(I32, (tile, HG_DK), 0)

    for h in range(HG_HEADS):
        sl = slice(h * HEAD_W, (h + 1) * HEAD_W)
        q = _silu(q_ref[:, sl].astype(F32)) * (HG_DK ** -0.5)
        ll = lbl_ref[:, sl]
        el = jnp.exp(ll - jnp.max(ll, axis=0, keepdims=True))
        lb = el[0:1, :] / jnp.sum(el, axis=0, keepdims=True)
        fg = lb + (1.0 - lb) * jax.nn.sigmoid(f_ref[:, sl].astype(F32))
        k = 1.0 - fg
        g = jnp.log(fg)
        v = i_ref[:, sl]

        scores = jnp.where(lev == -1,
                           lax.dot_general(q.astype(BF16), k.astype(BF16), NT_DIMS, preferred_element_type=F32),
                           0.0)
        c = g
        e = g
        for lvl in range(n_levels):
            blk = 1 << lvl
            qd = (q * jnp.exp(c)).astype(BF16)
            kd = (k * jnp.exp(e - c)).astype(BF16)
            s_l = lax.dot_general(qd, kd, NT_DIMS, preferred_element_type=F32)
            scores = jnp.where(lev == lvl, s_l, scores)
            odd = (rowk & blk) != 0
            e_prev = pltpu.roll(e, blk, axis=0)
            e_next = pltpu.roll(e, tile - blk, axis=0)
            c = c + jnp.where(odd, e_prev, 0.0)
            e = e + jnp.where(odd, e_prev, e_next)
        st = state_ref[h]
        qb = (q * jnp.exp(c)).astype(BF16)
        o = (jnp.dot(scores.astype(BF16), v, preferred_element_type=F32)
             + lax.dot_general(qb, st.astype(BF16), NT_DIMS, preferred_element_type=F32))
        kd = (k * jnp.exp(e - c)).astype(BF16)
        state_ref[h] = st * jnp.exp(e[0:1, :]) + lax.dot_general(v, kd, TN_DIMS, preferred_element_type=F32)

        ms = jnp.mean(o * o, axis=-1, keepdims=True)
        gate = _silu(g_ref[:, sl].astype(F32))
        o_ref[:, sl] = (o * lax.rsqrt(ms + EPS) * ng_ref[...] * gate).astype(o_ref.dtype)


def _hgrn_branch(proj, lb_logits, norm_g, bsz, seq, col0):
    ntok = proj.shape[0]
    tile = min(HGRN_TILE, seq)
    nt = seq // tile
    width = HG_HEADS * HEAD_W
    cb = col0 // width

    def spec(off):
        return pl.BlockSpec((tile, width), lambda b, t, off=off: (b * nt + t, cb + off))

    return pl.pallas_call(
        functools.partial(_hgrn_kernel, tile=tile),
        grid=(bsz, nt),
        in_specs=[spec(0), spec(1), spec(2), spec(3),
                  pl.BlockSpec(lb_logits.shape, lambda b, t: (0, 0)),
                  pl.BlockSpec((1, HG_DV), lambda b, t: (0, 0))],
        out_specs=pl.BlockSpec((tile, width), lambda b, t: (b * nt + t, 0)),
        out_shape=jax.ShapeDtypeStruct((ntok, width), BF16),
        scratch_shapes=[pltpu.VMEM((HG_HEADS, HG_DV, HG_DK), F32)],
        compiler_params=_cparams(("arbitrary", "arbitrary")),
    )(proj, proj, proj, proj, lb_logits, norm_g.reshape(1, HG_DV))


def _qknorm_kernel(q_ref, k_ref, qg_ref, kg_ref, qo_ref, ko_ref):
    lane = lax.broadcasted_iota(I32, (q_ref.shape[0], HEAD_W), 1)
    lo = lane < DA_DH

    def norm(x_ref, g_ref, o_ref, scale):
        for h in range(DA_HEADS):
            sl = slice(h * HEAD_W, (h + 1) * HEAD_W)
            x = x_ref[:, sl].astype(F32)
            xx = x * x
            s0 = jnp.sum(jnp.where(lo, xx, 0.0), axis=-1, keepdims=True)
            s1 = jnp.sum(jnp.where(lo, 0.0, xx), axis=-1, keepdims=True)
            ms = jnp.where(lo, s0, s1) * (1.0 / DA_DH)
            o_ref[:, sl] = (x * lax.rsqrt(ms + EPS) * g_ref[...] * scale).astype(o_ref.dtype)

    norm(q_ref, qg_ref, qo_ref, DA_DH ** -0.5 * LOG2E)
    norm(k_ref, kg_ref, ko_ref, 1.0)


def _qk_norm(proj, q_g, k_g, col_q, col_k):
    ntok = proj.shape[0]
    width = DA_HEADS * HEAD_W
    tm = min(512, ntok)
    qg = jnp.tile(q_g, 2).reshape(1, HEAD_W)
    kg = jnp.tile(k_g, 2).reshape(1, HEAD_W)
    return pl.pallas_call(
        _qknorm_kernel,
        grid=(ntok // tm,),
        in_specs=[pl.BlockSpec((tm, width), lambda i: (i, col_q // width)),
                  pl.BlockSpec((tm, width), lambda i: (i, col_k // width)),
                  pl.BlockSpec((1, HEAD_W), lambda i: (0, 0)),
                  pl.BlockSpec((1, HEAD_W), lambda i: (0, 0))],
        out_specs=[pl.BlockSpec((tm, width), lambda i: (i, 0)),
                   pl.BlockSpec((tm, width), lambda i: (i, 0))],
        out_shape=[jax.ShapeDtypeStruct((ntok, width), BF16)] * 2,
        compiler_params=_cparams(("arbitrary",)),
    )(proj, proj, qg, kg)


def _attn_kernel(qi_ref, ki_ref, q_ref, k_ref, v_ref, rel_ref, lamp_ref, ng_ref, o_ref,
                 m_scr, acc_scr, bias_scr, *, tile):
    b = pl.program_id(0)
    p = pl.program_id(1)
    qi = qi_ref[p]
    ki = ki_ref[p]
    diff = qi - ki

    row = lax.broadcasted_iota(I32, (tile, tile), 0)
    col = lax.broadcasted_iota(I32, (tile, tile), 1)

    @pl.when((b == 0) & (p == 0))
    def _():
        def per_head(h, carry):
            for d in range(2):
                dist = row - col + d * tile
                bias = jnp.full((tile, tile), rel_ref[0, h], F32)
                for j in range(1, RP_BUCKETS):
                    bias = jnp.where(dist >= BUCKET_START[j], rel_ref[j, h], bias)
                bias_scr[d, h] = bias * LOG2E
            return carry
        lax.fori_loop(0, DA_HEADS, per_head, 0)

    @pl.when(ki == 0)
    def _():
        m_scr[...] = jnp.full_like(m_scr, NEG_BIG)
        acc_scr[...] = jnp.zeros_like(acc_scr)

    lane = lax.broadcasted_iota(I32, (tile, HEAD_W), 1)
    lo = lane < DA_DH
    ones = jnp.ones((tile, HEAD_W), BF16)
    reps = tile // HEAD_W

    def step(kind):
        for h in range(DA_HEADS):
            sl = slice(h * HEAD_W, (h + 1) * HEAD_W)
            q = q_ref[:, sl]
            k = k_ref[:, sl]
            v_aug = jnp.concatenate([v_ref[:, sl], ones], axis=1)
            for c in range(2):
                qc = jnp.where(lo, q, jnp.zeros_like(q)) if c == 0 else jnp.where(lo, jnp.zeros_like(q), q)
                s = lax.dot_general(qc, k, NT_DIMS, preferred_element_type=F32)
                if kind == 2:
                    s = s + rel_ref[RP_BUCKETS - 1, h] * LOG2E
                else:
                    s = s + bias_scr[kind, h]
                if kind == 0:
                    s = jnp.where(row >= col, s, NEG_BIG)
                idx = 2 * h + c
                m_old = m_scr[idx]
                m_cur = jnp.broadcast_to(jnp.max(s, axis=-1, keepdims=True), (tile, HEAD_W))
                m_new = jnp.maximum(m_old, m_cur)
                alpha = jnp.exp2(m_old - m_new)
                pr = jnp.exp2(s - jnp.concatenate([m_new] * reps, axis=1))
                pv = jnp.dot(pr.astype(BF16), v_aug, preferred_element_type=F32)
                acc_scr[idx] = jnp.concatenate([alpha, alpha], axis=1) * acc_scr[idx] + pv
                m_scr[idx] = m_new

    for kind, cond in ((0, diff == 0), (1, diff == 1), (2, diff > 1)):
        pl.when(cond)(functools.partial(step, kind))

    @pl.when(diff == 0)
    def _():
        lp = lamp_ref[...]
        lam = (jnp.exp(jnp.sum(lp[0:1] * lp[1:2], axis=-1, keepdims=True))
               - jnp.exp(jnp.sum(lp[2:3] * lp[3:4], axis=-1, keepdims=True)) + LAMBDA_INIT)
        for h in range(DA_HEADS):
            sl = slice(h * HEAD_W, (h + 1) * HEAD_W)
            a0 = acc_scr[2 * h]
            a1 = acc_scr[2 * h + 1]
            o = a0[:, :DA_DV] / a0[:, DA_DV:] - lam * (a1[:, :DA_DV] / a1[:, DA_DV:])
            ms = jnp.mean(o * o, axis=-1, keepdims=True)
            o_ref[:, sl] = (o * lax.rsqrt(ms + EPS) * ng_ref[...] * (1.0 - LAMBDA_INIT)).astype(o_ref.dtype)


def _diff_attention(qn, kn, proj, col_v, rel_bias, lam_params, norm_g, bsz, seq):
    ntok = qn.shape[0]
    tile = min(ATTN_TILE, seq)
    nq = seq // tile
    width = DA_HEADS * HEAD_W
    qi_list, ki_list = [], []
    for a in range(nq):
        for c in range(a + 1):
            qi_list.append(a)
            ki_list.append(c)
    qi_arr = jnp.asarray(qi_list, I32)
    ki_arr = jnp.asarray(ki_list, I32)
    cv = col_v // width
    grid_spec = pltpu.PrefetchScalarGridSpec(
        num_scalar_prefetch=2,
        grid=(bsz, len(qi_list)),
        in_specs=[pl.BlockSpec((tile, width), lambda b, p, qi, ki: (b * nq + qi[p], 0)),
                  pl.BlockSpec((tile, width), lambda b, p, qi, ki: (b * nq + ki[p], 0)),
                  pl.BlockSpec((tile, width), lambda b, p, qi, ki: (b * nq + ki[p], cv)),
                  pl.BlockSpec(memory_space=pltpu.SMEM),
                  pl.BlockSpec((4, DA_DH), lambda b, p, qi, ki: (0, 0)),
                  pl.BlockSpec((1, DA_DV), lambda b, p, qi, ki: (0, 0))],
        out_specs=pl.BlockSpec((tile, width), lambda b, p, qi, ki: (b * nq + qi[p], 0)),
        scratch_shapes=[pltpu.VMEM((2 * DA_HEADS, tile, HEAD_W), F32),
                        pltpu.VMEM((2 * DA_HEADS, tile, 2 * DA_DV), F32),
                        pltpu.VMEM((2, DA_HEADS, tile, tile), F32)],
    )
    return pl.pallas_call(
        functools.partial(_attn_kernel, tile=tile),
        grid_spec=grid_spec,
        out_shape=jax.ShapeDtypeStruct((ntok, width), BF16),
        compiler_params=_cparams(("arbitrary", "arbitrary")),
    )(qi_arr, ki_arr, qn, kn, proj, rel_bias, lam_params, norm_g.reshape(1, DA_DV))


def _merge_kernel(oa_ref, ob_ref, ga_ref, gb_ref, x_ref, g1_ref, sc_ref, sh_ref, g2_ref, n2_ref,
                  wa_ref, wb_ref, wo_ref, wrh_ref, wrl_ref, sg_ref, su_ref, sd_ref,
                  base_ref, hp_ref, lg_ref):
    ya = jnp.dot(oa_ref[...], wa_ref[...], preferred_element_type=F32)
    yb = jnp.dot(ob_ref[...], wb_ref[...], preferred_element_type=F32)
    merged = (jax.nn.sigmoid(ga_ref[...].astype(F32)) * ya
              + jax.nn.sigmoid(gb_ref[...].astype(F32)) * yb).astype(BF16)
    y = jnp.dot(merged, wo_ref[...], preferred_element_type=F32)
    x1 = x_ref[...] + g1_ref[...] * y
    ms = jnp.mean(x1 * x1, axis=-1, keepdims=True)
    h2 = (x1 * lax.rsqrt(ms + EPS) * n2_ref[...]) * (1.0 + sc_ref[...]) + sh_ref[...]
    half = h2.shape[1] // 2
    _store_token_major(hp_ref, _pack_bf16_pair(h2[:, :half], h2[:, half:]))
    hh = h2.astype(BF16)
    a = (_silu(jnp.dot(hh, sg_ref[...], preferred_element_type=F32))
         * jnp.dot(hh, su_ref[...], preferred_element_type=F32)).astype(BF16)
    base_ref[...] = x1 + g2_ref[...] * jnp.dot(a, sd_ref[...], preferred_element_type=F32)
    hl = (h2 - hh.astype(F32)).astype(BF16)
    lg_ref[...] = (lax.dot_general(wrh_ref[...], hh, NT_DIMS, preferred_element_type=F32)
                   + lax.dot_general(wrh_ref[...], hl, NT_DIMS, preferred_element_type=F32)
                   + lax.dot_general(wrl_ref[...], hh, NT_DIMS, preferred_element_type=F32))


def _merge_out(o_hg, o_da, proj, col_ga, col_gb, x2, gate1, scale2, shift2, gate2, norm2_g,
               wa, wb, wo, wr_hi, wr_lo, sg, su, sd, seq):
    ntok, d = x2.shape
    tm = min(MERGE_TILE, seq)
    per_b = seq // tm
    wa_w = o_hg.shape[1]
    wb_w = o_da.shape[1]
    de = sg.shape[1]

    def const(shape):
        return pl.BlockSpec(shape, lambda i: (0,) * len(shape), pipeline_mode=pl.Buffered(1))

    def perb():
        return pl.BlockSpec((None, 1, d), lambda i: (i // per_b, 0, 0))

    return pl.pallas_call(
        _merge_kernel,
        grid=(ntok // tm,),
        in_specs=[pl.BlockSpec((tm, wa_w), lambda i: (i, 0)),
                  pl.BlockSpec((tm, wb_w), lambda i: (i, 0)),
                  pl.BlockSpec((tm, d), lambda i: (i, col_ga // d)),
                  pl.BlockSpec((tm, d), lambda i: (i, col_gb // d)),
                  pl.BlockSpec((tm, d), lambda i: (i, 0)),
                  perb(), perb(), perb(), perb(),
                  const((1, d)),
                  const((wa_w, d)), const((wb_w, d)), const((d, d)),
                  const((N_EXPERTS, d)), const((N_EXPERTS, d)),
                  const((d, de)), const((d, de)), const((de, d))],
        out_specs=[pl.BlockSpec((tm, d), lambda i: (i, 0)),
                   pl.BlockSpec((tm * ROW_TILE, HEAD_W), lambda i: (i, 0)),
                   pl.BlockSpec((N_EXPERTS, tm), lambda i: (0, i))],
        out_shape=[jax.ShapeDtypeStruct((ntok, d), F32),
                   jax.ShapeDtypeStruct((ntok * ROW_TILE, HEAD_W), U32),
                   jax.ShapeDtypeStruct((N_EXPERTS, ntok), F32)],
        compiler_params=_cparams(("arbitrary",)),
    )(o_hg, o_da, proj, proj, x2, gate1[:, None, :], scale2[:, None, :], shift2[:, None, :],
      gate2[:, None, :], norm2_g.reshape(1, d), wa, wb, wo, wr_hi, wr_lo, sg, su, sd)


def _route_kernel(lg_ref, rb_ref, te_ref, gw_ref, rk_ref, cnt_ref, carry_scr, *, tt):
    @pl.when(pl.program_id(0) == 0)
    def _():
        carry_scr[...] = jnp.zeros_like(carry_scr)

    per_g = N_EXPERTS // N_GROUPS
    scores = jax.nn.sigmoid(lg_ref[...])
    sel = scores + rb_ref[...]
    sel3 = sel.reshape(N_GROUPS, per_g, tt)
    j_io = lax.broadcasted_iota(I32, (N_GROUPS, per_g, tt), 1)
    m1 = jnp.max(sel3, axis=1, keepdims=True)
    i1 = jnp.min(jnp.where(sel3 == m1, j_io, per_g), axis=1, keepdims=True)
    m2 = jnp.max(jnp.where(j_io == i1, -jnp.inf, sel3), axis=1, keepdims=True)
    gs = (m1 + m2).reshape(N_GROUPS, tt)
    g_io = lax.broadcasted_iota(I32, (N_GROUPS, tt), 0)
    gmask = jnp.zeros((N_GROUPS, tt), jnp.bool_)
    for _ in range(TOPK_GROUPS):
        gm = jnp.max(gs, axis=0, keepdims=True)
        gi = jnp.min(jnp.where(gs == gm, g_io, N_GROUPS), axis=0, keepdims=True)
        hit = g_io == gi
        gmask = gmask | hit
        gs = jnp.where(hit, -jnp.inf, gs)
    emask = jnp.broadcast_to(gmask.reshape(N_GROUPS, 1, tt), (N_GROUPS, per_g, tt)).reshape(N_EXPERTS, tt)
    cand = jnp.where(emask, sel, -jnp.inf)
    e_io = lax.broadcasted_iota(I32, (N_EXPERTS, tt), 0)
    chosen = jnp.zeros((N_EXPERTS, tt), jnp.bool_)
    picks = []
    for _ in range(TOP_K):
        em = jnp.max(cand, axis=0, keepdims=True)
        ei = jnp.min(jnp.where(cand == em, e_io, N_EXPERTS), axis=0, keepdims=True)
        hit = e_io == ei
        chosen = chosen | hit
        cand = jnp.where(hit, -jnp.inf, cand)
        picks.append((ei, hit))
    gsel = jnp.where(chosen, scores, 0.0)
    wnorm = gsel / jnp.sum(gsel, axis=0, keepdims=True) * ROUTE_SCALE
    ch = jnp.where(chosen, 1.0, 0.0)
    tri = (lax.broadcasted_iota(I32, (tt, tt), 0) <= lax.broadcasted_iota(I32, (tt, tt), 1))
    incl = jnp.dot(ch.astype(BF16), jnp.where(tri, 1.0, 0.0).astype(BF16), preferred_element_type=F32)
    carry = carry_scr[...]
    excl = incl - ch + carry
    carry_new = carry + incl[:, tt - 1:tt]
    carry_scr[...] = carry_new
    for r, (ei, hit) in enumerate(picks):
        te_ref[r:r + 1, :] = ei
        gw_ref[r:r + 1, :] = jnp.sum(jnp.where(hit, wnorm, 0.0), axis=0, keepdims=True)
        rk_ref[r:r + 1, :] = jnp.sum(jnp.where(hit, excl, 0.0), axis=0, keepdims=True).astype(I32)
    cnt_ref[...] = jnp.broadcast_to(carry_new, cnt_ref.shape).astype(I32)


def _route(logits_t, router_bias):
    ntok = logits_t.shape[1]
    tt = min(ROUTE_TILE, ntok)
    return pl.pallas_call(
        functools.partial(_route_kernel, tt=tt),
        grid=(ntok // tt,),
        in_specs=[pl.BlockSpec((N_EXPERTS, tt), lambda i: (0, i)),
                  pl.BlockSpec((N_EXPERTS, 1), lambda i: (0, 0))],
        out_specs=[pl.BlockSpec((TOP_K, tt), lambda i: (0, i)),
                   pl.BlockSpec((TOP_K, tt), lambda i: (0, i)),
                   pl.BlockSpec((TOP_K, tt), lambda i: (0, i)),
                   pl.BlockSpec((N_EXPERTS, 128), lambda i: (0, 0))],
        out_shape=[jax.ShapeDtypeStruct((TOP_K, ntok), I32),
                   jax.ShapeDtypeStruct((TOP_K, ntok), F32),
                   jax.ShapeDtypeStruct((TOP_K, ntok), I32),
                   jax.ShapeDtypeStruct((N_EXPERTS, 128), I32)],
        scratch_shapes=[pltpu.VMEM((N_EXPERTS, 1), F32)],
        compiler_params=_cparams(("arbitrary",)),
    )(logits_t, router_bias.reshape(N_EXPERTS, 1))


def _slots_kernel(ps_ref, te_ref, rk_ref, d_ref):
    te = te_ref[...]
    dest = rk_ref[...]
    for e in range(N_EXPERTS):
        dest = dest + jnp.where(te == e, ps_ref[e], 0)
    d_ref[...] = dest


def _slots(pad_start, top_e, rank):
    ntok = top_e.shape[1]
    tt = min(2048, ntok)
    return pl.pallas_call(
        _slots_kernel,
        grid=(ntok // tt,),
        in_specs=[pl.BlockSpec(memory_space=pltpu.SMEM),
                  pl.BlockSpec((TOP_K, tt), lambda i: (0, i)),
                  pl.BlockSpec((TOP_K, tt), lambda i: (0, i))],
        out_specs=pl.BlockSpec((TOP_K, tt), lambda i: (0, i)),
        out_shape=jax.ShapeDtypeStruct((TOP_K, ntok), I32),
        compiler_params=_cparams(("arbitrary",)),
    )(pad_start, top_e, rank)


GATHER_UNROLL = 8


def _expert_kernel(be_ref, nu_ref, idx_ref, idxn_ref, h_hbm, wg_ref, wu_ref, wd_ref, o_ref,
                   xs, wgb, wub, wdb, sem, *, tm):
    i = pl.program_id(0)
    n_used = nu_ref[0]
    slot = i % 2

    def row_copy(tok, s, r):
        return pltpu.make_async_copy(h_hbm.at[tok], xs.at[s, pl.ds(r * ROW_TILE, ROW_TILE)], sem.at[s])

    def issue(idx_r, s):
        def body(g, carry):
            for u in range(GATHER_UNROLL):
                r = g * GATHER_UNROLL + u
                row_copy(idx_r[0, r], s, r).start(priority=u % 2)
            return carry
        lax.fori_loop(0, tm // GATHER_UNROLL, body, 0)

    @pl.when(i == 0)
    def _():
        issue(idx_ref, 0)

    @pl.when(i + 1 < n_used)
    def _():
        issue(idxn_ref, 1 - slot)

    @pl.when(i < n_used)
    def _():
        e = be_ref[i]
        e_prev = be_ref[jnp.maximum(i - 1, 0)]

        @pl.when((i == 0) | (e != e_prev))
        def _():
            wgb[...] = wg_ref[...].astype(BF16)
            wub[...] = wu_ref[...].astype(BF16)
            wdb[...] = wd_ref[...].astype(BF16)

        def wait_body(g, carry):
            for u in range(GATHER_UNROLL):
                row_copy(0, slot, g * GATHER_UNROLL + u).wait()
            return carry
        lax.fori_loop(0, tm // GATHER_UNROLL, wait_body, 0)

        xlo, xhi = _unpack_bf16_pair(jnp.concatenate(_load_token_major(xs.at[slot], 0, tm), axis=1))
        xlo = xlo.astype(BF16)
        xhi = xhi.astype(BF16)
        half = xlo.shape[1]

        def proj_in(w):
            return (jnp.dot(xlo, w[:half, :], preferred_element_type=F32)
                    + jnp.dot(xhi, w[half:, :], preferred_element_type=F32))

        a = (_silu(proj_in(wgb)) * proj_in(wub)).astype(BF16)
        y = jnp.dot(a, wdb[...], preferred_element_type=F32)
        _store_token_major(o_ref, _pack_bf16_pair(y[:, :half], y[:, half:]))

    @pl.when(i >= n_used)
    def _():
        o_ref[...] = jnp.zeros_like(o_ref)


def _routed_experts(h2p, slot_tok, block_e, n_used, w_gate, w_up, w_down, tm):
    n_blocks = block_e.shape[0]
    d, de = w_gate.shape[-2:]
    h3 = h2p.reshape(-1, ROW_TILE, HEAD_W)
    idx3 = slot_tok.reshape(n_blocks, 1, tm)

    def wspec(shape):
        return pl.BlockSpec((None, None) + shape, lambda i, be, nu: (0, be[i], 0, 0))

    grid_spec = pltpu.PrefetchScalarGridSpec(
        num_scalar_prefetch=2,
        grid=(n_blocks,),
        in_specs=[pl.BlockSpec((None, 1, tm), lambda i, be, nu: (i, 0, 0), memory_space=pltpu.SMEM),
                  pl.BlockSpec((None, 1, tm), lambda i, be, nu: (jnp.minimum(i + 1, n_blocks - 1), 0, 0),
                               memory_space=pltpu.SMEM),
                  pl.BlockSpec(memory_space=pl.ANY),
                  wspec((d, de)), wspec((d, de)), wspec((de, d))],
        out_specs=pl.BlockSpec((tm * ROW_TILE, HEAD_W), lambda i, be, nu: (i, 0)),
        scratch_shapes=[pltpu.VMEM((2, tm * ROW_TILE, HEAD_W), U32),
                        pltpu.VMEM((d, de), BF16), pltpu.VMEM((d, de), BF16), pltpu.VMEM((de, d), BF16),
                        pltpu.SemaphoreType.DMA((2,))],
    )
    return pl.pallas_call(
        functools.partial(_expert_kernel, tm=tm),
        grid_spec=grid_spec,
        out_shape=jax.ShapeDtypeStruct((n_blocks * tm * ROW_TILE, HEAD_W), U32),
        compiler_params=_cparams(("arbitrary",), disable_bounds_checks=True),
    )(block_e, n_used, idx3, idx3, h3, w_gate, w_up, w_down)


def _combine_kernel(d_ref, dn_ref, ys_hbm, gw_ref, base_ref, g2_ref, o_ref, buf, sem, *, tm):
    i = pl.program_id(0)
    n = pl.num_programs(0)
    slot = i % 2

    def row_copy(src, s, r, t):
        return pltpu.make_async_copy(ys_hbm.at[src], buf.at[s, pl.ds((r * tm + t) * ROW_TILE, ROW_TILE)],
                                     sem.at[s])

    def issue(d_r, s):
        def body(t, carry):
            for r in range(TOP_K):
                row_copy(d_r[r, t], s, r, t).start(priority=r % 2)
            return carry
        lax.fori_loop(0, tm, body, 0)

    @pl.when(i == 0)
    def _():
        issue(d_ref, 0)

    @pl.when(i + 1 < n)
    def _():
        issue(dn_ref, 1 - slot)

    def wait_body(t, carry):
        for r in range(TOP_K):
            row_copy(0, slot, r, t).wait()
        return carry
    lax.fori_loop(0, tm, wait_body, 0)

    half = ROW_TILE * HEAD_W
    acc_lo = [jnp.zeros((tm, HEAD_W), F32)] * ROW_TILE
    acc_hi = [jnp.zeros((tm, HEAD_W), F32)] * ROW_TILE
    for r in range(TOP_K):
        w = jnp.broadcast_to(gw_ref[:, r:r + 1], (tm, HEAD_W))
        for j, piece in enumerate(_load_token_major(buf.at[slot], r * tm, tm)):
            lo, hi = _unpack_bf16_pair(piece)
            acc_lo[j] = acc_lo[j] + w * lo
            acc_hi[j] = acc_hi[j] + w * hi
    for j in range(ROW_TILE):
        lo_sl = slice(j * HEAD_W, (j + 1) * HEAD_W)
        hi_sl = slice(half + j * HEAD_W, half + (j + 1) * HEAD_W)
        o_ref[:, lo_sl] = base_ref[:, lo_sl] + g2_ref[:, lo_sl] * acc_lo[j]
        o_ref[:, hi_sl] = base_ref[:, hi_sl] + g2_ref[:, hi_sl] * acc_hi[j]


def _combine(ys, dest, gate_w, base, gate2, seq):
    ntok, d = base.shape
    y3 = ys.reshape(-1, ROW_TILE, HEAD_W)
    tm = min(COMBINE_TILE, seq)
    nt = ntok // tm
    per_b = seq // tm
    d3 = dest.reshape(TOP_K, nt, tm).transpose(1, 0, 2)
    return pl.pallas_call(
        functools.partial(_combine_kernel, tm=tm),
        grid=(nt,),
        in_specs=[pl.BlockSpec((None, TOP_K, tm), lambda i: (i, 0, 0), memory_space=pltpu.SMEM),
                  pl.BlockSpec((None, TOP_K, tm), lambda i: (jnp.minimum(i + 1, nt - 1), 0, 0),
                               memory_space=pltpu.SMEM),
                  pl.BlockSpec(memory_space=pl.ANY),
                  pl.BlockSpec((tm, TOP_K), lambda i: (i, 0)),
                  pl.BlockSpec((tm, d), lambda i: (i, 0)),
                  pl.BlockSpec((None, 1, d), lambda i: (i // per_b, 0, 0))],
        out_specs=pl.BlockSpec((tm, d), lambda i: (i, 0)),
        out_shape=jax.ShapeDtypeStruct((ntok, d), F32),
        scratch_shapes=[pltpu.VMEM((2, TOP_K * tm * ROW_TILE, HEAD_W), U32), pltpu.SemaphoreType.DMA((2,))],
        compiler_params=_cparams(("arbitrary",), disable_bounds_checks=True),
    )(d3, d3, y3, gate_w.T, base, gate2[:, None, :])


def kernel(x, c, ada_w, ada_b, norm1_g, w_in, lb_logits, hg_norm_g, q_norm_g, k_norm_g, lambda_q1, lambda_k1,
           lambda_q2, lambda_k2, da_norm_g, rel_bias, w_branch_a, w_branch_b, w_out, norm2_g, router_w,
           router_bias, w_exp_gate, w_exp_up, w_exp_down, w_sh_gate, w_sh_up, w_sh_down):
    bsz, seq, d = x.shape
    ntok = bsz * seq
    l = 0
    x2 = x.reshape(ntok, d)

    mod = _ada_mod(c, ada_w[l], ada_b[l])
    shift1, scale1, gate1, shift2, scale2, gate2 = jnp.split(mod, 6, axis=-1)

    n_gate = 2 * d
    split = w_in.shape[2] - n_gate
    w_in_bf = jnp.concatenate([w_in[l][:, split:], w_in[l][:, :split]], axis=1).astype(BF16)
    proj = _in_projection(x2, scale1, shift1, norm1_g[l], w_in_bf, seq)
    col_hg = n_gate
    col_q = col_hg + 4 * HG_HEADS * HEAD_W
    col_k = col_q + DA_HEADS * HEAD_W
    col_v = col_k + DA_HEADS * HEAD_W

    o_hg = _hgrn_branch(proj, lb_logits, hg_norm_g[l], bsz, seq, col_hg)
    qn, kn = _qk_norm(proj, q_norm_g[l], k_norm_g[l], col_q, col_k)
    lam_params = jnp.stack([lambda_q1[l], lambda_k1[l], lambda_q2[l], lambda_k2[l]])
    o_da = _diff_attention(qn, kn, proj, col_v, rel_bias, lam_params, da_norm_g[l], bsz, seq)

    wr_t = router_w[l].T
    wr_hi = wr_t.astype(BF16)
    wr_lo = (wr_t - wr_hi.astype(F32)).astype(BF16)
    base, h2p, logits_t = _merge_out(
        o_hg, o_da, proj, 0, d, x2, gate1, scale2, shift2, gate2, norm2_g[l],
        w_branch_a[l].astype(BF16), w_branch_b[l].astype(BF16), w_out[l].astype(BF16), wr_hi, wr_lo,
        w_sh_gate[l].astype(BF16), w_sh_up[l].astype(BF16), w_sh_down[l].astype(BF16), seq)

    top_e, gate_w, rank, counts = _route(logits_t, router_bias[l])

    tm_e = EXPERT_TILE
    n_blocks = (ntok * TOP_K) // tm_e + N_EXPERTS
    nblk = (counts[:, 0] + tm_e - 1) // tm_e
    blk_end = jnp.cumsum(nblk)
    pad_start = ((blk_end - nblk) * tm_e).astype(I32)
    n_used = blk_end[-1:].astype(I32)
    block_e = jnp.minimum(jnp.sum(blk_end[None, :] <= jnp.arange(n_blocks, dtype=I32)[:, None], axis=1),
                          N_EXPERTS - 1).astype(I32)

    dest = _slots(pad_start, top_e, rank)
    tok = jnp.broadcast_to(jnp.arange(ntok, dtype=I32)[None, :], dest.shape)
    slot_tok = jnp.zeros((n_blocks * tm_e,), I32).at[dest.reshape(-1)].set(tok.reshape(-1), unique_indices=True)
    ys = _routed_experts(h2p, slot_tok, block_e, n_used, w_exp_gate, w_exp_up, w_exp_down, tm_e)
    out = _combine(ys, dest, gate_w, base, gate2, seq)
    return out.reshape(bsz, seq, d)
```

```python
import functools
import math

import numpy as np
import jax
import jax.numpy as jnp
from jax import lax
from jax.experimental import pallas as pl
from jax.experimental.pallas import tpu as pltpu

F32 = jnp.float32
BF16 = jnp.bfloat16
I32 = jnp.int32
U32 = jnp.uint32

HG_HEADS = 8
HG_DK = 128
HG_DV = 128
DA_HEADS = 8
DA_DH = 64
DA_DV = 128
RP_BUCKETS = 32
RP_MAX_EXACT = 16
RP_MAX_DIST = 128
N_EXPERTS = 64
N_GROUPS = 8
TOPK_GROUPS = 4
TOP_K = 8
ROUTE_SCALE = 2.5
EPS = 1e-6
LAMBDA_INIT = 0.8 - 0.6 * math.exp(-0.3 * 0)

HEAD_W = 128
VMEM_LIMIT_BYTES = 56 * 1024 * 1024
NEG_BIG = -1e30
LOG2E = math.log2(math.e)

ATTN_TILE = 512
HGRN_TILE = 256
MERGE_TILE = 256
EXPERT_TILE = 512
COMBINE_TILE = 128
ROUTE_TILE = 512

NT_DIMS = (((1,), (1,)), ((), ()))
TN_DIMS = (((0,), (0,)), ((), ()))


def _cparams(sem, **kw):
    return pltpu.CompilerParams(dimension_semantics=sem, vmem_limit_bytes=VMEM_LIMIT_BYTES, **kw)


def _silu(x):
    return x * jax.nn.sigmoid(x)


HI_MASK = np.uint32(0xFFFF0000)


def _pack_bf16_pair(a, b):
    ua = lax.bitcast_convert_type(a.astype(BF16).astype(F32), U32)
    ub = lax.bitcast_convert_type(b.astype(BF16).astype(F32), U32)
    return (ua >> 16) | (ub & HI_MASK)


def _unpack_bf16_pair(w):
    lo = lax.bitcast_convert_type(w << 16, F32)
    hi = lax.bitcast_convert_type(w & HI_MASK, F32)
    return lo, hi


ROW_TILE = 8


def _store_token_major(ref, x):
    rows = x.shape[0]
    for j in range(ROW_TILE):
        ref[pl.ds(j, rows, stride=ROW_TILE), :] = x[:, j * HEAD_W:(j + 1) * HEAD_W]


def _load_token_major(ref, first, rows):
    return [ref[pl.ds(first * ROW_TILE + j, rows, stride=ROW_TILE), :] for j in range(ROW_TILE)]


def _t5_bucket_starts():
    n = np.arange(0, RP_MAX_DIST + 1)
    nf = np.maximum(n, 1).astype(np.float32)
    large = RP_MAX_EXACT + (np.log(nf / np.float32(RP_MAX_EXACT)) / np.float32(math.log(RP_MAX_DIST / RP_MAX_EXACT))
                            * np.float32(RP_BUCKETS - RP_MAX_EXACT)).astype(np.int32)
    large = np.minimum(large, RP_BUCKETS - 1)
    bucket = np.where(n < RP_MAX_EXACT, n, large)
    assert np.all(np.diff(bucket) >= 0) and bucket[-1] == RP_BUCKETS - 1
    return [int(np.argmax(bucket >= j)) for j in range(RP_BUCKETS)]


BUCKET_START = _t5_bucket_starts()


def _ada_kernel(c_ref, w_ref, b_ref, o_ref):
    ca = _silu(c_ref[...]).astype(BF16)
    o_ref[...] = jnp.dot(ca, w_ref[...].astype(BF16), preferred_element_type=F32) + b_ref[...]


def _ada_mod(c, ada_w, ada_b):
    bsz, d = c.shape
    n = ada_w.shape[1]
    rows = 8
    cp = jnp.zeros((rows, d), F32).at[:bsz].set(c)
    tn = 1024
    out = pl.pallas_call(
        _ada_kernel,
        grid=(n // tn,),
        in_specs=[pl.BlockSpec((rows, d), lambda j: (0, 0)),
                  pl.BlockSpec((d, tn), lambda j: (0, j)),
                  pl.BlockSpec((1, tn), lambda j: (0, j))],
        out_specs=pl.BlockSpec((rows, tn), lambda j: (0, j)),
        out_shape=jax.ShapeDtypeStruct((rows, n), F32),
        compiler_params=_cparams(("arbitrary",)),
    )(cp, ada_w, ada_b.reshape(1, n))
    return out[:bsz]


def _inproj_kernel(x_ref, sc_ref, sh_ref, g_ref, w_ref, o_ref, h_scr):
    @pl.when(pl.program_id(1) == 0)
    def _():
        x = x_ref[...]
        ms = jnp.mean(x * x, axis=-1, keepdims=True)
        hn = x * lax.rsqrt(ms + EPS) * g_ref[...]
        h_scr[...] = (hn * (1.0 + sc_ref[...]) + sh_ref[...]).astype(BF16)

    o_ref[...] = jnp.dot(h_scr[...], w_ref[...], preferred_element_type=F32).astype(o_ref.dtype)


def _in_projection(x2, scale, shift, g, w_bf, seq):
    ntok, d = x2.shape
    n = w_bf.shape[1]
    tm = min(1024, seq)
    tn = 1024
    per_b = seq // tm
    return pl.pallas_call(
        _inproj_kernel,
        grid=(ntok // tm, n // tn),
        in_specs=[pl.BlockSpec((tm, d), lambda i, j: (i, 0)),
                  pl.BlockSpec((None, 1, d), lambda i, j: (i // per_b, 0, 0)),
                  pl.BlockSpec((None, 1, d), lambda i, j: (i // per_b, 0, 0)),
                  pl.BlockSpec((1, d), lambda i, j: (0, 0)),
                  pl.BlockSpec((d, tn), lambda i, j: (0, j))],
        out_specs=pl.BlockSpec((tm, tn), lambda i, j: (i, j)),
        out_shape=jax.ShapeDtypeStruct((ntok, n), BF16),
        scratch_shapes=[pltpu.VMEM((tm, d), BF16)],
        compiler_params=_cparams(("arbitrary", "arbitrary")),
    )(x2, scale[:, None, :], shift[:, None, :], g.reshape(1, d), w_bf)


def _hgrn_kernel(q_ref, f_ref, i_ref, g_ref, lbl_ref, ng_ref, o_ref, state_ref, *, tile):
    @pl.when(pl.program_id(1) == 0)
    def _():
        state_ref[...] = jnp.zeros_like(state_ref)

    n_levels = tile.bit_length() - 1
    row = lax.broadcasted_iota(I32, (tile, tile), 0)
    col = lax.broadcasted_iota(I32, (tile, tile), 1)
    lev = jnp.where(row >= col, 31 - lax.clz(row ^ col), -2)
    on_diag = lev == -1
    at_level = [lev == lvl for lvl in range(n_levels)]
    rowk = lax.broadcasted_iota(I32, (tile, HG_DK), 0)
    odd_at = [(rowk & (1 << lvl)) != 0 for lvl in range(n_levels)]

    for h in range(HG_HEADS):
        sl = slice(h * HEAD_W, (h + 1) * HEAD_W)
        q = _silu(q_ref[:, sl].astype(F32)) * (HG_DK ** -0.5)
        ll = lbl_ref[:, sl]
        el = jnp.exp(ll - jnp.max(ll, axis=0, keepdims=True))
        lb = el[0:1, :] / jnp.sum(el, axis=0, keepdims=True)
        fg = lb + (1.0 - lb) * jax.nn.sigmoid(f_ref[:, sl].astype(F32))
        k = 1.0 - fg
        g = jnp.log(fg)
        v = i_ref[:, sl]

        scores = jnp.where(on_diag,
                           lax.dot_general(q.astype(BF16), k.astype(BF16), NT_DIMS, preferred_element_type=F32),
                           0.0)
        c = g
        e = g
        for lvl in range(n_levels):
            blk = 1 << lvl
            qd = (q * jnp.exp(c)).astype(BF16)
            kd = (k * jnp.exp(e - c)).astype(BF16)
            s_l = lax.dot_general(qd, kd, NT_DIMS, preferred_element_type=F32)
            scores = jnp.where(at_level[lvl], s_l, scores)
            odd = odd_at[lvl]
            e_prev = pltpu.roll(e, blk, axis=0)
            e_next = pltpu.roll(e, tile - blk, axis=0)
            c = c + jnp.where(odd, e_prev, 0.0)
            e = e + jnp.where(odd, e_prev, e_next)
        st = state_ref[h]
        qb = (q * jnp.exp(c)).astype(BF16)
        o = (jnp.dot(scores.astype(BF16), v, preferred_element_type=F32)
             + lax.dot_general(qb, st.astype(BF16), NT_DIMS, preferred_element_type=F32))
        kd = (k * jnp.exp(e - c)).astype(BF16)
        state_ref[h] = st * jnp.exp(e[0:1, :]) + lax.dot_general(v, kd, TN_DIMS, preferred_element_type=F32)

        ms = jnp.mean(o * o, axis=-1, keepdims=True)
        gate = _silu(g_ref[:, sl].astype(F32))
        o_ref[:, sl] = (o * lax.rsqrt(ms + EPS) * ng_ref[...] * gate).astype(o_ref.dtype)


def _hgrn_branch(proj, lb_logits, norm_g, bsz, seq, col0):
    ntok = proj.shape[0]
    tile = min(HGRN_TILE, seq)
    nt = seq // tile
    width = HG_HEADS * HEAD_W
    cb = col0 // width

    def spec(off):
        return pl.BlockSpec((tile, width), lambda b, t, off=off: (b * nt + t, cb + off))

    return pl.pallas_call(
        functools.partial(_hgrn_kernel, tile=tile),
        grid=(bsz, nt),
        in_specs=[spec(0), spec(1), spec(2), spec(3),
                  pl.BlockSpec(lb_logits.shape, lambda b, t: (0, 0)),
                  pl.BlockSpec((1, HG_DV), lambda b, t: (0, 0))],
        out_specs=pl.BlockSpec((tile, width), lambda b, t: (b * nt + t, 0)),
        out_shape=jax.ShapeDtypeStruct((ntok, width), BF16),
        scratch_shapes=[pltpu.VMEM((HG_HEADS, HG_DV, HG_DK), F32)],
        compiler_params=_cparams(("arbitrary", "arbitrary")),
    )(proj, proj, proj, proj, lb_logits, norm_g.reshape(1, HG_DV))


def _qknorm_kernel(q_ref, k_ref, qg_ref, kg_ref, qo_ref, ko_ref):
    lane = lax.broadcasted_iota(I32, (q_ref.shape[0], HEAD_W), 1)
    lo = lane < DA_DH

    def norm(x_ref, g_ref, o_ref, scale):
        for h in range(DA_HEADS):
            sl = slice(h * HEAD_W, (h + 1) * HEAD_W)
            x = x_ref[:, sl].astype(F32)
            xx = x * x
            s0 = jnp.sum(jnp.where(lo, xx, 0.0), axis=-1, keepdims=True)
            s1 = jnp.sum(jnp.where(lo, 0.0, xx), axis=-1, keepdims=True)
            ms = jnp.where(lo, s0, s1) * (1.0 / DA_DH)
            o_ref[:, sl] = (x * lax.rsqrt(ms + EPS) * g_ref[...] * scale).astype(o_ref.dtype)

    norm(q_ref, qg_ref, qo_ref, DA_DH ** -0.5 * LOG2E)
    norm(k_ref, kg_ref, ko_ref, 1.0)


def _qk_norm(proj, q_g, k_g, col_q, col_k):
    ntok = proj.shape[0]
    width = DA_HEADS * HEAD_W
    tm = min(512, ntok)
    qg = jnp.tile(q_g, 2).reshape(1, HEAD_W)
    kg = jnp.tile(k_g, 2).reshape(1, HEAD_W)
    return pl.pallas_call(
        _qknorm_kernel,
        grid=(ntok // tm,),
        in_specs=[pl.BlockSpec((tm, width), lambda i: (i, col_q // width)),
                  pl.BlockSpec((tm, width), lambda i: (i, col_k // width)),
                  pl.BlockSpec((1, HEAD_W), lambda i: (0, 0)),
                  pl.BlockSpec((1, HEAD_W), lambda i: (0, 0))],
        out_specs=[pl.BlockSpec((tm, width), lambda i: (i, 0)),
                   pl.BlockSpec((tm, width), lambda i: (i, 0))],
        out_shape=[jax.ShapeDtypeStruct((ntok, width), BF16)] * 2,
        compiler_params=_cparams(("arbitrary",)),
    )(proj, proj, qg, kg)


def _attn_kernel(qi_ref, ki_ref, q_ref, k_ref, v_ref, rel_ref, lamp_ref, ng_ref, o_ref,
                 m_scr, acc_scr, bias_scr, *, tile):
    b = pl.program_id(0)
    p = pl.program_id(1)
    qi = qi_ref[p]
    ki = ki_ref[p]
    diff = qi - ki

    row = lax.broadcasted_iota(I32, (tile, tile), 0)
    col = lax.broadcasted_iota(I32, (tile, tile), 1)

    @pl.when((b == 0) & (p == 0))
    def _():
        def per_head(h, carry):
            for d in range(2):
                dist = row - col + d * tile
                bias = jnp.full((tile, tile), rel_ref[0, h], F32)
                for j in range(1, RP_BUCKETS):
                    bias = jnp.where(dist >= BUCKET_START[j], rel_ref[j, h], bias)
                bias_scr[d, h] = bias * LOG2E
            return carry
        lax.fori_loop(0, DA_HEADS, per_head, 0)

    @pl.when(ki == 0)
    def _():
        m_scr[...] = jnp.full_like(m_scr, NEG_BIG)
        acc_scr[...] = jnp.zeros_like(acc_scr)

    lane = lax.broadcasted_iota(I32, (tile, HEAD_W), 1)
    lo = lane < DA_DH
    ones = jnp.ones((tile, HEAD_W), BF16)
    reps = tile // HEAD_W

    def step(kind):
        for h in range(DA_HEADS):
            sl = slice(h * HEAD_W, (h + 1) * HEAD_W)
            q = q_ref[:, sl]
            k = k_ref[:, sl]
            v_aug = jnp.concatenate([v_ref[:, sl], ones], axis=1)
            for c in range(2):
                qc = jnp.where(lo, q, jnp.zeros_like(q)) if c == 0 else jnp.where(lo, jnp.zeros_like(q), q)
                s = lax.dot_general(qc, k, NT_DIMS, preferred_element_type=F32)
                if kind == 2:
                    s = s + rel_ref[RP_BUCKETS - 1, h] * LOG2E
                else:
                    s = s + bias_scr[kind, h]
                if kind == 0:
                    s = jnp.where(row >= col, s, NEG_BIG)
                idx = 2 * h + c
                m_old = m_scr[idx]
                m_cur = jnp.broadcast_to(jnp.max(s, axis=-1, keepdims=True), (tile, HEAD_W))
                m_new = jnp.maximum(m_old, m_cur)
                alpha = jnp.exp2(m_old - m_new)
                pr = jnp.exp2(s - jnp.concatenate([m_new] * reps, axis=1))
                pv = jnp.dot(pr.astype(BF16), v_aug, preferred_element_type=F32)
                acc_scr[idx] = jnp.concatenate([alpha, alpha], axis=1) * acc_scr[idx] + pv
                m_scr[idx] = m_new

    for kind, cond in ((0, diff == 0), (1, diff == 1), (2, diff > 1)):
        pl.when(cond)(functools.partial(step, kind))

    @pl.when(diff == 0)
    def _():
        lp = lamp_ref[...]
        lam = (jnp.exp(jnp.sum(lp[0:1] * lp[1:2], axis=-1, keepdims=True))
               - jnp.exp(jnp.sum(lp[2:3] * lp[3:4], axis=-1, keepdims=True)) + LAMBDA_INIT)
        for h in range(DA_HEADS):
            sl = slice(h * HEAD_W, (h + 1) * HEAD_W)
            a0 = acc_scr[2 * h]
            a1 = acc_scr[2 * h + 1]
            o = a0[:, :DA_DV] / a0[:, DA_DV:] - lam * (a1[:, :DA_DV] / a1[:, DA_DV:])
            ms = jnp.mean(o * o, axis=-1, keepdims=True)
            o_ref[:, sl] = (o * lax.rsqrt(ms + EPS) * ng_ref[...] * (1.0 - LAMBDA_INIT)).astype(o_ref.dtype)


def _diff_attention(qn, kn, proj, col_v, rel_bias, lam_params, norm_g, bsz, seq):
    ntok = qn.shape[0]
    tile = min(ATTN_TILE, seq)
    nq = seq // tile
    width = DA_HEADS * HEAD_W
    qi_list, ki_list = [], []
    for a in range(nq):
        for c in range(a + 1):
            qi_list.append(a)
            ki_list.append(c)
    qi_arr = jnp.asarray(qi_list, I32)
    ki_arr = jnp.asarray(ki_list, I32)
    cv = col_v // width
    grid_spec = pltpu.PrefetchScalarGridSpec(
        num_scalar_prefetch=2,
        grid=(bsz, len(qi_list)),
        in_specs=[pl.BlockSpec((tile, width), lambda b, p, qi, ki: (b * nq + qi[p], 0)),
                  pl.BlockSpec((tile, width), lambda b, p, qi, ki: (b * nq + ki[p], 0)),
                  pl.BlockSpec((tile, width), lambda b, p, qi, ki: (b * nq + ki[p], cv)),
                  pl.BlockSpec(memory_space=pltpu.SMEM),
                  pl.BlockSpec((4, DA_DH), lambda b, p, qi, ki: (0, 0)),
                  pl.BlockSpec((1, DA_DV), lambda b, p, qi, ki: (0, 0))],
        out_specs=pl.BlockSpec((tile, width), lambda b, p, qi, ki: (b * nq + qi[p], 0)),
        scratch_shapes=[pltpu.VMEM((2 * DA_HEADS, tile, HEAD_W), F32),
                        pltpu.VMEM((2 * DA_HEADS, tile, 2 * DA_DV), F32),
                        pltpu.VMEM((2, DA_HEADS, tile, tile), F32)],
    )
    return pl.pallas_call(
        functools.partial(_attn_kernel, tile=tile),
        grid_spec=grid_spec,
        out_shape=jax.ShapeDtypeStruct((ntok, width), BF16),
        compiler_params=_cparams(("arbitrary", "arbitrary")),
    )(qi_arr, ki_arr, qn, kn, proj, rel_bias, lam_params, norm_g.reshape(1, DA_DV))


def _merge_kernel(oa_ref, ob_ref, ga_ref, gb_ref, x_ref, g1_ref, sc_ref, sh_ref, g2_ref, n2_ref,
                  wa_ref, wb_ref, wo_ref, wrh_ref, wrl_ref, sg_ref, su_ref, sd_ref,
                  base_ref, hp_ref, lg_ref):
    ya = jnp.dot(oa_ref[...], wa_ref[...], preferred_element_type=F32)
    yb = jnp.dot(ob_ref[...], wb_ref[...], preferred_element_type=F32)
    merged = (jax.nn.sigmoid(ga_ref[...].astype(F32)) * ya
              + jax.nn.sigmoid(gb_ref[...].astype(F32)) * yb).astype(BF16)
    y = jnp.dot(merged, wo_ref[...], preferred_element_type=F32)
    x1 = x_ref[...] + g1_ref[...] * y
    ms = jnp.mean(x1 * x1, axis=-1, keepdims=True)
    h2 = (x1 * lax.rsqrt(ms + EPS) * n2_ref[...]) * (1.0 + sc_ref[...]) + sh_ref[...]
    half = h2.shape[1] // 2
    _store_token_major(hp_ref, _pack_bf16_pair(h2[:, :half], h2[:, half:]))
    hh = h2.astype(BF16)
    a = (_silu(jnp.dot(hh, sg_ref[...], preferred_element_type=F32))
         * jnp.dot(hh, su_ref[...], preferred_element_type=F32)).astype(BF16)
    base_ref[...] = x1 + g2_ref[...] * jnp.dot(a, sd_ref[...], preferred_element_type=F32)
    hl = (h2 - hh.astype(F32)).astype(BF16)
    lg_ref[...] = (lax.dot_general(wrh_ref[...], hh, NT_DIMS, preferred_element_type=F32)
                   + lax.dot_general(wrh_ref[...], hl, NT_DIMS, preferred_element_type=F32)
                   + lax.dot_general(wrl_ref[...], hh, NT_DIMS, preferred_element_type=F32))


def _merge_out(o_hg, o_da, proj, col_ga, col_gb, x2, gate1, scale2, shift2, gate2, norm2_g,
               wa, wb, wo, wr_hi, wr_lo, sg, su, sd, seq):
    ntok, d = x2.shape
    tm = min(MERGE_TILE, seq)
    per_b = seq // tm
    wa_w = o_hg.shape[1]
    wb_w = o_da.shape[1]
    de = sg.shape[1]

    def const(shape):
        return pl.BlockSpec(shape, lambda i: (0,) * len(shape), pipeline_mode=pl.Buffered(1))

    def perb():
        return pl.BlockSpec((None, 1, d), lambda i: (i // per_b, 0, 0))

    return pl.pallas_call(
        _merge_kernel,
        grid=(ntok // tm,),
        in_specs=[pl.BlockSpec((tm, wa_w), lambda i: (i, 0)),
                  pl.BlockSpec((tm, wb_w), lambda i: (i, 0)),
                  pl.BlockSpec((tm, d), lambda i: (i, col_ga // d)),
                  pl.BlockSpec((tm, d), lambda i: (i, col_gb // d)),
                  pl.BlockSpec((tm, d), lambda i: (i, 0)),
                  perb(), perb(), perb(), perb(),
                  const((1, d)),
                  const((wa_w, d)), const((wb_w, d)), const((d, d)),
                  const((N_EXPERTS, d)), const((N_EXPERTS, d)),
                  const((d, de)), const((d, de)), const((de, d))],
        out_specs=[pl.BlockSpec((tm, d), lambda i: (i, 0)),
                   pl.BlockSpec((tm * ROW_TILE, HEAD_W), lambda i: (i, 0)),
                   pl.BlockSpec((N_EXPERTS, tm), lambda i: (0, i))],
        out_shape=[jax.ShapeDtypeStruct((ntok, d), F32),
                   jax.ShapeDtypeStruct((ntok * ROW_TILE, HEAD_W), U32),
                   jax.ShapeDtypeStruct((N_EXPERTS, ntok), F32)],
        compiler_params=_cparams(("arbitrary",)),
    )(o_hg, o_da, proj, proj, x2, gate1[:, None, :], scale2[:, None, :], shift2[:, None, :],
      gate2[:, None, :], norm2_g.reshape(1, d), wa, wb, wo, wr_hi, wr_lo, sg, su, sd)


def _route_kernel(lg_ref, rb_ref, te_ref, gw_ref, rk_ref, cnt_ref, carry_scr, *, tt):
    @pl.when(pl.program_id(0) == 0)
    def _():
        carry_scr[...] = jnp.zeros_like(carry_scr)

    per_g = N_EXPERTS // N_GROUPS
    scores = jax.nn.sigmoid(lg_ref[...])
    sel = scores + rb_ref[...]
    sel3 = sel.reshape(N_GROUPS, per_g, tt)
    j_io = lax.broadcasted_iota(I32, (N_GROUPS, per_g, tt), 1)
    m1 = jnp.max(sel3, axis=1, keepdims=True)
    i1 = jnp.min(jnp.where(sel3 == m1, j_io, per_g), axis=1, keepdims=True)
    m2 = jnp.max(jnp.where(j_io == i1, -jnp.inf, sel3), axis=1, keepdims=True)
    gs = (m1 + m2).reshape(N_GROUPS, tt)
    g_io = lax.broadcasted_iota(I32, (N_GROUPS, tt), 0)
    gmask = jnp.zeros((N_GROUPS, tt), jnp.bool_)
    for _ in range(TOPK_GROUPS):
        gm = jnp.max(gs, axis=0, keepdims=True)
        gi = jnp.min(jnp.where(gs == gm, g_io, N_GROUPS), axis=0, keepdims=True)
        hit = g_io == gi
        gmask = gmask | hit
        gs = jnp.where(hit, -jnp.inf, gs)
    emask = jnp.broadcast_to(gmask.reshape(N_GROUPS, 1, tt), (N_GROUPS, per_g, tt)).reshape(N_EXPERTS, tt)
    cand = jnp.where(emask, sel, -jnp.inf)
    e_io = lax.broadcasted_iota(I32, (N_EXPERTS, tt), 0)
    chosen = jnp.zeros((N_EXPERTS, tt), jnp.bool_)
    picks = []
    for _ in range(TOP_K):
        em = jnp.max(cand, axis=0, keepdims=True)
        ei = jnp.min(jnp.where(cand == em, e_io, N_EXPERTS), axis=0, keepdims=True)
        hit = e_io == ei
        chosen = chosen | hit
        cand = jnp.where(hit, -jnp.inf, cand)
        picks.append((ei, hit))
    gsel = jnp.where(chosen, scores, 0.0)
    wnorm = gsel / jnp.sum(gsel, axis=0, keepdims=True) * ROUTE_SCALE
    ch = jnp.where(chosen, 1.0, 0.0)
    tri = (lax.broadcasted_iota(I32, (tt, tt), 0) <= lax.broadcasted_iota(I32, (tt, tt), 1))
    incl = jnp.dot(ch.astype(BF16), jnp.where(tri, 1.0, 0.0).astype(BF16), preferred_element_type=F32)
    carry = carry_scr[...]
    excl = incl - ch + carry
    carry_new = carry + incl[:, tt - 1:tt]
    carry_scr[...] = carry_new
    for r, (ei, hit) in enumerate(picks):
        te_ref[r:r + 1, :] = ei
        gw_ref[r:r + 1, :] = jnp.sum(jnp.where(hit, wnorm, 0.0), axis=0, keepdims=True)
        rk_ref[r:r + 1, :] = jnp.sum(jnp.where(hit, excl, 0.0), axis=0, keepdims=True).astype(I32)
    cnt_ref[...] = jnp.broadcast_to(carry_new, cnt_ref.shape).astype(I32)


def _route(logits_t, router_bias):
    ntok = logits_t.shape[1]
    tt = min(ROUTE_TILE, ntok)
    return pl.pallas_call(
        functools.partial(_route_kernel, tt=tt),
        grid=(ntok // tt,),
        in_specs=[pl.BlockSpec((N_EXPERTS, tt), lambda i: (0, i)),
                  pl.BlockSpec((N_EXPERTS, 1), lambda i: (0, 0))],
        out_specs=[pl.BlockSpec((TOP_K, tt), lambda i: (0, i)),
                   pl.BlockSpec((TOP_K, tt), lambda i: (0, i)),
                   pl.BlockSpec((TOP_K, tt), lambda i: (0, i)),
                   pl.BlockSpec((N_EXPERTS, 128), lambda i: (0, 0))],
        out_shape=[jax.ShapeDtypeStruct((TOP_K, ntok), I32),
                   jax.ShapeDtypeStruct((TOP_K, ntok), F32),
                   jax.ShapeDtypeStruct((TOP_K, ntok), I32),
                   jax.ShapeDtypeStruct((N_EXPERTS, 128), I32)],
        scratch_shapes=[pltpu.VMEM((N_EXPERTS, 1), F32)],
        compiler_params=_cparams(("arbitrary",)),
    )(logits_t, router_bias.reshape(N_EXPERTS, 1))


def _slots_kernel(ps_ref, te_ref, rk_ref, d_ref):
    te = te_ref[...]
    dest = rk_ref[...]
    for e in range(N_EXPERTS):
        dest = dest + jnp.where(te == e, ps_ref[e], 0)
    d_ref[...] = dest


def _slots(pad_start, top_e, rank):
    ntok = top_e.shape[1]
    tt = min(2048, ntok)
    return pl.pallas_call(
        _slots_kernel,
        grid=(ntok // tt,),
        in_specs=[pl.BlockSpec(memory_space=pltpu.SMEM),
                  pl.BlockSpec((TOP_K, tt), lambda i: (0, i)),
                  pl.BlockSpec((TOP_K, tt), lambda i: (0, i))],
        out_specs=pl.BlockSpec((TOP_K, tt), lambda i: (0, i)),
        out_shape=jax.ShapeDtypeStruct((TOP_K, ntok), I32),
        compiler_params=_cparams(("arbitrary",)),
    )(pad_start, top_e, rank)


GATHER_UNROLL = 8


def _expert_kernel(be_ref, nu_ref, idx_ref, idxn_ref, h_hbm, wg_ref, wu_ref, wd_ref, o_ref,
                   xs, wgb, wub, wdb, sem, *, tm):
    i = pl.program_id(0)
    n_used = nu_ref[0]
    slot = i % 2

    def row_copy(tok, s, r):
        return pltpu.make_async_copy(h_hbm.at[tok], xs.at[s, pl.ds(r * ROW_TILE, ROW_TILE)], sem.at[s])

    def issue(idx_r, s):
        def body(g, carry):
            for u in range(GATHER_UNROLL):
                r = g * GATHER_UNROLL + u
                row_copy(idx_r[0, r], s, r).start(priority=u % 2)
            return carry
        lax.fori_loop(0, tm // GATHER_UNROLL, body, 0)

    @pl.when(i == 0)
    def _():
        issue(idx_ref, 0)

    @pl.when(i + 1 < n_used)
    def _():
        issue(idxn_ref, 1 - slot)

    @pl.when(i < n_used)
    def _():
        e = be_ref[i]
        e_prev = be_ref[jnp.maximum(i - 1, 0)]

        @pl.when((i == 0) | (e != e_prev))
        def _():
            wgb[...] = wg_ref[...].astype(BF16)
            wub[...] = wu_ref[...].astype(BF16)
            wdb[...] = wd_ref[...].astype(BF16)

        def wait_body(g, carry):
            for u in range(GATHER_UNROLL):
                row_copy(0, slot, g * GATHER_UNROLL + u).wait()
            return carry
        lax.fori_loop(0, tm // GATHER_UNROLL, wait_body, 0)

        xlo, xhi = _unpack_bf16_pair(jnp.concatenate(_load_token_major(xs.at[slot], 0, tm), axis=1))
        xlo = xlo.astype(BF16)
        xhi = xhi.astype(BF16)
        half = xlo.shape[1]

        def proj_in(w):
            return (jnp.dot(xlo, w[:half, :], preferred_element_type=F32)
                    + jnp.dot(xhi, w[half:, :], preferred_element_type=F32))

        a = (_silu(proj_in(wgb)) * proj_in(wub)).astype(BF16)
        y = jnp.dot(a, wdb[...], preferred_element_type=F32)
        _store_token_major(o_ref, _pack_bf16_pair(y[:, :half], y[:, half:]))

    @pl.when(i >= n_used)
    def _():
        o_ref[...] = jnp.zeros_like(o_ref)


def _routed_experts(h2p, slot_tok, block_e, n_used, w_gate, w_up, w_down, tm):
    n_blocks = block_e.shape[0]
    d, de = w_gate.shape[-2:]
    h3 = h2p.reshape(-1, ROW_TILE, HEAD_W)
    idx3 = slot_tok.reshape(n_blocks, 1, tm)

    def wspec(shape):
        return pl.BlockSpec((None, None) + shape, lambda i, be, nu: (0, be[i], 0, 0))

    grid_spec = pltpu.PrefetchScalarGridSpec(
        num_scalar_prefetch=2,
        grid=(n_blocks,),
        in_specs=[pl.BlockSpec((None, 1, tm), lambda i, be, nu: (i, 0, 0), memory_space=pltpu.SMEM),
                  pl.BlockSpec((None, 1, tm), lambda i, be, nu: (jnp.minimum(i + 1, n_blocks - 1), 0, 0),
                               memory_space=pltpu.SMEM),
                  pl.BlockSpec(memory_space=pl.ANY),
                  wspec((d, de)), wspec((d, de)), wspec((de, d))],
        out_specs=pl.BlockSpec((tm * ROW_TILE, HEAD_W), lambda i, be, nu: (i, 0)),
        scratch_shapes=[pltpu.VMEM((2, tm * ROW_TILE, HEAD_W), U32),
                        pltpu.VMEM((d, de), BF16), pltpu.VMEM((d, de), BF16), pltpu.VMEM((de, d), BF16),
                        pltpu.SemaphoreType.DMA((2,))],
    )
    return pl.pallas_call(
        functools.partial(_expert_kernel, tm=tm),
        grid_spec=grid_spec,
        out_shape=jax.ShapeDtypeStruct((n_blocks * tm * ROW_TILE, HEAD_W), U32),
        compiler_params=_cparams(("arbitrary",), disable_bounds_checks=True),
    )(block_e, n_used, idx3, idx3, h3, w_gate, w_up, w_down)


def _combine_kernel(d_ref, dn_ref, ys_hbm, gw_ref, base_ref, g2_ref, o_ref, buf, sem, *, tm):
    i = pl.program_id(0)
    n = pl.num_programs(0)
    slot = i % 2

    def row_copy(src, s, r, t):
        return pltpu.make_async_copy(ys_hbm.at[src], buf.at[s, pl.ds((r * tm + t) * ROW_TILE, ROW_TILE)],
                                     sem.at[s])

    def issue(d_r, s):
        def body(t, carry):
            for r in range(TOP_K):
                row_copy(d_r[r, t], s, r, t).start(priority=r % 2)
            return carry
        lax.fori_loop(0, tm, body, 0)

    @pl.when(i == 0)
    def _():
        issue(d_ref, 0)

    @pl.when(i + 1 < n)
    def _():
        issue(dn_ref, 1 - slot)

    def wait_body(t, carry):
        for r in range(TOP_K):
            row_copy(0, slot, r, t).wait()
        return carry
    lax.fori_loop(0, tm, wait_body, 0)

    half = ROW_TILE * HEAD_W
    acc_lo = [jnp.zeros((tm, HEAD_W), F32)] * ROW_TILE
    acc_hi = [jnp.zeros((tm, HEAD_W), F32)] * ROW_TILE
    for r in range(TOP_K):
        w = jnp.broadcast_to(gw_ref[:, r:r + 1], (tm, HEAD_W))
        for j, piece in enumerate(_load_token_major(buf.at[slot], r * tm, tm)):
            lo, hi = _unpack_bf16_pair(piece)
            acc_lo[j] = acc_lo[j] + w * lo
            acc_hi[j] = acc_hi[j] + w * hi
    for j in range(ROW_TILE):
        lo_sl = slice(j * HEAD_W, (j + 1) * HEAD_W)
        hi_sl = slice(half + j * HEAD_W, half + (j + 1) * HEAD_W)
        o_ref[:, lo_sl] = base_ref[:, lo_sl] + g2_ref[:, lo_sl] * acc_lo[j]
        o_ref[:, hi_sl] = base_ref[:, hi_sl] + g2_ref[:, hi_sl] * acc_hi[j]


def _combine(ys, dest, gate_w, base, gate2, seq):
    ntok, d = base.shape
    y3 = ys.reshape(-1, ROW_TILE, HEAD_W)
    tm = min(COMBINE_TILE, seq)
    nt = ntok // tm
    per_b = seq // tm
    d3 = dest.reshape(TOP_K, nt, tm).transpose(1, 0, 2)
    return pl.pallas_call(
        functools.partial(_combine_kernel, tm=tm),
        grid=(nt,),
        in_specs=[pl.BlockSpec((None, TOP_K, tm), lambda i: (i, 0, 0), memory_space=pltpu.SMEM),
                  pl.BlockSpec((None, TOP_K, tm), lambda i: (jnp.minimum(i + 1, nt - 1), 0, 0),
                               memory_space=pltpu.SMEM),
                  pl.BlockSpec(memory_space=pl.ANY),
                  pl.BlockSpec((tm, TOP_K), lambda i: (i, 0)),
                  pl.BlockSpec((tm, d), lambda i: (i, 0)),
                  pl.BlockSpec((None, 1, d), lambda i: (i // per_b, 0, 0))],
        out_specs=pl.BlockSpec((tm, d), lambda i: (i, 0)),
        out_shape=jax.ShapeDtypeStruct((ntok, d), F32),
        scratch_shapes=[pltpu.VMEM((2, TOP_K * tm * ROW_TILE, HEAD_W), U32), pltpu.SemaphoreType.DMA((2,))],
        compiler_params=_cparams(("arbitrary",), disable_bounds_checks=True),
    )(d3, d3, y3, gate_w.T, base, gate2[:, None, :])


def kernel(x, c, ada_w, ada_b, norm1_g, w_in, lb_logits, hg_norm_g, q_norm_g, k_norm_g, lambda_q1, lambda_k1,
           lambda_q2, lambda_k2, da_norm_g, rel_bias, w_branch_a, w_branch_b, w_out, norm2_g, router_w,
           router_bias, w_exp_gate, w_exp_up, w_exp_down, w_sh_gate, w_sh_up, w_sh_down):
    bsz, seq, d = x.shape
    ntok = bsz * seq
    l = 0
    x2 = x.reshape(ntok, d)

    mod = _ada_mod(c, ada_w[l], ada_b[l])
    shift1, scale1, gate1, shift2, scale2, gate2 = jnp.split(mod, 6, axis=-1)

    n_gate = 2 * d
    split = w_in.shape[2] - n_gate
    w_in_bf = jnp.concatenate([w_in[l][:, split:], w_in[l][:, :split]], axis=1).astype(BF16)
    proj = _in_projection(x2, scale1, shift1, norm1_g[l], w_in_bf, seq)
    col_hg = n_gate
    col_q = col_hg + 4 * HG_HEADS * HEAD_W
    col_k = col_q + DA_HEADS * HEAD_W
    col_v = col_k + DA_HEADS * HEAD_W

    o_hg = _hgrn_branch(proj, lb_logits, hg_norm_g[l], bsz, seq, col_hg)
    qn, kn = _qk_norm(proj, q_norm_g[l], k_norm_g[l], col_q, col_k)
    lam_params = jnp.stack([lambda_q1[l], lambda_k1[l], lambda_q2[l], lambda_k2[l]])
    o_da = _diff_attention(qn, kn, proj, col_v, rel_bias, lam_params, da_norm_g[l], bsz, seq)

    wr_t = router_w[l].T
    wr_hi = wr_t.astype(BF16)
    wr_lo = (wr_t - wr_hi.astype(F32)).astype(BF16)
    base, h2p, logits_t = _merge_out(
        o_hg, o_da, proj, 0, d, x2, gate1, scale2, shift2, gate2, norm2_g[l],
        w_branch_a[l].astype(BF16), w_branch_b[l].astype(BF16), w_out[l].astype(BF16), wr_hi, wr_lo,
        w_sh_gate[l].astype(BF16), w_sh_up[l].astype(BF16), w_sh_down[l].astype(BF16), seq)

    top_e, gate_w, rank, counts = _route(logits_t, router_bias[l])

    tm_e = EXPERT_TILE
    n_blocks = (ntok * TOP_K) // tm_e + N_EXPERTS
    nblk = (counts[:, 0] + tm_e - 1) // tm_e
    blk_end = jnp.cumsum(nblk)
    pad_start = ((blk_end - nblk) * tm_e).astype(I32)
    n_used = blk_end[-1:].astype(I32)
    block_e = jnp.minimum(jnp.sum(blk_end[None, :] <= jnp.arange(n_blocks, dtype=I32)[:, None], axis=1),
                          N_EXPERTS - 1).astype(I32)

    dest = _slots(pad_start, top_e, rank)
    tok = jnp.broadcast_to(jnp.arange(ntok, dtype=I32)[None, :], dest.shape)
    slot_tok = jnp.zeros((n_blocks * tm_e,), I32).at[dest.reshape(-1)].set(tok.reshape(-1), unique_indices=True)
    ys = _routed_experts(h2p, slot_tok, block_e, n_used, w_exp_gate, w_exp_up, w_exp_down, tm_e)
    out = _combine(ys, dest, gate_w, base, gate2, seq)
    return out.reshape(bsz, seq, d)
```

```python
import functools
import math

import numpy as np
import jax
import jax.numpy as jnp
from jax import lax
from jax.experimental import pallas as pl
from jax.experimental.pallas import tpu as pltpu

F32 = jnp.float32
BF16 = jnp.bfloat16
I32 = jnp.int32
U32 = jnp.uint32

HG_HEADS = 8
HG_DK = 128
HG_DV = 128
DA_HEADS = 8
DA_DH = 64
DA_DV = 128
RP_BUCKETS = 32
RP_MAX_EXACT = 16
RP_MAX_DIST = 128
N_EXPERTS = 64
N_GROUPS = 8
TOPK_GROUPS = 4
TOP_K = 8
ROUTE_SCALE = 2.5
EPS = 1e-6
LAMBDA_INIT = 0.8 - 0.6 * math.exp(-0.3 * 0)

HEAD_W = 128
VMEM_LIMIT_BYTES = 56 * 1024 * 1024
NEG_BIG = -1e30
LOG2E = math.log2(math.e)

ATTN_TILE = 512
HGRN_TILE = 256
MERGE_TILE = 256
EXPERT_TILE = 256
COMBINE_TILE = 128
ROUTE_TILE = 512

NT_DIMS = (((1,), (1,)), ((), ()))
TN_DIMS = (((0,), (0,)), ((), ()))


def _cparams(sem, **kw):
    return pltpu.CompilerParams(dimension_semantics=sem, vmem_limit_bytes=VMEM_LIMIT_BYTES, **kw)


def _silu(x):
    return x * jax.nn.sigmoid(x)


HI_MASK = np.uint32(0xFFFF0000)


def _pack_bf16_pair(a, b):
    ua = lax.bitcast_convert_type(a.astype(BF16).astype(F32), U32)
    ub = lax.bitcast_convert_type(b.astype(BF16).astype(F32), U32)
    return (ua >> 16) | (ub & HI_MASK)


def _unpack_bf16_pair(w):
    lo = lax.bitcast_convert_type(w << 16, F32)
    hi = lax.bitcast_convert_type(w & HI_MASK, F32)
    return lo, hi


ROW_TILE = 8


def _store_token_major(ref, x):
    rows = x.shape[0]
    for j in range(ROW_TILE):
        ref[pl.ds(j, rows, stride=ROW_TILE), :] = x[:, j * HEAD_W:(j + 1) * HEAD_W]


def _load_token_major(ref, first, rows):
    return [ref[pl.ds(first * ROW_TILE + j, rows, stride=ROW_TILE), :] for j in range(ROW_TILE)]


def _t5_bucket_starts():
    n = np.arange(0, RP_MAX_DIST + 1)
    nf = np.maximum(n, 1).astype(np.float32)
    large = RP_MAX_EXACT + (np.log(nf / np.float32(RP_MAX_EXACT)) / np.float32(math.log(RP_MAX_DIST / RP_MAX_EXACT))
                            * np.float32(RP_BUCKETS - RP_MAX_EXACT)).astype(np.int32)
    large = np.minimum(large, RP_BUCKETS - 1)
    bucket = np.where(n < RP_MAX_EXACT, n, large)
    assert np.all(np.diff(bucket) >= 0) and bucket[-1] == RP_BUCKETS - 1
    return [int(np.argmax(bucket >= j)) for j in range(RP_BUCKETS)]


BUCKET_START = _t5_bucket_starts()


def _ada_kernel(c_ref, w_ref, b_ref, o_ref):
    ca = _silu(c_ref[...]).astype(BF16)
    o_ref[...] = jnp.dot(ca, w_ref[...].astype(BF16), preferred_element_type=F32) + b_ref[...]


def _ada_mod(c, ada_w, ada_b):
    bsz, d = c.shape
    n = ada_w.shape[1]
    rows = 8
    cp = jnp.zeros((rows, d), F32).at[:bsz].set(c)
    tn = 1024
    out = pl.pallas_call(
        _ada_kernel,
        grid=(n // tn,),
        in_specs=[pl.BlockSpec((rows, d), lambda j: (0, 0)),
                  pl.BlockSpec((d, tn), lambda j: (0, j)),
                  pl.BlockSpec((1, tn), lambda j: (0, j))],
        out_specs=pl.BlockSpec((rows, tn), lambda j: (0, j)),
        out_shape=jax.ShapeDtypeStruct((rows, n), F32),
        compiler_params=_cparams(("arbitrary",)),
    )(cp, ada_w, ada_b.reshape(1, n))
    return out[:bsz]


def _inproj_kernel(x_ref, sc_ref, sh_ref, g_ref, w_ref, o_ref, h_scr):
    @pl.when(pl.program_id(1) == 0)
    def _():
        x = x_ref[...]
        ms = jnp.mean(x * x, axis=-1, keepdims=True)
        hn = x * lax.rsqrt(ms + EPS) * g_ref[...]
        h_scr[...] = (hn * (1.0 + sc_ref[...]) + sh_ref[...]).astype(BF16)

    o_ref[...] = jnp.dot(h_scr[...], w_ref[...], preferred_element_type=F32).astype(o_ref.dtype)


def _in_projection(x2, scale, shift, g, w_bf, seq):
    ntok, d = x2.shape
    n = w_bf.shape[1]
    tm = min(1024, seq)
    tn = 1024
    per_b = seq // tm
    return pl.pallas_call(
        _inproj_kernel,
        grid=(ntok // tm, n // tn),
        in_specs=[pl.BlockSpec((tm, d), lambda i, j: (i, 0)),
                  pl.BlockSpec((None, 1, d), lambda i, j: (i // per_b, 0, 0)),
                  pl.BlockSpec((None, 1, d), lambda i, j: (i // per_b, 0, 0)),
                  pl.BlockSpec((1, d), lambda i, j: (0, 0)),
                  pl.BlockSpec((d, tn), lambda i, j: (0, j))],
        out_specs=pl.BlockSpec((tm, tn), lambda i, j: (i, j)),
        out_shape=jax.ShapeDtypeStruct((ntok, n), BF16),
        scratch_shapes=[pltpu.VMEM((tm, d), BF16)],
        compiler_params=_cparams(("arbitrary", "arbitrary")),
    )(x2, scale[:, None, :], shift[:, None, :], g.reshape(1, d), w_bf)


def _hgrn_kernel(q_ref, f_ref, i_ref, g_ref, lbl_ref, ng_ref, o_ref, state_ref, *, tile):
    @pl.when(pl.program_id(1) == 0)
    def _():
        state_ref[...] = jnp.zeros_like(state_ref)

    n_levels = tile.bit_length() - 1
    row = lax.broadcasted_iota(I32, (tile, tile), 0)
    col = lax.broadcasted_iota(I32, (tile, tile), 1)
    lev = jnp.where(row >= col, 31 - lax.clz(row ^ col), -2)
    on_diag = lev == -1
    at_level = [lev == lvl for lvl in range(n_levels)]
    rowk = lax.broadcasted_iota(I32, (tile, HG_DK), 0)
    odd_at = [(rowk & (1 << lvl)) != 0 for lvl in range(n_levels)]

    for h in range(HG_HEADS):
        sl = slice(h * HEAD_W, (h + 1) * HEAD_W)
        q = _silu(q_ref[:, sl].astype(F32)) * (HG_DK ** -0.5)
        ll = lbl_ref[:, sl]
        el = jnp.exp(ll - jnp.max(ll, axis=0, keepdims=True))
        lb = el[0:1, :] / jnp.sum(el, axis=0, keepdims=True)
        fg = lb + (1.0 - lb) * jax.nn.sigmoid(f_ref[:, sl].astype(F32))
        k = 1.0 - fg
        g = jnp.log(fg)
        v = i_ref[:, sl]

        scores = jnp.where(on_diag,
                           lax.dot_general(q.astype(BF16), k.astype(BF16), NT_DIMS, preferred_element_type=F32),
                           0.0)
        c = g
        e = g
        for lvl in range(n_levels):
            blk = 1 << lvl
            qd = (q * jnp.exp(c)).astype(BF16)
            kd = (k * jnp.exp(e - c)).astype(BF16)
            s_l = lax.dot_general(qd, kd, NT_DIMS, preferred_element_type=F32)
            scores = jnp.where(at_level[lvl], s_l, scores)
            odd = odd_at[lvl]
            e_prev = pltpu.roll(e, blk, axis=0)
            e_next = pltpu.roll(e, tile - blk, axis=0)
            c = c + jnp.where(odd, e_prev, 0.0)
            e = e + jnp.where(odd, e_prev, e_next)
        st = state_ref[h]
        qb = (q * jnp.exp(c)).astype(BF16)
        o = (jnp.dot(scores.astype(BF16), v, preferred_element_type=F32)
             + lax.dot_general(qb, st.astype(BF16), NT_DIMS, preferred_element_type=F32))
        kd = (k * jnp.exp(e - c)).astype(BF16)
        state_ref[h] = st * jnp.exp(e[0:1, :]) + lax.dot_general(v, kd, TN_DIMS, preferred_element_type=F32)

        ms = jnp.mean(o * o, axis=-1, keepdims=True)
        gate = _silu(g_ref[:, sl].astype(F32))
        o_ref[:, sl] = (o * lax.rsqrt(ms + EPS) * ng_ref[...] * gate).astype(o_ref.dtype)


def _hgrn_branch(proj, lb_logits, norm_g, bsz, seq, col0):
    ntok = proj.shape[0]
    tile = min(HGRN_TILE, seq)
    nt = seq // tile
    width = HG_HEADS * HEAD_W
    cb = col0 // width

    def spec(off):
        return pl.BlockSpec((tile, width), lambda b, t, off=off: (b * nt + t, cb + off))

    return pl.pallas_call(
        functools.partial(_hgrn_kernel, tile=tile),
        grid=(bsz, nt),
        in_specs=[spec(0), spec(1), spec(2), spec(3),
                  pl.BlockSpec(lb_logits.shape, lambda b, t: (0, 0)),
                  pl.BlockSpec((1, HG_DV), lambda b, t: (0, 0))],
        out_specs=pl.BlockSpec((tile, width), lambda b, t: (b * nt + t, 0)),
        out_shape=jax.ShapeDtypeStruct((ntok, width), BF16),
        scratch_shapes=[pltpu.VMEM((HG_HEADS, HG_DV, HG_DK), F32)],
        compiler_params=_cparams(("arbitrary", "arbitrary")),
    )(proj, proj, proj, proj, lb_logits, norm_g.reshape(1, HG_DV))


def _qknorm_kernel(q_ref, k_ref, qg_ref, kg_ref, qo_ref, ko_ref):
    lane = lax.broadcasted_iota(I32, (q_ref.shape[0], HEAD_W), 1)
    lo = lane < DA_DH

    def norm(x_ref, g_ref, o_ref, scale):
        for h in range(DA_HEADS):
            sl = slice(h * HEAD_W, (h + 1) * HEAD_W)
            x = x_ref[:, sl].astype(F32)
            xx = x * x
            s0 = jnp.sum(jnp.where(lo, xx, 0.0), axis=-1, keepdims=True)
            s1 = jnp.sum(jnp.where(lo, 0.0, xx), axis=-1, keepdims=True)
            ms = jnp.where(lo, s0, s1) * (1.0 / DA_DH)
            o_ref[:, sl] = (x * lax.rsqrt(ms + EPS) * g_ref[...] * scale).astype(o_ref.dtype)

    norm(q_ref, qg_ref, qo_ref, DA_DH ** -0.5 * LOG2E)
    norm(k_ref, kg_ref, ko_ref, 1.0)


def _qk_norm(proj, q_g, k_g, col_q, col_k):
    ntok = proj.shape[0]
    width = DA_HEADS * HEAD_W
    tm = min(512, ntok)
    qg = jnp.tile(q_g, 2).reshape(1, HEAD_W)
    kg = jnp.tile(k_g, 2).reshape(1, HEAD_W)
    return pl.pallas_call(
        _qknorm_kernel,
        grid=(ntok // tm,),
        in_specs=[pl.BlockSpec((tm, width), lambda i: (i, col_q // width)),
                  pl.BlockSpec((tm, width), lambda i: (i, col_k // width)),
                  pl.BlockSpec((1, HEAD_W), lambda i: (0, 0)),
                  pl.BlockSpec((1, HEAD_W), lambda i: (0, 0))],
        out_specs=[pl.BlockSpec((tm, width), lambda i: (i, 0)),
                   pl.BlockSpec((tm, width), lambda i: (i, 0))],
        out_shape=[jax.ShapeDtypeStruct((ntok, width), BF16)] * 2,
        compiler_params=_cparams(("arbitrary",)),
    )(proj, proj, qg, kg)


def _attn_kernel(qi_ref, ki_ref, q_ref, k_ref, v_ref, rel_ref, lamp_ref, ng_ref, o_ref,
                 m_scr, acc_scr, bias_scr, *, tile):
    b = pl.program_id(0)
    p = pl.program_id(1)
    qi = qi_ref[p]
    ki = ki_ref[p]
    diff = qi - ki

    row = lax.broadcasted_iota(I32, (tile, tile), 0)
    col = lax.broadcasted_iota(I32, (tile, tile), 1)

    @pl.when((b == 0) & (p == 0))
    def _():
        def per_head(h, carry):
            for d in range(2):
                dist = row - col + d * tile
                bias = jnp.full((tile, tile), rel_ref[0, h], F32)
                for j in range(1, RP_BUCKETS):
                    bias = jnp.where(dist >= BUCKET_START[j], rel_ref[j, h], bias)
                bias_scr[d, h] = bias * LOG2E
            return carry
        lax.fori_loop(0, DA_HEADS, per_head, 0)

    @pl.when(ki == 0)
    def _():
        m_scr[...] = jnp.full_like(m_scr, NEG_BIG)
        acc_scr[...] = jnp.zeros_like(acc_scr)

    lane = lax.broadcasted_iota(I32, (tile, HEAD_W), 1)
    lo = lane < DA_DH
    ones = jnp.ones((tile, HEAD_W), BF16)
    reps = tile // HEAD_W

    def step(kind):
        for h in range(DA_HEADS):
            sl = slice(h * HEAD_W, (h + 1) * HEAD_W)
            q = q_ref[:, sl]
            k = k_ref[:, sl]
            v_aug = jnp.concatenate([v_ref[:, sl], ones], axis=1)
            for c in range(2):
                qc = jnp.where(lo, q, jnp.zeros_like(q)) if c == 0 else jnp.where(lo, jnp.zeros_like(q), q)
                s = lax.dot_general(qc, k, NT_DIMS, preferred_element_type=F32)
                if kind == 2:
                    s = s + rel_ref[RP_BUCKETS - 1, h] * LOG2E
                else:
                    s = s + bias_scr[kind, h]
                if kind == 0:
                    s = jnp.where(row >= col, s, NEG_BIG)
                idx = 2 * h + c
                m_old = m_scr[idx]
                m_cur = jnp.broadcast_to(jnp.max(s, axis=-1, keepdims=True), (tile, HEAD_W))
                m_new = jnp.maximum(m_old, m_cur)
                alpha = jnp.exp2(m_old - m_new)
                pr = jnp.exp2(s - jnp.concatenate([m_new] * reps, axis=1))
                pv = jnp.dot(pr.astype(BF16), v_aug, preferred_element_type=F32)
                acc_scr[idx] = jnp.concatenate([alpha, alpha], axis=1) * acc_scr[idx] + pv
                m_scr[idx] = m_new

    for kind, cond in ((0, diff == 0), (1, diff == 1), (2, diff > 1)):
        pl.when(cond)(functools.partial(step, kind))

    @pl.when(diff == 0)
    def _():
        lp = lamp_ref[...]
        lam = (jnp.exp(jnp.sum(lp[0:1] * lp[1:2], axis=-1, keepdims=True))
               - jnp.exp(jnp.sum(lp[2:3] * lp[3:4], axis=-1, keepdims=True)) + LAMBDA_INIT)
        for h in range(DA_HEADS):
            sl = slice(h * HEAD_W, (h + 1) * HEAD_W)
            a0 = acc_scr[2 * h]
            a1 = acc_scr[2 * h + 1]
            o = a0[:, :DA_DV] / a0[:, DA_DV:] - lam * (a1[:, :DA_DV] / a1[:, DA_DV:])
            ms = jnp.mean(o * o, axis=-1, keepdims=True)
            o_ref[:, sl] = (o * lax.rsqrt(ms + EPS) * ng_ref[...] * (1.0 - LAMBDA_INIT)).astype(o_ref.dtype)


def _diff_attention(qn, kn, proj, col_v, rel_bias, lam_params, norm_g, bsz, seq):
    ntok = qn.shape[0]
    tile = min(ATTN_TILE, seq)
    nq = seq // tile
    width = DA_HEADS * HEAD_W
    qi_list, ki_list = [], []
    for a in range(nq):
        for c in range(a + 1):
            qi_list.append(a)
            ki_list.append(c)
    qi_arr = jnp.asarray(qi_list, I32)
    ki_arr = jnp.asarray(ki_list, I32)
    cv = col_v // width
    grid_spec = pltpu.PrefetchScalarGridSpec(
        num_scalar_prefetch=2,
        grid=(bsz, len(qi_list)),
        in_specs=[pl.BlockSpec((tile, width), lambda b, p, qi, ki: (b * nq + qi[p], 0)),
                  pl.BlockSpec((tile, width), lambda b, p, qi, ki: (b * nq + ki[p], 0)),
                  pl.BlockSpec((tile, width), lambda b, p, qi, ki: (b * nq + ki[p], cv)),
                  pl.BlockSpec(memory_space=pltpu.SMEM),
                  pl.BlockSpec((4, DA_DH), lambda b, p, qi, ki: (0, 0)),
                  pl.BlockSpec((1, DA_DV), lambda b, p, qi, ki: (0, 0))],
        out_specs=pl.BlockSpec((tile, width), lambda b, p, qi, ki: (b * nq + qi[p], 0)),
        scratch_shapes=[pltpu.VMEM((2 * DA_HEADS, tile, HEAD_W), F32),
                        pltpu.VMEM((2 * DA_HEADS, tile, 2 * DA_DV), F32),
                        pltpu.VMEM((2, DA_HEADS, tile, tile), F32)],
    )
    return pl.pallas_call(
        functools.partial(_attn_kernel, tile=tile),
        grid_spec=grid_spec,
        out_shape=jax.ShapeDtypeStruct((ntok, width), BF16),
        compiler_params=_cparams(("arbitrary", "arbitrary")),
    )(qi_arr, ki_arr, qn, kn, proj, rel_bias, lam_params, norm_g.reshape(1, DA_DV))


def _merge_kernel(oa_ref, ob_ref, ga_ref, gb_ref, x_ref, g1_ref, sc_ref, sh_ref, g2_ref, n2_ref,
                  wa_ref, wb_ref, wo_ref, wrh_ref, wrl_ref, sg_ref, su_ref, sd_ref,
                  base_ref, hp_ref, lg_ref):
    ya = jnp.dot(oa_ref[...], wa_ref[...], preferred_element_type=F32)
    yb = jnp.dot(ob_ref[...], wb_ref[...], preferred_element_type=F32)
    merged = (jax.nn.sigmoid(ga_ref[...].astype(F32)) * ya
              + jax.nn.sigmoid(gb_ref[...].astype(F32)) * yb).astype(BF16)
    y = jnp.dot(merged, wo_ref[...], preferred_element_type=F32)
    x1 = x_ref[...] + g1_ref[...] * y
    ms = jnp.mean(x1 * x1, axis=-1, keepdims=True)
    h2 = (x1 * lax.rsqrt(ms + EPS) * n2_ref[...]) * (1.0 + sc_ref[...]) + sh_ref[...]
    half = h2.shape[1] // 2
    _store_token_major(hp_ref, _pack_bf16_pair(h2[:, :half], h2[:, half:]))
    hh = h2.astype(BF16)
    a = (_silu(jnp.dot(hh, sg_ref[...], preferred_element_type=F32))
         * jnp.dot(hh, su_ref[...], preferred_element_type=F32)).astype(BF16)
    base_ref[...] = x1 + g2_ref[...] * jnp.dot(a, sd_ref[...], preferred_element_type=F32)
    hl = (h2 - hh.astype(F32)).astype(BF16)
    lg_ref[...] = (lax.dot_general(wrh_ref[...], hh, NT_DIMS, preferred_element_type=F32)
                   + lax.dot_general(wrh_ref[...], hl, NT_DIMS, preferred_element_type=F32)
                   + lax.dot_general(wrl_ref[...], hh, NT_DIMS, preferred_element_type=F32))


def _merge_out(o_hg, o_da, proj, col_ga, col_gb, x2, gate1, scale2, shift2, gate2, norm2_g,
               wa, wb, wo, wr_hi, wr_lo, sg, su, sd, seq):
    ntok, d = x2.shape
    tm = min(MERGE_TILE, seq)
    per_b = seq // tm
    wa_w = o_hg.shape[1]
    wb_w = o_da.shape[1]
    de = sg.shape[1]

    def const(shape):
        return pl.BlockSpec(shape, lambda i: (0,) * len(shape), pipeline_mode=pl.Buffered(1))

    def perb():
        return pl.BlockSpec((None, 1, d), lambda i: (i // per_b, 0, 0))

    return pl.pallas_call(
        _merge_kernel,
        grid=(ntok // tm,),
        in_specs=[pl.BlockSpec((tm, wa_w), lambda i: (i, 0)),
                  pl.BlockSpec((tm, wb_w), lambda i: (i, 0)),
                  pl.BlockSpec((tm, d), lambda i: (i, col_ga // d)),
                  pl.BlockSpec((tm, d), lambda i: (i, col_gb // d)),
                  pl.BlockSpec((tm, d), lambda i: (i, 0)),
                  perb(), perb(), perb(), perb(),
                  const((1, d)),
                  const((wa_w, d)), const((wb_w, d)), const((d, d)),
                  const((N_EXPERTS, d)), const((N_EXPERTS, d)),
                  const((d, de)), const((d, de)), const((de, d))],
        out_specs=[pl.BlockSpec((tm, d), lambda i: (i, 0)),
                   pl.BlockSpec((tm * ROW_TILE, HEAD_W), lambda i: (i, 0)),
                   pl.BlockSpec((N_EXPERTS, tm), lambda i: (0, i))],
        out_shape=[jax.ShapeDtypeStruct((ntok, d), F32),
                   jax.ShapeDtypeStruct((ntok * ROW_TILE, HEAD_W), U32),
                   jax.ShapeDtypeStruct((N_EXPERTS, ntok), F32)],
        compiler_params=_cparams(("arbitrary",)),
    )(o_hg, o_da, proj, proj, x2, gate1[:, None, :], scale2[:, None, :], shift2[:, None, :],
      gate2[:, None, :], norm2_g.reshape(1, d), wa, wb, wo, wr_hi, wr_lo, sg, su, sd)


def _route_kernel(lg_ref, rb_ref, te_ref, gw_ref, rk_ref, cnt_ref, r1_ref, ex_ref, carry_scr, *, tt):
    @pl.when(pl.program_id(0) == 0)
    def _():
        carry_scr[...] = jnp.zeros_like(carry_scr)

    per_g = N_EXPERTS // N_GROUPS
    scores = jax.nn.sigmoid(lg_ref[...])
    sel = scores + rb_ref[...]
    sel3 = sel.reshape(N_GROUPS, per_g, tt)
    j_io = lax.broadcasted_iota(I32, (N_GROUPS, per_g, tt), 1)
    m1 = jnp.max(sel3, axis=1, keepdims=True)
    i1 = jnp.min(jnp.where(sel3 == m1, j_io, per_g), axis=1, keepdims=True)
    m2 = jnp.max(jnp.where(j_io == i1, -jnp.inf, sel3), axis=1, keepdims=True)
    gs = (m1 + m2).reshape(N_GROUPS, tt)
    g_io = lax.broadcasted_iota(I32, (N_GROUPS, tt), 0)
    gmask = jnp.zeros((N_GROUPS, tt), jnp.bool_)
    for _ in range(TOPK_GROUPS):
        gm = jnp.max(gs, axis=0, keepdims=True)
        gi = jnp.min(jnp.where(gs == gm, g_io, N_GROUPS), axis=0, keepdims=True)
        hit = g_io == gi
        gmask = gmask | hit
        gs = jnp.where(hit, -jnp.inf, gs)
    emask = jnp.broadcast_to(gmask.reshape(N_GROUPS, 1, tt), (N_GROUPS, per_g, tt)).reshape(N_EXPERTS, tt)
    cand = jnp.where(emask, sel, -jnp.inf)
    e_io = lax.broadcasted_iota(I32, (N_EXPERTS, tt), 0)
    chosen = jnp.zeros((N_EXPERTS, tt), jnp.bool_)
    picks = []
    for _ in range(TOP_K):
        em = jnp.max(cand, axis=0, keepdims=True)
        ei = jnp.min(jnp.where(cand == em, e_io, N_EXPERTS), axis=0, keepdims=True)
        hit = e_io == ei
        chosen = chosen | hit
        cand = jnp.where(hit, -jnp.inf, cand)
        picks.append((ei, hit))
    gsel = jnp.where(chosen, scores, 0.0)
    wnorm = gsel / jnp.sum(gsel, axis=0, keepdims=True) * ROUTE_SCALE
    ch = jnp.where(chosen, 1.0, 0.0)
    tri = (lax.broadcasted_iota(I32, (tt, tt), 0) <= lax.broadcasted_iota(I32, (tt, tt), 1))
    incl = jnp.dot(ch.astype(BF16), jnp.where(tri, 1.0, 0.0).astype(BF16), preferred_element_type=F32)
    carry = carry_scr[...]
    excl = incl - ch + carry
    carry_new = carry + incl[:, tt - 1:tt]
    carry_scr[...] = carry_new
    r1_ref[...] = jnp.where(chosen, excl + 1.0, 0.0).astype(I32)
    ex_ref[...] = excl.astype(I32)
    for r, (ei, hit) in enumerate(picks):
        te_ref[r:r + 1, :] = ei
        gw_ref[r:r + 1, :] = jnp.sum(jnp.where(hit, wnorm, 0.0), axis=0, keepdims=True)
        rk_ref[r:r + 1, :] = jnp.sum(jnp.where(hit, excl, 0.0), axis=0, keepdims=True).astype(I32)
    cnt_ref[...] = jnp.broadcast_to(carry_new, cnt_ref.shape).astype(I32)


def _route(logits_t, router_bias):
    ntok = logits_t.shape[1]
    tt = min(ROUTE_TILE, ntok)
    return pl.pallas_call(
        functools.partial(_route_kernel, tt=tt),
        grid=(ntok // tt,),
        in_specs=[pl.BlockSpec((N_EXPERTS, tt), lambda i: (0, i)),
                  pl.BlockSpec((N_EXPERTS, 1), lambda i: (0, 0))],
        out_specs=[pl.BlockSpec((TOP_K, tt), lambda i: (0, i)),
                   pl.BlockSpec((TOP_K, tt), lambda i: (0, i)),
                   pl.BlockSpec((TOP_K, tt), lambda i: (0, i)),
                   pl.BlockSpec((N_EXPERTS, 128), lambda i: (0, 0)),
                   pl.BlockSpec((N_EXPERTS, tt), lambda i: (0, i)),
                   pl.BlockSpec((N_EXPERTS, tt), lambda i: (0, i))],
        out_shape=[jax.ShapeDtypeStruct((TOP_K, ntok), I32),
                   jax.ShapeDtypeStruct((TOP_K, ntok), F32),
                   jax.ShapeDtypeStruct((TOP_K, ntok), I32),
                   jax.ShapeDtypeStruct((N_EXPERTS, 128), I32),
                   jax.ShapeDtypeStruct((N_EXPERTS, ntok), I32),
                   jax.ShapeDtypeStruct((N_EXPERTS, ntok), I32)],
        scratch_shapes=[pltpu.VMEM((N_EXPERTS, 1), F32)],
        compiler_params=_cparams(("arbitrary",)),
    )(logits_t, router_bias.reshape(N_EXPERTS, 1))


def _slots_kernel(ps_ref, te_ref, rk_ref, d_ref):
    te = te_ref[...]
    dest = rk_ref[...]
    for e in range(N_EXPERTS):
        dest = dest + jnp.where(te == e, ps_ref[e], 0)
    d_ref[...] = dest


def _slots(pad_start, top_e, rank):
    ntok = top_e.shape[1]
    tt = min(2048, ntok)
    return pl.pallas_call(
        _slots_kernel,
        grid=(ntok // tt,),
        in_specs=[pl.BlockSpec(memory_space=pltpu.SMEM),
                  pl.BlockSpec((TOP_K, tt), lambda i: (0, i)),
                  pl.BlockSpec((TOP_K, tt), lambda i: (0, i))],
        out_specs=pl.BlockSpec((TOP_K, tt), lambda i: (0, i)),
        out_shape=jax.ShapeDtypeStruct((TOP_K, ntok), I32),
        compiler_params=_cparams(("arbitrary",)),
    )(pad_start, top_e, rank)


BLOCKS_PER_STEP = 8


def _slot_tokens_kernel(be_ref, bs_ref, r_ref, base_ref, o_ref, *, tm):
    i = pl.program_id(0)
    n_ch = r_ref.shape[1]
    ch_io = lax.broadcasted_iota(I32, (n_ch, tm), 0)
    l_io = lax.broadcasted_iota(I32, (HEAD_W, tm), 0)
    j_io = lax.broadcasted_iota(I32, (1, tm), 1)
    for b in range(BLOCKS_PER_STEP):
        blk = i * BLOCKS_PER_STEP + b
        e = be_ref[blk]
        g = (blk - bs_ref[e]) * tm + j_io
        ch = jnp.sum((base_ref[e] <= g).astype(I32), axis=0, keepdims=True) - 1
        pick = jnp.where(ch_io == ch, 1.0, 0.0).astype(BF16)
        r = r_ref[e]
        hi = (r >> 7).astype(F32).astype(BF16)
        lo = (r & 127).astype(F32).astype(BF16)
        ranks = (lax.dot_general(hi, pick, TN_DIMS, preferred_element_type=F32) * 128.0
                 + lax.dot_general(lo, pick, TN_DIMS, preferred_element_type=F32))
        hit = ranks == (g + 1).astype(F32)
        lane = jnp.sum(jnp.where(hit, l_io, 0), axis=0, keepdims=True)
        o_ref[b] = ch * HEAD_W + lane


def _slot_tokens(r1, excl, block_e, blk_start, tm):
    ntok = r1.shape[1]
    n_ch = ntok // HEAD_W
    n_blocks = block_e.shape[0]
    r3 = r1.reshape(N_EXPERTS, n_ch, HEAD_W)
    base = excl[:, ::HEAD_W].reshape(N_EXPERTS, n_ch, 1)
    grid_spec = pltpu.PrefetchScalarGridSpec(
        num_scalar_prefetch=2,
        grid=(n_blocks // BLOCKS_PER_STEP,),
        in_specs=[pl.BlockSpec((N_EXPERTS, n_ch, HEAD_W), lambda i, be, bs: (0, 0, 0)),
                  pl.BlockSpec((N_EXPERTS, n_ch, 1), lambda i, be, bs: (0, 0, 0))],
        out_specs=pl.BlockSpec((BLOCKS_PER_STEP, 1, tm), lambda i, be, bs: (i, 0, 0)),
    )
    return pl.pallas_call(
        functools.partial(_slot_tokens_kernel, tm=tm),
        grid_spec=grid_spec,
        out_shape=jax.ShapeDtypeStruct((n_blocks, 1, tm), I32),
        compiler_params=_cparams(("arbitrary",)),
    )(block_e, blk_start, r3, base)


GATHER_UNROLL = 8


def _expert_kernel(be_ref, nu_ref, idx_ref, idxn_ref, h_hbm, wg_ref, wu_ref, wd_ref, o_ref,
                   xs, wgb, wub, wdb, sem, *, tm):
    i = pl.program_id(0)
    n_used = nu_ref[0]
    slot = i % 2

    def row_copy(tok, s, r):
        return pltpu.make_async_copy(h_hbm.at[tok], xs.at[s, pl.ds(r * ROW_TILE, ROW_TILE)], sem.at[s])

    def issue(idx_r, s):
        def body(g, carry):
            for u in range(GATHER_UNROLL):
                r = g * GATHER_UNROLL + u
                row_copy(idx_r[0, r], s, r).start(priority=u % 2)
            return carry
        lax.fori_loop(0, tm // GATHER_UNROLL, body, 0)

    @pl.when(i == 0)
    def _():
        issue(idx_ref, 0)

    @pl.when(i + 1 < n_used)
    def _():
        issue(idxn_ref, 1 - slot)

    @pl.when(i < n_used)
    def _():
        e = be_ref[i]
        e_prev = be_ref[jnp.maximum(i - 1, 0)]

        @pl.when((i == 0) | (e != e_prev))
        def _():
            wgb[...] = wg_ref[...].astype(BF16)
            wub[...] = wu_ref[...].astype(BF16)
            wdb[...] = wd_ref[...].astype(BF16)

        def wait_body(g, carry):
            for u in range(GATHER_UNROLL):
                row_copy(0, slot, g * GATHER_UNROLL + u).wait()
            return carry
        lax.fori_loop(0, tm // GATHER_UNROLL, wait_body, 0)

        xlo, xhi = _unpack_bf16_pair(jnp.concatenate(_load_token_major(xs.at[slot], 0, tm), axis=1))
        xlo = xlo.astype(BF16)
        xhi = xhi.astype(BF16)
        half = xlo.shape[1]

        def proj_in(w):
            return (jnp.dot(xlo, w[:half, :], preferred_element_type=F32)
                    + jnp.dot(xhi, w[half:, :], preferred_element_type=F32))

        a = (_silu(proj_in(wgb)) * proj_in(wub)).astype(BF16)
        y = jnp.dot(a, wdb[...], preferred_element_type=F32)
        _store_token_major(o_ref, _pack_bf16_pair(y[:, :half], y[:, half:]))

    @pl.when(i >= n_used)
    def _():
        o_ref[...] = jnp.zeros_like(o_ref)


def _routed_experts(h2p, idx3, block_e, n_used, w_gate, w_up, w_down, tm):
    n_blocks = block_e.shape[0]
    d, de = w_gate.shape[-2:]
    h3 = h2p.reshape(-1, ROW_TILE, HEAD_W)

    def wspec(shape):
        return pl.BlockSpec((None, None) + shape, lambda i, be, nu: (0, be[i], 0, 0))

    grid_spec = pltpu.PrefetchScalarGridSpec(
        num_scalar_prefetch=2,
        grid=(n_blocks,),
        in_specs=[pl.BlockSpec((None, 1, tm), lambda i, be, nu: (i, 0, 0), memory_space=pltpu.SMEM),
                  pl.BlockSpec((None, 1, tm), lambda i, be, nu: (jnp.minimum(i + 1, n_blocks - 1), 0, 0),
                               memory_space=pltpu.SMEM),
                  pl.BlockSpec(memory_space=pl.ANY),
                  wspec((d, de)), wspec((d, de)), wspec((de, d))],
        out_specs=pl.BlockSpec((tm * ROW_TILE, HEAD_W), lambda i, be, nu: (i, 0)),
        scratch_shapes=[pltpu.VMEM((2, tm * ROW_TILE, HEAD_W), U32),
                        pltpu.VMEM((d, de), BF16), pltpu.VMEM((d, de), BF16), pltpu.VMEM((de, d), BF16),
                        pltpu.SemaphoreType.DMA((2,))],
    )
    return pl.pallas_call(
        functools.partial(_expert_kernel, tm=tm),
        grid_spec=grid_spec,
        out_shape=jax.ShapeDtypeStruct((n_blocks * tm * ROW_TILE, HEAD_W), U32),
        compiler_params=_cparams(("arbitrary",), disable_bounds_checks=True),
    )(block_e, n_used, idx3, idx3, h3, w_gate, w_up, w_down)


def _combine_kernel(d_ref, dn_ref, ys_hbm, gw_ref, base_ref, g2_ref, o_ref, buf, sem, *, tm):
    i = pl.program_id(0)
    n = pl.num_programs(0)
    slot = i % 2

    def row_copy(src, s, r, t):
        return pltpu.make_async_copy(ys_hbm.at[src], buf.at[s, pl.ds((r * tm + t) * ROW_TILE, ROW_TILE)],
                                     sem.at[s])

    def issue(d_r, s):
        def body(t, carry):
            for r in range(TOP_K):
                row_copy(d_r[r, t], s, r, t).start(priority=r % 2)
            return carry
        lax.fori_loop(0, tm, body, 0)

    @pl.when(i == 0)
    def _():
        issue(d_ref, 0)

    @pl.when(i + 1 < n)
    def _():
        issue(dn_ref, 1 - slot)

    def wait_body(t, carry):
        for r in range(TOP_K):
            row_copy(0, slot, r, t).wait()
        return carry
    lax.fori_loop(0, tm, wait_body, 0)

    half = ROW_TILE * HEAD_W
    acc_lo = [jnp.zeros((tm, HEAD_W), F32)] * ROW_TILE
    acc_hi = [jnp.zeros((tm, HEAD_W), F32)] * ROW_TILE
    for r in range(TOP_K):
        w = jnp.broadcast_to(gw_ref[:, r:r + 1], (tm, HEAD_W))
        for j, piece in enumerate(_load_token_major(buf.at[slot], r * tm, tm)):
            lo, hi = _unpack_bf16_pair(piece)
            acc_lo[j] = acc_lo[j] + w * lo
            acc_hi[j] = acc_hi[j] + w * hi
    for j in range(ROW_TILE):
        lo_sl = slice(j * HEAD_W, (j + 1) * HEAD_W)
        hi_sl = slice(half + j * HEAD_W, half + (j + 1) * HEAD_W)
        o_ref[:, lo_sl] = base_ref[:, lo_sl] + g2_ref[:, lo_sl] * acc_lo[j]
        o_ref[:, hi_sl] = base_ref[:, hi_sl] + g2_ref[:, hi_sl] * acc_hi[j]


def _combine(ys, dest, gate_w, base, gate2, seq):
    ntok, d = base.shape
    y3 = ys.reshape(-1, ROW_TILE, HEAD_W)
    tm = min(COMBINE_TILE, seq)
    nt = ntok // tm
    per_b = seq // tm
    d3 = dest.reshape(TOP_K, nt, tm).transpose(1, 0, 2)
    return pl.pallas_call(
        functools.partial(_combine_kernel, tm=tm),
        grid=(nt,),
        in_specs=[pl.BlockSpec((None, TOP_K, tm), lambda i: (i, 0, 0), memory_space=pltpu.SMEM),
                  pl.BlockSpec((None, TOP_K, tm), lambda i: (jnp.minimum(i + 1, nt - 1), 0, 0),
                               memory_space=pltpu.SMEM),
                  pl.BlockSpec(memory_space=pl.ANY),
                  pl.BlockSpec((tm, TOP_K), lambda i: (i, 0)),
                  pl.BlockSpec((tm, d), lambda i: (i, 0)),
                  pl.BlockSpec((None, 1, d), lambda i: (i // per_b, 0, 0))],
        out_specs=pl.BlockSpec((tm, d), lambda i: (i, 0)),
        out_shape=jax.ShapeDtypeStruct((ntok, d), F32),
        scratch_shapes=[pltpu.VMEM((2, TOP_K * tm * ROW_TILE, HEAD_W), U32), pltpu.SemaphoreType.DMA((2,))],
        compiler_params=_cparams(("arbitrary",), disable_bounds_checks=True),
    )(d3, d3, y3, gate_w.T, base, gate2[:, None, :])


def kernel(x, c, ada_w, ada_b, norm1_g, w_in, lb_logits, hg_norm_g, q_norm_g, k_norm_g, lambda_q1, lambda_k1,
           lambda_q2, lambda_k2, da_norm_g, rel_bias, w_branch_a, w_branch_b, w_out, norm2_g, router_w,
           router_bias, w_exp_gate, w_exp_up, w_exp_down, w_sh_gate, w_sh_up, w_sh_down):
    bsz, seq, d = x.shape
    ntok = bsz * seq
    l = 0
    x2 = x.reshape(ntok, d)

    mod = _ada_mod(c, ada_w[l], ada_b[l])
    shift1, scale1, gate1, shift2, scale2, gate2 = jnp.split(mod, 6, axis=-1)

    n_gate = 2 * d
    split = w_in.shape[2] - n_gate
    w_in_bf = jnp.concatenate([w_in[l][:, split:], w_in[l][:, :split]], axis=1).astype(BF16)
    proj = _in_projection(x2, scale1, shift1, norm1_g[l], w_in_bf, seq)
    col_hg = n_gate
    col_q = col_hg + 4 * HG_HEADS * HEAD_W
    col_k = col_q + DA_HEADS * HEAD_W
    col_v = col_k + DA_HEADS * HEAD_W

    o_hg = _hgrn_branch(proj, lb_logits, hg_norm_g[l], bsz, seq, col_hg)
    qn, kn = _qk_norm(proj, q_norm_g[l], k_norm_g[l], col_q, col_k)
    lam_params = jnp.stack([lambda_q1[l], lambda_k1[l], lambda_q2[l], lambda_k2[l]])
    o_da = _diff_attention(qn, kn, proj, col_v, rel_bias, lam_params, da_norm_g[l], bsz, seq)

    wr_t = router_w[l].T
    wr_hi = wr_t.astype(BF16)
    wr_lo = (wr_t - wr_hi.astype(F32)).astype(BF16)
    base, h2p, logits_t = _merge_out(
        o_hg, o_da, proj, 0, d, x2, gate1, scale2, shift2, gate2, norm2_g[l],
        w_branch_a[l].astype(BF16), w_branch_b[l].astype(BF16), w_out[l].astype(BF16), wr_hi, wr_lo,
        w_sh_gate[l].astype(BF16), w_sh_up[l].astype(BF16), w_sh_down[l].astype(BF16), seq)

    top_e, gate_w, rank, counts, rank1, excl = _route(logits_t, router_bias[l])

    tm_e = EXPERT_TILE
    n_blocks = (ntok * TOP_K) // tm_e + N_EXPERTS
    nblk = (counts[:, 0] + tm_e - 1) // tm_e
    blk_end = jnp.cumsum(nblk)
    blk_start = (blk_end - nblk).astype(I32)
    pad_start = blk_start * tm_e
    n_used = blk_end[-1:].astype(I32)
    block_e = jnp.minimum(jnp.sum(blk_end[None, :] <= jnp.arange(n_blocks, dtype=I32)[:, None], axis=1),
                          N_EXPERTS - 1).astype(I32)

    dest = _slots(pad_start, top_e, rank)
    slot_tok = _slot_tokens(rank1, excl, block_e, blk_start, tm_e)
    ys = _routed_experts(h2p, slot_tok, block_e, n_used, w_exp_gate, w_exp_up, w_exp_down, tm_e)
    out = _combine(ys, dest, gate_w, base, gate2, seq)
    return out.reshape(bsz, seq, d)
```

```python
import functools
import math

import numpy as np
import jax
import jax.numpy as jnp
from jax import lax
from jax.experimental import pallas as pl
from jax.experimental.pallas import tpu as pltpu

F32 = jnp.float32
BF16 = jnp.bfloat16
I32 = jnp.int32
U32 = jnp.uint32

HG_HEADS = 8
HG_DK = 128
HG_DV = 128
DA_HEADS = 8
DA_DH = 64
DA_DV = 128
RP_BUCKETS = 32
RP_MAX_EXACT = 16
RP_MAX_DIST = 128
N_EXPERTS = 64
N_GROUPS = 8
TOPK_GROUPS = 4
TOP_K = 8
ROUTE_SCALE = 2.5
EPS = 1e-6
LAMBDA_INIT = 0.8 - 0.6 * math.exp(-0.3 * 0)

HEAD_W = 128
VMEM_LIMIT_BYTES = 56 * 1024 * 1024
NEG_BIG = -1e30
LOG2E = math.log2(math.e)

ATTN_TILE = 512
HGRN_TILE = 256
MERGE_TILE = 256
EXPERT_TILE = 256
COMBINE_TILE = 128
ROUTE_TILE = 512

NT_DIMS = (((1,), (1,)), ((), ()))
TN_DIMS = (((0,), (0,)), ((), ()))


def _cparams(sem, **kw):
    return pltpu.CompilerParams(dimension_semantics=sem, vmem_limit_bytes=VMEM_LIMIT_BYTES, **kw)


def _silu(x):
    return x * jax.nn.sigmoid(x)


HI_MASK = np.uint32(0xFFFF0000)
BF16_HALF_ULP = np.uint32(0x8000)


def _pack_bf16_pair(a, b):
    ua = lax.bitcast_convert_type(a, U32) + BF16_HALF_ULP
    ub = lax.bitcast_convert_type(b, U32) + BF16_HALF_ULP
    return (ua >> 16) | (ub & HI_MASK)


def _unpack_bf16_pair(w):
    lo = lax.bitcast_convert_type(w << 16, F32)
    hi = lax.bitcast_convert_type(w & HI_MASK, F32)
    return lo, hi


ROW_TILE = 8


def _store_token_major(ref, x):
    rows = x.shape[0]
    for j in range(ROW_TILE):
        ref[pl.ds(j, rows, stride=ROW_TILE), :] = x[:, j * HEAD_W:(j + 1) * HEAD_W]


def _load_token_major(ref, first, rows):
    return [ref[pl.ds(first * ROW_TILE + j, rows, stride=ROW_TILE), :] for j in range(ROW_TILE)]


def _t5_bucket_starts():
    n = np.arange(0, RP_MAX_DIST + 1)
    nf = np.maximum(n, 1).astype(np.float32)
    large = RP_MAX_EXACT + (np.log(nf / np.float32(RP_MAX_EXACT)) / np.float32(math.log(RP_MAX_DIST / RP_MAX_EXACT))
                            * np.float32(RP_BUCKETS - RP_MAX_EXACT)).astype(np.int32)
    large = np.minimum(large, RP_BUCKETS - 1)
    bucket = np.where(n < RP_MAX_EXACT, n, large)
    assert np.all(np.diff(bucket) >= 0) and bucket[-1] == RP_BUCKETS - 1
    return [int(np.argmax(bucket >= j)) for j in range(RP_BUCKETS)]


BUCKET_START = _t5_bucket_starts()


def _ada_kernel(c_ref, w_ref, b_ref, o_ref):
    ca = _silu(c_ref[...]).astype(BF16)
    o_ref[...] = jnp.dot(ca, w_ref[...].astype(BF16), preferred_element_type=F32) + b_ref[...]


def _ada_mod(c, ada_w, ada_b):
    bsz, d = c.shape
    n = ada_w.shape[1]
    rows = 8
    cp = jnp.zeros((rows, d), F32).at[:bsz].set(c)
    tn = 1024
    out = pl.pallas_call(
        _ada_kernel,
        grid=(n // tn,),
        in_specs=[pl.BlockSpec((rows, d), lambda j: (0, 0)),
                  pl.BlockSpec((d, tn), lambda j: (0, j)),
                  pl.BlockSpec((1, tn), lambda j: (0, j))],
        out_specs=pl.BlockSpec((rows, tn), lambda j: (0, j)),
        out_shape=jax.ShapeDtypeStruct((rows, n), F32),
        compiler_params=_cparams(("arbitrary",)),
    )(cp, ada_w, ada_b.reshape(1, n))
    return out[:bsz]


def _inproj_kernel(x_ref, sc_ref, sh_ref, g_ref, w_ref, o_ref, h_scr):
    @pl.when(pl.program_id(1) == 0)
    def _():
        x = x_ref[...]
        ms = jnp.mean(x * x, axis=-1, keepdims=True)
        hn = x * lax.rsqrt(ms + EPS) * g_ref[...]
        h_scr[...] = (hn * (1.0 + sc_ref[...]) + sh_ref[...]).astype(BF16)

    o_ref[...] = jnp.dot(h_scr[...], w_ref[...], preferred_element_type=F32).astype(o_ref.dtype)


def _in_projection(x2, scale, shift, g, w_bf, seq):
    ntok, d = x2.shape
    n = w_bf.shape[1]
    tm = min(1024, seq)
    tn = 1024
    per_b = seq // tm
    return pl.pallas_call(
        _inproj_kernel,
        grid=(ntok // tm, n // tn),
        in_specs=[pl.BlockSpec((tm, d), lambda i, j: (i, 0)),
                  pl.BlockSpec((None, 1, d), lambda i, j: (i // per_b, 0, 0)),
                  pl.BlockSpec((None, 1, d), lambda i, j: (i // per_b, 0, 0)),
                  pl.BlockSpec((1, d), lambda i, j: (0, 0)),
                  pl.BlockSpec((d, tn), lambda i, j: (0, j))],
        out_specs=pl.BlockSpec((tm, tn), lambda i, j: (i, j)),
        out_shape=jax.ShapeDtypeStruct((ntok, n), BF16),
        scratch_shapes=[pltpu.VMEM((tm, d), BF16)],
        compiler_params=_cparams(("arbitrary", "arbitrary")),
    )(x2, scale[:, None, :], shift[:, None, :], g.reshape(1, d), w_bf)


def _hgrn_kernel(q_ref, f_ref, i_ref, g_ref, lbl_ref, ng_ref, o_ref, state_ref, *, tile):
    @pl.when(pl.program_id(1) == 0)
    def _():
        state_ref[...] = jnp.zeros_like(state_ref)

    n_levels = tile.bit_length() - 1
    row = lax.broadcasted_iota(I32, (tile, tile), 0)
    col = lax.broadcasted_iota(I32, (tile, tile), 1)
    lev = jnp.where(row >= col, 31 - lax.clz(row ^ col), -2)
    on_diag = lev == -1
    at_level = [lev == lvl for lvl in range(n_levels)]
    rowk = lax.broadcasted_iota(I32, (tile, HG_DK), 0)
    odd_at = [(rowk & (1 << lvl)) != 0 for lvl in range(n_levels)]

    for h in range(HG_HEADS):
        sl = slice(h * HEAD_W, (h + 1) * HEAD_W)
        q = _silu(q_ref[:, sl].astype(F32)) * (HG_DK ** -0.5)
        ll = lbl_ref[:, sl]
        el = jnp.exp(ll - jnp.max(ll, axis=0, keepdims=True))
        lb = el[0:1, :] / jnp.sum(el, axis=0, keepdims=True)
        fg = lb + (1.0 - lb) * jax.nn.sigmoid(f_ref[:, sl].astype(F32))
        k = 1.0 - fg
        g = jnp.log2(fg)
        v = i_ref[:, sl]

        scores = jnp.where(on_diag,
                           lax.dot_general(q.astype(BF16), k.astype(BF16), NT_DIMS, preferred_element_type=F32),
                           0.0)
        c = g
        e = g
        for lvl in range(n_levels):
            blk = 1 << lvl
            qd = (q * jnp.exp2(c)).astype(BF16)
            kd = (k * jnp.exp2(e - c)).astype(BF16)
            s_l = lax.dot_general(qd, kd, NT_DIMS, preferred_element_type=F32)
            scores = jnp.where(at_level[lvl], s_l, scores)
            odd = odd_at[lvl]
            e_prev = pltpu.roll(e, blk, axis=0)
            e_next = pltpu.roll(e, tile - blk, axis=0)
            c = c + jnp.where(odd, e_prev, 0.0)
            e = e + jnp.where(odd, e_prev, e_next)
        st = state_ref[h]
        qb = (q * jnp.exp2(c)).astype(BF16)
        o = (jnp.dot(scores.astype(BF16), v, preferred_element_type=F32)
             + lax.dot_general(qb, st.astype(BF16), NT_DIMS, preferred_element_type=F32))
        kd = (k * jnp.exp2(e - c)).astype(BF16)
        state_ref[h] = st * jnp.exp2(e[0:1, :]) + lax.dot_general(v, kd, TN_DIMS, preferred_element_type=F32)

        ms = jnp.mean(o * o, axis=-1, keepdims=True)
        gate = _silu(g_ref[:, sl].astype(F32))
        o_ref[:, sl] = (o * lax.rsqrt(ms + EPS) * ng_ref[...] * gate).astype(o_ref.dtype)


def _hgrn_branch(proj, lb_logits, norm_g, bsz, seq, col0):
    ntok = proj.shape[0]
    tile = min(HGRN_TILE, seq)
    nt = seq // tile
    width = HG_HEADS * HEAD_W
    cb = col0 // width

    def spec(off):
        return pl.BlockSpec((tile, width), lambda b, t, off=off: (b * nt + t, cb + off))

    return pl.pallas_call(
        functools.partial(_hgrn_kernel, tile=tile),
        grid=(bsz, nt),
        in_specs=[spec(0), spec(1), spec(2), spec(3),
                  pl.BlockSpec(lb_logits.shape, lambda b, t: (0, 0)),
                  pl.BlockSpec((1, HG_DV), lambda b, t: (0, 0))],
        out_specs=pl.BlockSpec((tile, width), lambda b, t: (b * nt + t, 0)),
        out_shape=jax.ShapeDtypeStruct((ntok, width), BF16),
        scratch_shapes=[pltpu.VMEM((HG_HEADS, HG_DV, HG_DK), F32)],
        compiler_params=_cparams(("arbitrary", "arbitrary")),
    )(proj, proj, proj, proj, lb_logits, norm_g.reshape(1, HG_DV))


def _qknorm_kernel(q_ref, k_ref, qg_ref, kg_ref, qo_ref, ko_ref):
    lane = lax.broadcasted_iota(I32, (q_ref.shape[0], HEAD_W), 1)
    lo = lane < DA_DH

    def norm(x_ref, g_ref, o_ref, scale):
        for h in range(DA_HEADS):
            sl = slice(h * HEAD_W, (h + 1) * HEAD_W)
            x = x_ref[:, sl].astype(F32)
            xx = x * x
            s0 = jnp.sum(jnp.where(lo, xx, 0.0), axis=-1, keepdims=True)
            s1 = jnp.sum(jnp.where(lo, 0.0, xx), axis=-1, keepdims=True)
            ms = jnp.where(lo, s0, s1) * (1.0 / DA_DH)
            o_ref[:, sl] = (x * lax.rsqrt(ms + EPS) * g_ref[...] * scale).astype(o_ref.dtype)

    norm(q_ref, qg_ref, qo_ref, DA_DH ** -0.5 * LOG2E)
    norm(k_ref, kg_ref, ko_ref, 1.0)


def _qk_norm(proj, q_g, k_g, col_q, col_k):
    ntok = proj.shape[0]
    width = DA_HEADS * HEAD_W
    tm = min(512, ntok)
    qg = jnp.tile(q_g, 2).reshape(1, HEAD_W)
    kg = jnp.tile(k_g, 2).reshape(1, HEAD_W)
    return pl.pallas_call(
        _qknorm_kernel,
        grid=(ntok // tm,),
        in_specs=[pl.BlockSpec((tm, width), lambda i: (i, col_q // width)),
                  pl.BlockSpec((tm, width), lambda i: (i, col_k // width)),
                  pl.BlockSpec((1, HEAD_W), lambda i: (0, 0)),
                  pl.BlockSpec((1, HEAD_W), lambda i: (0, 0))],
        out_specs=[pl.BlockSpec((tm, width), lambda i: (i, 0)),
                   pl.BlockSpec((tm, width), lambda i: (i, 0))],
        out_shape=[jax.ShapeDtypeStruct((ntok, width), BF16)] * 2,
        compiler_params=_cparams(("arbitrary",)),
    )(proj, proj, qg, kg)


def _attn_kernel(qi_ref, ki_ref, q_ref, k_ref, v_ref, rel_ref, lamp_ref, ng_ref, o_ref,
                 m_scr, acc_scr, bias_scr, *, tile):
    b = pl.program_id(0)
    p = pl.program_id(1)
    qi = qi_ref[p]
    ki = ki_ref[p]
    diff = qi - ki

    row = lax.broadcasted_iota(I32, (tile, tile), 0)
    col = lax.broadcasted_iota(I32, (tile, tile), 1)

    @pl.when((b == 0) & (p == 0))
    def _():
        def per_head(h, carry):
            for d in range(2):
                dist = row - col + d * tile
                bias = jnp.full((tile, tile), rel_ref[0, h], F32)
                for j in range(1, RP_BUCKETS):
                    bias = jnp.where(dist >= BUCKET_START[j], rel_ref[j, h], bias)
                bias_scr[d, h] = (bias - rel_ref[RP_BUCKETS - 1, h]) * LOG2E
            return carry
        lax.fori_loop(0, DA_HEADS, per_head, 0)

    @pl.when(ki == 0)
    def _():
        m_scr[...] = jnp.full_like(m_scr, NEG_BIG)
        acc_scr[...] = jnp.zeros_like(acc_scr)

    lane = lax.broadcasted_iota(I32, (tile, HEAD_W), 1)
    lo = lane < DA_DH
    ones = jnp.ones((tile, HEAD_W), BF16)
    reps = tile // HEAD_W

    def step(kind):
        for h in range(DA_HEADS):
            sl = slice(h * HEAD_W, (h + 1) * HEAD_W)
            q = q_ref[:, sl]
            k = k_ref[:, sl]
            v_aug = jnp.concatenate([v_ref[:, sl], ones], axis=1)
            for c in range(2):
                qc = jnp.where(lo, q, jnp.zeros_like(q)) if c == 0 else jnp.where(lo, jnp.zeros_like(q), q)
                s = lax.dot_general(qc, k, NT_DIMS, preferred_element_type=F32)
                if kind != 2:
                    s = s + bias_scr[kind, h]
                if kind == 0:
                    s = jnp.where(row >= col, s, NEG_BIG)
                idx = 2 * h + c
                m_old = m_scr[idx]
                m_cur = jnp.broadcast_to(jnp.max(s, axis=-1, keepdims=True), (tile, HEAD_W))
                m_new = jnp.maximum(m_old, m_cur)
                alpha = jnp.exp2(m_old - m_new)
                pr = jnp.exp2(s - jnp.concatenate([m_new] * reps, axis=1))
                pv = jnp.dot(pr.astype(BF16), v_aug, preferred_element_type=F32)
                acc_scr[idx] = jnp.concatenate([alpha, alpha], axis=1) * acc_scr[idx] + pv
                m_scr[idx] = m_new

    for kind, cond in ((0, diff == 0), (1, diff == 1), (2, diff > 1)):
        pl.when(cond)(functools.partial(step, kind))

    @pl.when(diff == 0)
    def _():
        lp = lamp_ref[...]
        lam = (jnp.exp(jnp.sum(lp[0:1] * lp[1:2], axis=-1, keepdims=True))
               - jnp.exp(jnp.sum(lp[2:3] * lp[3:4], axis=-1, keepdims=True)) + LAMBDA_INIT)
        for h in range(DA_HEADS):
            sl = slice(h * HEAD_W, (h + 1) * HEAD_W)
            a0 = acc_scr[2 * h]
            a1 = acc_scr[2 * h + 1]
            o = a0[:, :DA_DV] / a0[:, DA_DV:] - lam * (a1[:, :DA_DV] / a1[:, DA_DV:])
            ms = jnp.mean(o * o, axis=-1, keepdims=True)
            o_ref[:, sl] = (o * lax.rsqrt(ms + EPS) * ng_ref[...] * (1.0 - LAMBDA_INIT)).astype(o_ref.dtype)


def _diff_attention(qn, kn, proj, col_v, rel_bias, lam_params, norm_g, bsz, seq):
    ntok = qn.shape[0]
    tile = min(ATTN_TILE, seq)
    nq = seq // tile
    width = DA_HEADS * HEAD_W
    qi_list, ki_list = [], []
    for a in range(nq):
        for c in range(a + 1):
            qi_list.append(a)
            ki_list.append(c)
    qi_arr = jnp.asarray(qi_list, I32)
    ki_arr = jnp.asarray(ki_list, I32)
    cv = col_v // width
    grid_spec = pltpu.PrefetchScalarGridSpec(
        num_scalar_prefetch=2,
        grid=(bsz, len(qi_list)),
        in_specs=[pl.BlockSpec((tile, width), lambda b, p, qi, ki: (b * nq + qi[p], 0)),
                  pl.BlockSpec((tile, width), lambda b, p, qi, ki: (b * nq + ki[p], 0)),
                  pl.BlockSpec((tile, width), lambda b, p, qi, ki: (b * nq + ki[p], cv)),
                  pl.BlockSpec(memory_space=pltpu.SMEM),
                  pl.BlockSpec((4, DA_DH), lambda b, p, qi, ki: (0, 0)),
                  pl.BlockSpec((1, DA_DV), lambda b, p, qi, ki: (0, 0))],
        out_specs=pl.BlockSpec((tile, width), lambda b, p, qi, ki: (b * nq + qi[p], 0)),
        scratch_shapes=[pltpu.VMEM((2 * DA_HEADS, tile, HEAD_W), F32),
                        pltpu.VMEM((2 * DA_HEADS, tile, 2 * DA_DV), F32),
                        pltpu.VMEM((2, DA_HEADS, tile, tile), F32)],
    )
    return pl.pallas_call(
        functools.partial(_attn_kernel, tile=tile),
        grid_spec=grid_spec,
        out_shape=jax.ShapeDtypeStruct((ntok, width), BF16),
        compiler_params=_cparams(("arbitrary", "arbitrary")),
    )(qi_arr, ki_arr, qn, kn, proj, rel_bias, lam_params, norm_g.reshape(1, DA_DV))


def _merge_kernel(oa_ref, ob_ref, ga_ref, gb_ref, x_ref, g1_ref, sc_ref, sh_ref, g2_ref, n2_ref,
                  wa_ref, wb_ref, wo_ref, wrh_ref, wrl_ref, sg_ref, su_ref, sd_ref,
                  base_ref, hp_ref, lg_ref):
    ya = jnp.dot(oa_ref[...], wa_ref[...], preferred_element_type=F32)
    yb = jnp.dot(ob_ref[...], wb_ref[...], preferred_element_type=F32)
    merged = (jax.nn.sigmoid(ga_ref[...].astype(F32)) * ya
              + jax.nn.sigmoid(gb_ref[...].astype(F32)) * yb).astype(BF16)
    y = jnp.dot(merged, wo_ref[...], preferred_element_type=F32)
    x1 = x_ref[...] + g1_ref[...] * y
    ms = jnp.mean(x1 * x1, axis=-1, keepdims=True)
    h2 = (x1 * lax.rsqrt(ms + EPS) * n2_ref[...]) * (1.0 + sc_ref[...]) + sh_ref[...]
    half = h2.shape[1] // 2
    _store_token_major(hp_ref, _pack_bf16_pair(h2[:, :half], h2[:, half:]))
    hh = h2.astype(BF16)
    a = (_silu(jnp.dot(hh, sg_ref[...], preferred_element_type=F32))
         * jnp.dot(hh, su_ref[...], preferred_element_type=F32)).astype(BF16)
    base_ref[...] = x1 + g2_ref[...] * jnp.dot(a, sd_ref[...], preferred_element_type=F32)
    hl = (h2 - hh.astype(F32)).astype(BF16)
    lg_ref[...] = (lax.dot_general(wrh_ref[...], hh, NT_DIMS, preferred_element_type=F32)
                   + lax.dot_general(wrh_ref[...], hl, NT_DIMS, preferred_element_type=F32)
                   + lax.dot_general(wrl_ref[...], hh, NT_DIMS, preferred_element_type=F32))


def _merge_out(o_hg, o_da, proj, col_ga, col_gb, x2, gate1, scale2, shift2, gate2, norm2_g,
               wa, wb, wo, wr_hi, wr_lo, sg, su, sd, seq):
    ntok, d = x2.shape
    tm = min(MERGE_TILE, seq)
    per_b = seq // tm
    wa_w = o_hg.shape[1]
    wb_w = o_da.shape[1]
    de = sg.shape[1]

    def const(shape):
        return pl.BlockSpec(shape, lambda i: (0,) * len(shape), pipeline_mode=pl.Buffered(1))

    def perb():
        return pl.BlockSpec((None, 1, d), lambda i: (i // per_b, 0, 0))

    return pl.pallas_call(
        _merge_kernel,
        grid=(ntok // tm,),
        in_specs=[pl.BlockSpec((tm, wa_w), lambda i: (i, 0)),
                  pl.BlockSpec((tm, wb_w), lambda i: (i, 0)),
                  pl.BlockSpec((tm, d), lambda i: (i, col_ga // d)),
                  pl.BlockSpec((tm, d), lambda i: (i, col_gb // d)),
                  pl.BlockSpec((tm, d), lambda i: (i, 0)),
                  perb(), perb(), perb(), perb(),
                  const((1, d)),
                  const((wa_w, d)), const((wb_w, d)), const((d, d)),
                  const((N_EXPERTS, d)), const((N_EXPERTS, d)),
                  const((d, de)), const((d, de)), const((de, d))],
        out_specs=[pl.BlockSpec((tm, d), lambda i: (i, 0)),
                   pl.BlockSpec((tm * ROW_TILE, HEAD_W), lambda i: (i, 0)),
                   pl.BlockSpec((N_EXPERTS, tm), lambda i: (0, i))],
        out_shape=[jax.ShapeDtypeStruct((ntok, d), F32),
                   jax.ShapeDtypeStruct((ntok * ROW_TILE, HEAD_W), U32),
                   jax.ShapeDtypeStruct((N_EXPERTS, ntok), F32)],
        compiler_params=_cparams(("arbitrary",)),
    )(o_hg, o_da, proj, proj, x2, gate1[:, None, :], scale2[:, None, :], shift2[:, None, :],
      gate2[:, None, :], norm2_g.reshape(1, d), wa, wb, wo, wr_hi, wr_lo, sg, su, sd)


def _route_kernel(lg_ref, rb_ref, te_ref, gw_ref, rk_ref, cnt_ref, r1_ref, ex_ref, carry_scr, *, tt):
    @pl.when(pl.program_id(0) == 0)
    def _():
        carry_scr[...] = jnp.zeros_like(carry_scr)

    per_g = N_EXPERTS // N_GROUPS
    scores = jax.nn.sigmoid(lg_ref[...])
    sel = scores + rb_ref[...]
    sel3 = sel.reshape(N_GROUPS, per_g, tt)
    j_io = lax.broadcasted_iota(I32, (N_GROUPS, per_g, tt), 1)
    m1 = jnp.max(sel3, axis=1, keepdims=True)
    i1 = jnp.min(jnp.where(sel3 == m1, j_io, per_g), axis=1, keepdims=True)
    m2 = jnp.max(jnp.where(j_io == i1, -jnp.inf, sel3), axis=1, keepdims=True)
    gs = (m1 + m2).reshape(N_GROUPS, tt)
    g_io = lax.broadcasted_iota(I32, (N_GROUPS, tt), 0)
    gmask = jnp.zeros((N_GROUPS, tt), jnp.bool_)
    for _ in range(TOPK_GROUPS):
        gm = jnp.max(gs, axis=0, keepdims=True)
        gi = jnp.min(jnp.where(gs == gm, g_io, N_GROUPS), axis=0, keepdims=True)
        hit = g_io == gi
        gmask = gmask | hit
        gs = jnp.where(hit, -jnp.inf, gs)
    emask = jnp.broadcast_to(gmask.reshape(N_GROUPS, 1, tt), (N_GROUPS, per_g, tt)).reshape(N_EXPERTS, tt)
    cand = jnp.where(emask, sel, -jnp.inf)
    e_io = lax.broadcasted_iota(I32, (N_EXPERTS, tt), 0)
    chosen = jnp.zeros((N_EXPERTS, tt), jnp.bool_)
    picks = []
    for _ in range(TOP_K):
        em = jnp.max(cand, axis=0, keepdims=True)
        ei = jnp.min(jnp.where(cand == em, e_io, N_EXPERTS), axis=0, keepdims=True)
        hit = e_io == ei
        chosen = chosen | hit
        cand = jnp.where(hit, -jnp.inf, cand)
        picks.append((ei, hit))
    gsel = jnp.where(chosen, scores, 0.0)
    wnorm = gsel / jnp.sum(gsel, axis=0, keepdims=True) * ROUTE_SCALE
    ch = jnp.where(chosen, 1.0, 0.0)
    tri = (lax.broadcasted_iota(I32, (tt, tt), 0) <= lax.broadcasted_iota(I32, (tt, tt), 1))
    incl = jnp.dot(ch.astype(BF16), jnp.where(tri, 1.0, 0.0).astype(BF16), preferred_element_type=F32)
    carry = carry_scr[...]
    excl = incl - ch + carry
    carry_new = carry + incl[:, tt - 1:tt]
    carry_scr[...] = carry_new
    r1_ref[...] = jnp.where(chosen, excl + 1.0, 0.0).astype(I32)
    ex_ref[...] = excl.astype(I32)
    for r, (ei, hit) in enumerate(picks):
        te_ref[r:r + 1, :] = ei
        gw_ref[r:r + 1, :] = jnp.sum(jnp.where(hit, wnorm, 0.0), axis=0, keepdims=True)
        rk_ref[r:r + 1, :] = jnp.sum(jnp.where(hit, excl, 0.0), axis=0, keepdims=True).astype(I32)
    cnt_ref[...] = jnp.broadcast_to(carry_new, cnt_ref.shape).astype(I32)


def _route(logits_t, router_bias):
    ntok = logits_t.shape[1]
    tt = min(ROUTE_TILE, ntok)
    return pl.pallas_call(
        functools.partial(_route_kernel, tt=tt),
        grid=(ntok // tt,),
        in_specs=[pl.BlockSpec((N_EXPERTS, tt), lambda i: (0, i)),
                  pl.BlockSpec((N_EXPERTS, 1), lambda i: (0, 0))],
        out_specs=[pl.BlockSpec((TOP_K, tt), lambda i: (0, i)),
                   pl.BlockSpec((TOP_K, tt), lambda i: (0, i)),
                   pl.BlockSpec((TOP_K, tt), lambda i: (0, i)),
                   pl.BlockSpec((N_EXPERTS, 128), lambda i: (0, 0)),
                   pl.BlockSpec((N_EXPERTS, tt), lambda i: (0, i)),
                   pl.BlockSpec((N_EXPERTS, tt), lambda i: (0, i))],
        out_shape=[jax.ShapeDtypeStruct((TOP_K, ntok), I32),
                   jax.ShapeDtypeStruct((TOP_K, ntok), F32),
                   jax.ShapeDtypeStruct((TOP_K, ntok), I32),
                   jax.ShapeDtypeStruct((N_EXPERTS, 128), I32),
                   jax.ShapeDtypeStruct((N_EXPERTS, ntok), I32),
                   jax.ShapeDtypeStruct((N_EXPERTS, ntok), I32)],
        scratch_shapes=[pltpu.VMEM((N_EXPERTS, 1), F32)],
        compiler_params=_cparams(("arbitrary",)),
    )(logits_t, router_bias.reshape(N_EXPERTS, 1))


def _slots_kernel(ps_ref, te_ref, rk_ref, d_ref):
    te = te_ref[...]
    dest = rk_ref[...]
    for e in range(N_EXPERTS):
        dest = dest + jnp.where(te == e, ps_ref[e], 0)
    d_ref[...] = dest


def _slots(pad_start, top_e, rank):
    ntok = top_e.shape[1]
    tt = min(2048, ntok)
    return pl.pallas_call(
        _slots_kernel,
        grid=(ntok // tt,),
        in_specs=[pl.BlockSpec(memory_space=pltpu.SMEM),
                  pl.BlockSpec((TOP_K, tt), lambda i: (0, i)),
                  pl.BlockSpec((TOP_K, tt), lambda i: (0, i))],
        out_specs=pl.BlockSpec((TOP_K, tt), lambda i: (0, i)),
        out_shape=jax.ShapeDtypeStruct((TOP_K, ntok), I32),
        compiler_params=_cparams(("arbitrary",)),
    )(pad_start, top_e, rank)


BLOCKS_PER_STEP = 8


def _slot_tokens_kernel(be_ref, bs_ref, r_ref, base_ref, o_ref, *, tm):
    i = pl.program_id(0)
    n_ch = r_ref.shape[1]
    ch_io = lax.broadcasted_iota(I32, (n_ch, tm), 0)
    l_io = lax.broadcasted_iota(I32, (HEAD_W, tm), 0)
    j_io = lax.broadcasted_iota(I32, (1, tm), 1)
    for b in range(BLOCKS_PER_STEP):
        blk = i * BLOCKS_PER_STEP + b
        e = be_ref[blk]
        g = (blk - bs_ref[e]) * tm + j_io
        ch = jnp.sum((base_ref[e] <= g).astype(I32), axis=0, keepdims=True) - 1
        pick = jnp.where(ch_io == ch, 1.0, 0.0).astype(BF16)
        r = r_ref[e]
        hi = (r >> 7).astype(F32).astype(BF16)
        lo = (r & 127).astype(F32).astype(BF16)
        ranks = (lax.dot_general(hi, pick, TN_DIMS, preferred_element_type=F32) * 128.0
                 + lax.dot_general(lo, pick, TN_DIMS, preferred_element_type=F32))
        hit = ranks == (g + 1).astype(F32)
        lane = jnp.sum(jnp.where(hit, l_io, 0), axis=0, keepdims=True)
        o_ref[b] = ch * HEAD_W + lane


def _slot_tokens(r1, excl, block_e, blk_start, tm):
    ntok = r1.shape[1]
    n_ch = ntok // HEAD_W
    n_blocks = block_e.shape[0]
    r3 = r1.reshape(N_EXPERTS, n_ch, HEAD_W)
    base = excl[:, ::HEAD_W].reshape(N_EXPERTS, n_ch, 1)
    grid_spec = pltpu.PrefetchScalarGridSpec(
        num_scalar_prefetch=2,
        grid=(n_blocks // BLOCKS_PER_STEP,),
        in_specs=[pl.BlockSpec((N_EXPERTS, n_ch, HEAD_W), lambda i, be, bs: (0, 0, 0)),
                  pl.BlockSpec((N_EXPERTS, n_ch, 1), lambda i, be, bs: (0, 0, 0))],
        out_specs=pl.BlockSpec((BLOCKS_PER_STEP, 1, tm), lambda i, be, bs: (i, 0, 0)),
    )
    return pl.pallas_call(
        functools.partial(_slot_tokens_kernel, tm=tm),
        grid_spec=grid_spec,
        out_shape=jax.ShapeDtypeStruct((n_blocks, 1, tm), I32),
        compiler_params=_cparams(("arbitrary",)),
    )(block_e, blk_start, r3, base)


GATHER_UNROLL = 8


def _expert_kernel(be_ref, nu_ref, we_ref, idx_ref, idxn_ref, h_hbm, wg_ref, wu_ref, wd_ref, o_ref,
                   xs, wgb, wub, wdb, sem, *, tm):
    del we_ref
    i = pl.program_id(0)
    n_used = nu_ref[0]
    slot = i % 2

    def row_copy(tok, s, r):
        return pltpu.make_async_copy(h_hbm.at[tok], xs.at[s, pl.ds(r * ROW_TILE, ROW_TILE)], sem.at[s])

    def issue(idx_r, s):
        def body(g, carry):
            for u in range(GATHER_UNROLL):
                r = g * GATHER_UNROLL + u
                row_copy(idx_r[0, r], s, r).start(priority=u % 2)
            return carry
        lax.fori_loop(0, tm // GATHER_UNROLL, body, 0)

    @pl.when(i == 0)
    def _():
        issue(idx_ref, 0)

    @pl.when(i + 1 < n_used)
    def _():
        issue(idxn_ref, 1 - slot)

    @pl.when(i < n_used)
    def _():
        e = be_ref[i]
        e_prev = be_ref[jnp.maximum(i - 1, 0)]

        @pl.when((i == 0) | (e != e_prev))
        def _():
            wgb[...] = wg_ref[...].astype(BF16)
            wub[...] = wu_ref[...].astype(BF16)
            wdb[...] = wd_ref[...].astype(BF16)

        def wait_body(g, carry):
            for u in range(GATHER_UNROLL):
                row_copy(0, slot, g * GATHER_UNROLL + u).wait()
            return carry
        lax.fori_loop(0, tm // GATHER_UNROLL, wait_body, 0)

        xlo, xhi = _unpack_bf16_pair(jnp.concatenate(_load_token_major(xs.at[slot], 0, tm), axis=1))
        xlo = xlo.astype(BF16)
        xhi = xhi.astype(BF16)
        half = xlo.shape[1]

        def proj_in(w):
            return (jnp.dot(xlo, w[:half, :], preferred_element_type=F32)
                    + jnp.dot(xhi, w[half:, :], preferred_element_type=F32))

        a = (_silu(proj_in(wgb)) * proj_in(wub)).astype(BF16)
        y = jnp.dot(a, wdb[...], preferred_element_type=F32)
        _store_token_major(o_ref, _pack_bf16_pair(y[:, :half], y[:, half:]))

    @pl.when(i >= n_used)
    def _():
        o_ref[...] = jnp.zeros_like(o_ref)


def _routed_experts(h2p, idx3, block_e, n_used, w_expert, w_gate, w_up, w_down, tm):
    n_blocks = block_e.shape[0]
    d, de = w_gate.shape[-2:]
    h3 = h2p.reshape(-1, ROW_TILE, HEAD_W)

    def wspec(shape):
        return pl.BlockSpec((None, None) + shape, lambda i, be, nu, we: (0, we[i], 0, 0))

    grid_spec = pltpu.PrefetchScalarGridSpec(
        num_scalar_prefetch=3,
        grid=(n_blocks,),
        in_specs=[pl.BlockSpec((None, 1, tm), lambda i, be, nu, we: (i, 0, 0), memory_space=pltpu.SMEM),
                  pl.BlockSpec((None, 1, tm), lambda i, be, nu, we: (jnp.minimum(i + 1, n_blocks - 1), 0, 0),
                               memory_space=pltpu.SMEM),
                  pl.BlockSpec(memory_space=pl.ANY),
                  wspec((d, de)), wspec((d, de)), wspec((de, d))],
        out_specs=pl.BlockSpec((tm * ROW_TILE, HEAD_W), lambda i, be, nu, we: (i, 0)),
        scratch_shapes=[pltpu.VMEM((2, tm * ROW_TILE, HEAD_W), U32),
                        pltpu.VMEM((d, de), BF16), pltpu.VMEM((d, de), BF16), pltpu.VMEM((de, d), BF16),
                        pltpu.SemaphoreType.DMA((2,))],
    )
    return pl.pallas_call(
        functools.partial(_expert_kernel, tm=tm),
        grid_spec=grid_spec,
        out_shape=jax.ShapeDtypeStruct((n_blocks * tm * ROW_TILE, HEAD_W), U32),
        compiler_params=_cparams(("arbitrary",), disable_bounds_checks=True),
    )(block_e, n_used, w_expert, idx3, idx3, h3, w_gate, w_up, w_down)


def _combine_kernel(d_ref, dn_ref, ys_hbm, gw_ref, base_ref, g2_ref, o_ref, buf, sem, *, tm):
    i = pl.program_id(0)
    n = pl.num_programs(0)
    slot = i % 2

    def row_copy(src, s, r, t):
        return pltpu.make_async_copy(ys_hbm.at[src], buf.at[s, pl.ds((r * tm + t) * ROW_TILE, ROW_TILE)],
                                     sem.at[s])

    def issue(d_r, s):
        def body(t, carry):
            for r in range(TOP_K):
                row_copy(d_r[r, t], s, r, t).start(priority=r % 2)
            return carry
        lax.fori_loop(0, tm, body, 0)

    @pl.when(i == 0)
    def _():
        issue(d_ref, 0)

    @pl.when(i + 1 < n)
    def _():
        issue(dn_ref, 1 - slot)

    def wait_body(t, carry):
        for r in range(TOP_K):
            row_copy(0, slot, r, t).wait()
        return carry
    lax.fori_loop(0, tm, wait_body, 0)

    half = ROW_TILE * HEAD_W
    acc_lo = [jnp.zeros((tm, HEAD_W), F32)] * ROW_TILE
    acc_hi = [jnp.zeros((tm, HEAD_W), F32)] * ROW_TILE
    for r in range(TOP_K):
        w = jnp.broadcast_to(gw_ref[:, r:r + 1], (tm, HEAD_W))
        for j, piece in enumerate(_load_token_major(buf.at[slot], r * tm, tm)):
            lo, hi = _unpack_bf16_pair(piece)
            acc_lo[j] = acc_lo[j] + w * lo
            acc_hi[j] = acc_hi[j] + w * hi
    for j in range(ROW_TILE):
        lo_sl = slice(j * HEAD_W, (j + 1) * HEAD_W)
        hi_sl = slice(half + j * HEAD_W, half + (j + 1) * HEAD_W)
        o_ref[:, lo_sl] = base_ref[:, lo_sl] + g2_ref[:, lo_sl] * acc_lo[j]
        o_ref[:, hi_sl] = base_ref[:, hi_sl] + g2_ref[:, hi_sl] * acc_hi[j]


def _combine(ys, dest, gate_w, base, gate2, seq):
    ntok, d = base.shape
    y3 = ys.reshape(-1, ROW_TILE, HEAD_W)
    tm = min(COMBINE_TILE, seq)
    nt = ntok // tm
    per_b = seq // tm
    d3 = dest.reshape(TOP_K, nt, tm).transpose(1, 0, 2)
    return pl.pallas_call(
        functools.partial(_combine_kernel, tm=tm),
        grid=(nt,),
        in_specs=[pl.BlockSpec((None, TOP_K, tm), lambda i: (i, 0, 0), memory_space=pltpu.SMEM),
                  pl.BlockSpec((None, TOP_K, tm), lambda i: (jnp.minimum(i + 1, nt - 1), 0, 0),
                               memory_space=pltpu.SMEM),
                  pl.BlockSpec(memory_space=pl.ANY),
                  pl.BlockSpec((tm, TOP_K), lambda i: (i, 0)),
                  pl.BlockSpec((tm, d), lambda i: (i, 0)),
                  pl.BlockSpec((None, 1, d), lambda i: (i // per_b, 0, 0))],
        out_specs=pl.BlockSpec((tm, d), lambda i: (i, 0)),
        out_shape=jax.ShapeDtypeStruct((ntok, d), F32),
        scratch_shapes=[pltpu.VMEM((2, TOP_K * tm * ROW_TILE, HEAD_W), U32), pltpu.SemaphoreType.DMA((2,))],
        compiler_params=_cparams(("arbitrary",), disable_bounds_checks=True),
    )(d3, d3, y3, gate_w.T, base, gate2[:, None, :])


def kernel(x, c, ada_w, ada_b, norm1_g, w_in, lb_logits, hg_norm_g, q_norm_g, k_norm_g, lambda_q1, lambda_k1,
           lambda_q2, lambda_k2, da_norm_g, rel_bias, w_branch_a, w_branch_b, w_out, norm2_g, router_w,
           router_bias, w_exp_gate, w_exp_up, w_exp_down, w_sh_gate, w_sh_up, w_sh_down):
    bsz, seq, d = x.shape
    ntok = bsz * seq
    l = 0
    x2 = x.reshape(ntok, d)

    mod = _ada_mod(c, ada_w[l], ada_b[l])
    shift1, scale1, gate1, shift2, scale2, gate2 = jnp.split(mod, 6, axis=-1)

    n_gate = 2 * d
    split = w_in.shape[2] - n_gate
    w_in_bf = jnp.concatenate([w_in[l][:, split:], w_in[l][:, :split]], axis=1).astype(BF16)
    proj = _in_projection(x2, scale1, shift1, norm1_g[l], w_in_bf, seq)
    col_hg = n_gate
    col_q = col_hg + 4 * HG_HEADS * HEAD_W
    col_k = col_q + DA_HEADS * HEAD_W
    col_v = col_k + DA_HEADS * HEAD_W

    o_hg = _hgrn_branch(proj, lb_logits, hg_norm_g[l], bsz, seq, col_hg)
    qn, kn = _qk_norm(proj, q_norm_g[l], k_norm_g[l], col_q, col_k)
    lam_params = jnp.stack([lambda_q1[l], lambda_k1[l], lambda_q2[l], lambda_k2[l]])
    o_da = _diff_attention(qn, kn, proj, col_v, rel_bias, lam_params, da_norm_g[l], bsz, seq)

    wr_t = router_w[l].T
    wr_hi = wr_t.astype(BF16)
    wr_lo = (wr_t - wr_hi.astype(F32)).astype(BF16)
    base, h2p, logits_t = _merge_out(
        o_hg, o_da, proj, 0, d, x2, gate1, scale2, shift2, gate2, norm2_g[l],
        w_branch_a[l].astype(BF16), w_branch_b[l].astype(BF16), w_out[l].astype(BF16), wr_hi, wr_lo,
        w_sh_gate[l].astype(BF16), w_sh_up[l].astype(BF16), w_sh_down[l].astype(BF16), seq)

    top_e, gate_w, rank, counts, rank1, excl = _route(logits_t, router_bias[l])

    tm_e = EXPERT_TILE
    n_blocks = (ntok * TOP_K) // tm_e + N_EXPERTS
    nblk = (counts[:, 0] + tm_e - 1) // tm_e
    blk_end = jnp.cumsum(nblk)
    blk_start = (blk_end - nblk).astype(I32)
    pad_start = blk_start * tm_e
    n_used = blk_end[-1:].astype(I32)
    block_e = jnp.minimum(jnp.sum(blk_end[None, :] <= jnp.arange(n_blocks, dtype=I32)[:, None], axis=1),
                          N_EXPERTS - 1).astype(I32)

    dest = _slots(pad_start, top_e, rank)
    slot_tok = _slot_tokens(rank1, excl, block_e, blk_start, tm_e)
    e_ids = jnp.arange(N_EXPERTS, dtype=I32)
    nonempty_at_or_after = lax.cummin(jnp.where(nblk > 0, e_ids, N_EXPERTS)[::-1])[::-1]
    next_e = jnp.concatenate([nonempty_at_or_after[1:], jnp.full((1,), N_EXPERTS, I32)])
    next_e = jnp.where(next_e < N_EXPERTS, next_e, e_ids).astype(I32)
    is_first = jnp.arange(n_blocks, dtype=I32) == blk_start[block_e]
    w_expert = jnp.where(is_first, block_e, next_e[block_e]).astype(I32)
    ys = _routed_experts(h2p, slot_tok, block_e, n_used, w_expert, w_exp_gate, w_exp_up, w_exp_down, tm_e)
    out = _combine(ys, dest, gate_w, base, gate2, seq)
    return out.reshape(bsz, seq, d)
```

```python
import functools
import math

import numpy as np
import jax
import jax.numpy as jnp
from jax import lax
from jax.experimental import pallas as pl
from jax.experimental.pallas import tpu as pltpu

F32 = jnp.float32
BF16 = jnp.bfloat16
I32 = jnp.int32
U32 = jnp.uint32

HG_HEADS = 8
HG_DK = 128
HG_DV = 128
DA_HEADS = 8
DA_DH = 64
DA_DV = 128
RP_BUCKETS = 32
RP_MAX_EXACT = 16
RP_MAX_DIST = 128
N_EXPERTS = 64
N_GROUPS = 8
TOPK_GROUPS = 4
TOP_K = 8
ROUTE_SCALE = 2.5
EPS = 1e-6
LAMBDA_INIT = 0.8 - 0.6 * math.exp(-0.3 * 0)

HEAD_W = 128
VMEM_LIMIT_BYTES = 56 * 1024 * 1024
NEG_BIG = -1e30
LOG2E = math.log2(math.e)

ATTN_TILE = 512
HGRN_TILE = 256
MERGE_TILE = 256
EXPERT_TILE = 256
COMBINE_TILE = 128
ROUTE_TILE = 512

NT_DIMS = (((1,), (1,)), ((), ()))
TN_DIMS = (((0,), (0,)), ((), ()))


def _cparams(sem, **kw):
    return pltpu.CompilerParams(dimension_semantics=sem, vmem_limit_bytes=VMEM_LIMIT_BYTES, **kw)


def _silu(x):
    return x * jax.nn.sigmoid(x)


HI_MASK = np.uint32(0xFFFF0000)
BF16_HALF_ULP = np.uint32(0x8000)


def _pack_bf16_pair(a, b):
    ua = lax.bitcast_convert_type(a, U32) + BF16_HALF_ULP
    ub = lax.bitcast_convert_type(b, U32) + BF16_HALF_ULP
    return (ua >> 16) | (ub & HI_MASK)


def _unpack_bf16_pair(w):
    lo = lax.bitcast_convert_type(w << 16, F32)
    hi = lax.bitcast_convert_type(w & HI_MASK, F32)
    return lo, hi


ROW_TILE = 8


def _store_token_major(ref, x):
    rows = x.shape[0]
    for j in range(ROW_TILE):
        ref[pl.ds(j, rows, stride=ROW_TILE), :] = x[:, j * HEAD_W:(j + 1) * HEAD_W]


def _load_token_major(ref, first, rows):
    return [ref[pl.ds(first * ROW_TILE + j, rows, stride=ROW_TILE), :] for j in range(ROW_TILE)]


def _t5_bucket_starts():
    n = np.arange(0, RP_MAX_DIST + 1)
    nf = np.maximum(n, 1).astype(np.float32)
    large = RP_MAX_EXACT + (np.log(nf / np.float32(RP_MAX_EXACT)) / np.float32(math.log(RP_MAX_DIST / RP_MAX_EXACT))
                            * np.float32(RP_BUCKETS - RP_MAX_EXACT)).astype(np.int32)
    large = np.minimum(large, RP_BUCKETS - 1)
    bucket = np.where(n < RP_MAX_EXACT, n, large)
    assert np.all(np.diff(bucket) >= 0) and bucket[-1] == RP_BUCKETS - 1
    return [int(np.argmax(bucket >= j)) for j in range(RP_BUCKETS)]


BUCKET_START = _t5_bucket_starts()


def _ada_kernel(c_ref, w_ref, b_ref, o_ref):
    ca = _silu(c_ref[...]).astype(BF16)
    o_ref[...] = jnp.dot(ca, w_ref[...].astype(BF16), preferred_element_type=F32) + b_ref[...]


def _ada_mod(c, ada_w, ada_b):
    bsz, d = c.shape
    n = ada_w.shape[1]
    rows = 8
    cp = jnp.zeros((rows, d), F32).at[:bsz].set(c)
    tn = 1024
    out = pl.pallas_call(
        _ada_kernel,
        grid=(n // tn,),
        in_specs=[pl.BlockSpec((rows, d), lambda j: (0, 0)),
                  pl.BlockSpec((d, tn), lambda j: (0, j)),
                  pl.BlockSpec((1, tn), lambda j: (0, j))],
        out_specs=pl.BlockSpec((rows, tn), lambda j: (0, j)),
        out_shape=jax.ShapeDtypeStruct((rows, n), F32),
        compiler_params=_cparams(("arbitrary",)),
    )(cp, ada_w, ada_b.reshape(1, n))
    return out[:bsz]


def _inproj_kernel(x_ref, sc_ref, sh_ref, g_ref, w_ref, o_ref, h_scr):
    @pl.when(pl.program_id(1) == 0)
    def _():
        x = x_ref[...]
        ms = jnp.mean(x * x, axis=-1, keepdims=True)
        hn = x * lax.rsqrt(ms + EPS) * g_ref[...]
        h_scr[...] = (hn * (1.0 + sc_ref[...]) + sh_ref[...]).astype(BF16)

    o_ref[...] = jnp.dot(h_scr[...], w_ref[...].astype(BF16), preferred_element_type=F32).astype(o_ref.dtype)


def _in_projection(x2, scale, shift, g, w, seq):
    ntok, d = x2.shape
    n = w.shape[1]
    tm = min(1024, seq)
    tn = 1024
    per_b = seq // tm
    return pl.pallas_call(
        _inproj_kernel,
        grid=(ntok // tm, n // tn),
        in_specs=[pl.BlockSpec((tm, d), lambda i, j: (i, 0)),
                  pl.BlockSpec((None, 1, d), lambda i, j: (i // per_b, 0, 0)),
                  pl.BlockSpec((None, 1, d), lambda i, j: (i // per_b, 0, 0)),
                  pl.BlockSpec((1, d), lambda i, j: (0, 0)),
                  pl.BlockSpec((d, tn), lambda i, j: (0, j))],
        out_specs=pl.BlockSpec((tm, tn), lambda i, j: (i, j)),
        out_shape=jax.ShapeDtypeStruct((ntok, n), BF16),
        scratch_shapes=[pltpu.VMEM((tm, d), BF16)],
        compiler_params=_cparams(("arbitrary", "arbitrary")),
    )(x2, scale[:, None, :], shift[:, None, :], g.reshape(1, d), w)


def _hgrn_kernel(q_ref, f_ref, i_ref, g_ref, lbl_ref, ng_ref, o_ref, state_ref, *, tile):
    @pl.when(pl.program_id(1) == 0)
    def _():
        state_ref[...] = jnp.zeros_like(state_ref)

    n_levels = tile.bit_length() - 1
    row = lax.broadcasted_iota(I32, (tile, tile), 0)
    col = lax.broadcasted_iota(I32, (tile, tile), 1)
    lev = jnp.where(row >= col, 31 - lax.clz(row ^ col), -2)
    on_diag = lev == -1
    at_level = [lev == lvl for lvl in range(n_levels)]
    rowk = lax.broadcasted_iota(I32, (tile, HG_DK), 0)
    odd_at = [(rowk & (1 << lvl)) != 0 for lvl in range(n_levels)]

    for h in range(HG_HEADS):
        sl = slice(h * HEAD_W, (h + 1) * HEAD_W)
        q = _silu(q_ref[:, sl].astype(F32)) * (HG_DK ** -0.5)
        ll = lbl_ref[:, sl]
        el = jnp.exp(ll - jnp.max(ll, axis=0, keepdims=True))
        lb = el[0:1, :] / jnp.sum(el, axis=0, keepdims=True)
        fg = lb + (1.0 - lb) * jax.nn.sigmoid(f_ref[:, sl].astype(F32))
        k = 1.0 - fg
        g = jnp.log2(fg)
        v = i_ref[:, sl]

        scores = jnp.where(on_diag,
                           lax.dot_general(q.astype(BF16), k.astype(BF16), NT_DIMS, preferred_element_type=F32),
                           0.0)
        c = g
        e = g
        for lvl in range(n_levels):
            blk = 1 << lvl
            qd = (q * jnp.exp2(c)).astype(BF16)
            kd = (k * jnp.exp2(e - c)).astype(BF16)
            s_l = lax.dot_general(qd, kd, NT_DIMS, preferred_element_type=F32)
            scores = jnp.where(at_level[lvl], s_l, scores)
            odd = odd_at[lvl]
            e_prev = pltpu.roll(e, blk, axis=0)
            e_next = pltpu.roll(e, tile - blk, axis=0)
            c = c + jnp.where(odd, e_prev, 0.0)
            e = e + jnp.where(odd, e_prev, e_next)
        st = state_ref[h]
        qb = (q * jnp.exp2(c)).astype(BF16)
        o = (jnp.dot(scores.astype(BF16), v, preferred_element_type=F32)
             + lax.dot_general(qb, st.astype(BF16), NT_DIMS, preferred_element_type=F32))
        kd = (k * jnp.exp2(e - c)).astype(BF16)
        state_ref[h] = st * jnp.exp2(e[0:1, :]) + lax.dot_general(v, kd, TN_DIMS, preferred_element_type=F32)

        ms = jnp.mean(o * o, axis=-1, keepdims=True)
        gate = _silu(g_ref[:, sl].astype(F32))
        o_ref[:, sl] = (o * lax.rsqrt(ms + EPS) * ng_ref[...] * gate).astype(o_ref.dtype)


def _hgrn_branch(proj, lb_logits, norm_g, bsz, seq, col0):
    ntok = proj.shape[0]
    tile = min(HGRN_TILE, seq)
    nt = seq // tile
    width = HG_HEADS * HEAD_W
    cb = col0 // width

    def spec(off):
        return pl.BlockSpec((tile, width), lambda b, t, off=off: (b * nt + t, cb + off))

    return pl.pallas_call(
        functools.partial(_hgrn_kernel, tile=tile),
        grid=(bsz, nt),
        in_specs=[spec(0), spec(1), spec(2), spec(3),
                  pl.BlockSpec(lb_logits.shape, lambda b, t: (0, 0)),
                  pl.BlockSpec((1, HG_DV), lambda b, t: (0, 0))],
        out_specs=pl.BlockSpec((tile, width), lambda b, t: (b * nt + t, 0)),
        out_shape=jax.ShapeDtypeStruct((ntok, width), BF16),
        scratch_shapes=[pltpu.VMEM((HG_HEADS, HG_DV, HG_DK), F32)],
        compiler_params=_cparams(("arbitrary", "arbitrary")),
    )(proj, proj, proj, proj, lb_logits, norm_g.reshape(1, HG_DV))


def _qknorm_kernel(q_ref, k_ref, qg_ref, kg_ref, qo_ref, ko_ref):
    lane = lax.broadcasted_iota(I32, (q_ref.shape[0], HEAD_W), 1)
    lo = lane < DA_DH

    def norm(x_ref, g_ref, o_ref, scale):
        for h in range(DA_HEADS):
            sl = slice(h * HEAD_W, (h + 1) * HEAD_W)
            x = x_ref[:, sl].astype(F32)
            xx = x * x
            s0 = jnp.sum(jnp.where(lo, xx, 0.0), axis=-1, keepdims=True)
            s1 = jnp.sum(jnp.where(lo, 0.0, xx), axis=-1, keepdims=True)
            ms = jnp.where(lo, s0, s1) * (1.0 / DA_DH)
            o_ref[:, sl] = (x * lax.rsqrt(ms + EPS) * g_ref[...] * scale).astype(o_ref.dtype)

    norm(q_ref, qg_ref, qo_ref, DA_DH ** -0.5 * LOG2E)
    norm(k_ref, kg_ref, ko_ref, 1.0)


def _qk_norm(proj, q_g, k_g, col_q, col_k):
    ntok = proj.shape[0]
    width = DA_HEADS * HEAD_W
    tm = min(512, ntok)
    qg = jnp.tile(q_g, 2).reshape(1, HEAD_W)
    kg = jnp.tile(k_g, 2).reshape(1, HEAD_W)
    return pl.pallas_call(
        _qknorm_kernel,
        grid=(ntok // tm,),
        in_specs=[pl.BlockSpec((tm, width), lambda i: (i, col_q // width)),
                  pl.BlockSpec((tm, width), lambda i: (i, col_k // width)),
                  pl.BlockSpec((1, HEAD_W), lambda i: (0, 0)),
                  pl.BlockSpec((1, HEAD_W), lambda i: (0, 0))],
        out_specs=[pl.BlockSpec((tm, width), lambda i: (i, 0)),
                   pl.BlockSpec((tm, width), lambda i: (i, 0))],
        out_shape=[jax.ShapeDtypeStruct((ntok, width), BF16)] * 2,
        compiler_params=_cparams(("arbitrary",)),
    )(proj, proj, qg, kg)


def _attn_kernel(qi_ref, ki_ref, q_ref, k_ref, v_ref, rel_ref, lamp_ref, ng_ref, o_ref,
                 m_scr, acc_scr, bias_scr, *, tile):
    b = pl.program_id(0)
    p = pl.program_id(1)
    qi = qi_ref[p]
    ki = ki_ref[p]
    diff = qi - ki

    row = lax.broadcasted_iota(I32, (tile, tile), 0)
    col = lax.broadcasted_iota(I32, (tile, tile), 1)

    @pl.when((b == 0) & (p == 0))
    def _():
        def per_head(h, carry):
            for d in range(2):
                dist = row - col + d * tile
                bias = jnp.full((tile, tile), rel_ref[0, h], F32)
                for j in range(1, RP_BUCKETS):
                    bias = jnp.where(dist >= BUCKET_START[j], rel_ref[j, h], bias)
                bias_scr[d, h] = (bias - rel_ref[RP_BUCKETS - 1, h]) * LOG2E
            return carry
        lax.fori_loop(0, DA_HEADS, per_head, 0)

    @pl.when(ki == 0)
    def _():
        m_scr[...] = jnp.full_like(m_scr, NEG_BIG)
        acc_scr[...] = jnp.zeros_like(acc_scr)

    lane = lax.broadcasted_iota(I32, (tile, HEAD_W), 1)
    lo = lane < DA_DH
    ones = jnp.ones((tile, HEAD_W), BF16)
    reps = tile // HEAD_W

    def step(kind):
        for h in range(DA_HEADS):
            sl = slice(h * HEAD_W, (h + 1) * HEAD_W)
            q = q_ref[:, sl]
            k = k_ref[:, sl]
            v_aug = jnp.concatenate([v_ref[:, sl], ones], axis=1)
            for c in range(2):
                qc = jnp.where(lo, q, jnp.zeros_like(q)) if c == 0 else jnp.where(lo, jnp.zeros_like(q), q)
                s = lax.dot_general(qc, k, NT_DIMS, preferred_element_type=F32)
                if kind != 2:
                    s = s + bias_scr[kind, h]
                if kind == 0:
                    s = jnp.where(row >= col, s, NEG_BIG)
                idx = 2 * h + c
                m_old = m_scr[idx]
                m_cur = jnp.broadcast_to(jnp.max(s, axis=-1, keepdims=True), (tile, HEAD_W))
                m_new = jnp.maximum(m_old, m_cur)
                alpha = jnp.exp2(m_old - m_new)
                pr = jnp.exp2(s - jnp.concatenate([m_new] * reps, axis=1))
                pv = jnp.dot(pr.astype(BF16), v_aug, preferred_element_type=F32)
                acc_scr[idx] = jnp.concatenate([alpha, alpha], axis=1) * acc_scr[idx] + pv
                m_scr[idx] = m_new

    for kind, cond in ((0, diff == 0), (1, diff == 1), (2, diff > 1)):
        pl.when(cond)(functools.partial(step, kind))

    @pl.when(diff == 0)
    def _():
        lp = lamp_ref[...]
        lam = (jnp.exp(jnp.sum(lp[0:1] * lp[1:2], axis=-1, keepdims=True))
               - jnp.exp(jnp.sum(lp[2:3] * lp[3:4], axis=-1, keepdims=True)) + LAMBDA_INIT)
        for h in range(DA_HEADS):
            sl = slice(h * HEAD_W, (h + 1) * HEAD_W)
            a0 = acc_scr[2 * h]
            a1 = acc_scr[2 * h + 1]
            o = a0[:, :DA_DV] / a0[:, DA_DV:] - lam * (a1[:, :DA_DV] / a1[:, DA_DV:])
            ms = jnp.mean(o * o, axis=-1, keepdims=True)
            o_ref[:, sl] = (o * lax.rsqrt(ms + EPS) * ng_ref[...] * (1.0 - LAMBDA_INIT)).astype(o_ref.dtype)


def _diff_attention(qn, kn, proj, col_v, rel_bias, lam_params, norm_g, bsz, seq):
    ntok = qn.shape[0]
    tile = min(ATTN_TILE, seq)
    nq = seq // tile
    width = DA_HEADS * HEAD_W
    qi_list, ki_list = [], []
    for a in range(nq):
        for c in range(a + 1):
            qi_list.append(a)
            ki_list.append(c)
    qi_arr = jnp.asarray(qi_list, I32)
    ki_arr = jnp.asarray(ki_list, I32)
    cv = col_v // width
    grid_spec = pltpu.PrefetchScalarGridSpec(
        num_scalar_prefetch=2,
        grid=(bsz, len(qi_list)),
        in_specs=[pl.BlockSpec((tile, width), lambda b, p, qi, ki: (b * nq + qi[p], 0)),
                  pl.BlockSpec((tile, width), lambda b, p, qi, ki: (b * nq + ki[p], 0)),
                  pl.BlockSpec((tile, width), lambda b, p, qi, ki: (b * nq + ki[p], cv)),
                  pl.BlockSpec(memory_space=pltpu.SMEM),
                  pl.BlockSpec((4, DA_DH), lambda b, p, qi, ki: (0, 0)),
                  pl.BlockSpec((1, DA_DV), lambda b, p, qi, ki: (0, 0))],
        out_specs=pl.BlockSpec((tile, width), lambda b, p, qi, ki: (b * nq + qi[p], 0)),
        scratch_shapes=[pltpu.VMEM((2 * DA_HEADS, tile, HEAD_W), F32),
                        pltpu.VMEM((2 * DA_HEADS, tile, 2 * DA_DV), F32),
                        pltpu.VMEM((2, DA_HEADS, tile, tile), F32)],
    )
    return pl.pallas_call(
        functools.partial(_attn_kernel, tile=tile),
        grid_spec=grid_spec,
        out_shape=jax.ShapeDtypeStruct((ntok, width), BF16),
        compiler_params=_cparams(("arbitrary", "arbitrary")),
    )(qi_arr, ki_arr, qn, kn, proj, rel_bias, lam_params, norm_g.reshape(1, DA_DV))


def _merge_kernel(oa_ref, ob_ref, ga0_ref, ga1_ref, gb0_ref, gb1_ref, x_ref, g1_ref, sc_ref, sh_ref, g2_ref, n2_ref,
                  wa_ref, wb_ref, wo_ref, wrh_ref, wrl_ref, sg_ref, su_ref, sd_ref,
                  base_ref, hp_ref, lg_ref):
    ya = jnp.dot(oa_ref[...], wa_ref[...], preferred_element_type=F32)
    yb = jnp.dot(ob_ref[...], wb_ref[...], preferred_element_type=F32)
    hw = ga0_ref.shape[1]

    def gated(ga_ref, gb_ref, sl):
        return (jax.nn.sigmoid(ga_ref[...].astype(F32)) * ya[:, sl]
                + jax.nn.sigmoid(gb_ref[...].astype(F32)) * yb[:, sl]).astype(BF16)

    merged = jnp.concatenate([gated(ga0_ref, gb0_ref, slice(0, hw)),
                              gated(ga1_ref, gb1_ref, slice(hw, 2 * hw))], axis=1)
    y = jnp.dot(merged, wo_ref[...], preferred_element_type=F32)
    x1 = x_ref[...] + g1_ref[...] * y
    ms = jnp.mean(x1 * x1, axis=-1, keepdims=True)
    h2 = (x1 * lax.rsqrt(ms + EPS) * n2_ref[...]) * (1.0 + sc_ref[...]) + sh_ref[...]
    half = h2.shape[1] // 2
    _store_token_major(hp_ref, _pack_bf16_pair(h2[:, :half], h2[:, half:]))
    hh = h2.astype(BF16)
    a = (_silu(jnp.dot(hh, sg_ref[...], preferred_element_type=F32))
         * jnp.dot(hh, su_ref[...], preferred_element_type=F32)).astype(BF16)
    base_ref[...] = x1 + g2_ref[...] * jnp.dot(a, sd_ref[...], preferred_element_type=F32)
    hl = (h2 - hh.astype(F32)).astype(BF16)
    lg_ref[...] = (lax.dot_general(wrh_ref[...], hh, NT_DIMS, preferred_element_type=F32)
                   + lax.dot_general(wrh_ref[...], hl, NT_DIMS, preferred_element_type=F32)
                   + lax.dot_general(wrl_ref[...], hh, NT_DIMS, preferred_element_type=F32))


def _merge_out(o_hg, o_da, proj, col_ga, col_gb, x2, gate1, scale2, shift2, gate2, norm2_g,
               wa, wb, wo, wr_hi, wr_lo, sg, su, sd, seq):
    ntok, d = x2.shape
    tm = min(MERGE_TILE, seq)
    per_b = seq // tm
    wa_w = o_hg.shape[1]
    wb_w = o_da.shape[1]
    de = sg.shape[1]

    def const(shape):
        return pl.BlockSpec(shape, lambda i: (0,) * len(shape), pipeline_mode=pl.Buffered(1))

    def perb():
        return pl.BlockSpec((None, 1, d), lambda i: (i // per_b, 0, 0))

    return pl.pallas_call(
        _merge_kernel,
        grid=(ntok // tm,),
        in_specs=[pl.BlockSpec((tm, wa_w), lambda i: (i, 0)),
                  pl.BlockSpec((tm, wb_w), lambda i: (i, 0)),
                  pl.BlockSpec((tm, d // 2), lambda i: (i, col_ga // (d // 2))),
                  pl.BlockSpec((tm, d // 2), lambda i: (i, col_ga // (d // 2) + 1)),
                  pl.BlockSpec((tm, d // 2), lambda i: (i, col_gb // (d // 2))),
                  pl.BlockSpec((tm, d // 2), lambda i: (i, col_gb // (d // 2) + 1)),
                  pl.BlockSpec((tm, d), lambda i: (i, 0)),
                  perb(), perb(), perb(), perb(),
                  const((1, d)),
                  const((wa_w, d)), const((wb_w, d)), const((d, d)),
                  const((N_EXPERTS, d)), const((N_EXPERTS, d)),
                  const((d, de)), const((d, de)), const((de, d))],
        out_specs=[pl.BlockSpec((tm, d), lambda i: (i, 0)),
                   pl.BlockSpec((tm * ROW_TILE, HEAD_W), lambda i: (i, 0)),
                   pl.BlockSpec((N_EXPERTS, tm), lambda i: (0, i))],
        out_shape=[jax.ShapeDtypeStruct((ntok, d), F32),
                   jax.ShapeDtypeStruct((ntok * ROW_TILE, HEAD_W), U32),
                   jax.ShapeDtypeStruct((N_EXPERTS, ntok), F32)],
        compiler_params=_cparams(("arbitrary",)),
    )(o_hg, o_da, proj, proj, proj, proj, x2, gate1[:, None, :], scale2[:, None, :], shift2[:, None, :],
      gate2[:, None, :], norm2_g.reshape(1, d), wa, wb, wo, wr_hi, wr_lo, sg, su, sd)


def _route_kernel(lg_ref, rb_ref, te_ref, gw_ref, rk_ref, cnt_ref, r1_ref, ex_ref, carry_scr, *, tt):
    @pl.when(pl.program_id(0) == 0)
    def _():
        carry_scr[...] = jnp.zeros_like(carry_scr)

    per_g = N_EXPERTS // N_GROUPS
    scores = jax.nn.sigmoid(lg_ref[...])
    sel = scores + rb_ref[...]
    sel3 = sel.reshape(N_GROUPS, per_g, tt)
    j_io = lax.broadcasted_iota(I32, (N_GROUPS, per_g, tt), 1)
    m1 = jnp.max(sel3, axis=1, keepdims=True)
    i1 = jnp.min(jnp.where(sel3 == m1, j_io, per_g), axis=1, keepdims=True)
    m2 = jnp.max(jnp.where(j_io == i1, -jnp.inf, sel3), axis=1, keepdims=True)
    gs = (m1 + m2).reshape(N_GROUPS, tt)
    g_io = lax.broadcasted_iota(I32, (N_GROUPS, tt), 0)
    gmask = jnp.zeros((N_GROUPS, tt), jnp.bool_)
    for _ in range(TOPK_GROUPS):
        gm = jnp.max(gs, axis=0, keepdims=True)
        gi = jnp.min(jnp.where(gs == gm, g_io, N_GROUPS), axis=0, keepdims=True)
        hit = g_io == gi
        gmask = gmask | hit
        gs = jnp.where(hit, -jnp.inf, gs)
    emask = jnp.broadcast_to(gmask.reshape(N_GROUPS, 1, tt), (N_GROUPS, per_g, tt)).reshape(N_EXPERTS, tt)
    cand = jnp.where(emask, sel, -jnp.inf)
    e_io = lax.broadcasted_iota(I32, (N_EXPERTS, tt), 0)
    chosen = jnp.zeros((N_EXPERTS, tt), jnp.bool_)
    picks = []
    for _ in range(TOP_K):
        em = jnp.max(cand, axis=0, keepdims=True)
        ei = jnp.min(jnp.where(cand == em, e_io, N_EXPERTS), axis=0, keepdims=True)
        hit = e_io == ei
        chosen = chosen | hit
        cand = jnp.where(hit, -jnp.inf, cand)
        picks.append((ei, hit))
    gsel = jnp.where(chosen, scores, 0.0)
    wnorm = gsel / jnp.sum(gsel, axis=0, keepdims=True) * ROUTE_SCALE
    ch = jnp.where(chosen, 1.0, 0.0)
    tri = (lax.broadcasted_iota(I32, (tt, tt), 0) <= lax.broadcasted_iota(I32, (tt, tt), 1))
    incl = jnp.dot(ch.astype(BF16), jnp.where(tri, 1.0, 0.0).astype(BF16), preferred_element_type=F32)
    carry = carry_scr[...]
    excl = incl - ch + carry
    carry_new = carry + incl[:, tt - 1:tt]
    carry_scr[...] = carry_new
    r1_ref[...] = jnp.where(chosen, excl + 1.0, 0.0).astype(I32)
    ex_ref[...] = excl.astype(I32)
    for r, (ei, hit) in enumerate(picks):
        te_ref[r:r + 1, :] = ei
        gw_ref[r:r + 1, :] = jnp.sum(jnp.where(hit, wnorm, 0.0), axis=0, keepdims=True)
        rk_ref[r:r + 1, :] = jnp.sum(jnp.where(hit, excl, 0.0), axis=0, keepdims=True).astype(I32)
    cnt_ref[...] = jnp.broadcast_to(carry_new, cnt_ref.shape).astype(I32)


def _route(logits_t, router_bias):
    ntok = logits_t.shape[1]
    tt = min(ROUTE_TILE, ntok)
    return pl.pallas_call(
        functools.partial(_route_kernel, tt=tt),
        grid=(ntok // tt,),
        in_specs=[pl.BlockSpec((N_EXPERTS, tt), lambda i: (0, i)),
                  pl.BlockSpec((N_EXPERTS, 1), lambda i: (0, 0))],
        out_specs=[pl.BlockSpec((TOP_K, tt), lambda i: (0, i)),
                   pl.BlockSpec((TOP_K, tt), lambda i: (0, i)),
                   pl.BlockSpec((TOP_K, tt), lambda i: (0, i)),
                   pl.BlockSpec((N_EXPERTS, 128), lambda i: (0, 0)),
                   pl.BlockSpec((N_EXPERTS, tt), lambda i: (0, i)),
                   pl.BlockSpec((N_EXPERTS, tt), lambda i: (0, i))],
        out_shape=[jax.ShapeDtypeStruct((TOP_K, ntok), I32),
                   jax.ShapeDtypeStruct((TOP_K, ntok), F32),
                   jax.ShapeDtypeStruct((TOP_K, ntok), I32),
                   jax.ShapeDtypeStruct((N_EXPERTS, 128), I32),
                   jax.ShapeDtypeStruct((N_EXPERTS, ntok), I32),
                   jax.ShapeDtypeStruct((N_EXPERTS, ntok), I32)],
        scratch_shapes=[pltpu.VMEM((N_EXPERTS, 1), F32)],
        compiler_params=_cparams(("arbitrary",)),
    )(logits_t, router_bias.reshape(N_EXPERTS, 1))


def _slots_kernel(ps_ref, te_ref, rk_ref, d_ref):
    te = te_ref[...]
    dest = rk_ref[...]
    for e in range(N_EXPERTS):
        dest = dest + jnp.where(te == e, ps_ref[e], 0)
    d_ref[...] = dest


def _slots(pad_start, top_e, rank):
    ntok = top_e.shape[1]
    tt = min(2048, ntok)
    return pl.pallas_call(
        _slots_kernel,
        grid=(ntok // tt,),
        in_specs=[pl.BlockSpec(memory_space=pltpu.SMEM),
                  pl.BlockSpec((TOP_K, tt), lambda i: (0, i)),
                  pl.BlockSpec((TOP_K, tt), lambda i: (0, i))],
        out_specs=pl.BlockSpec((TOP_K, tt), lambda i: (0, i)),
        out_shape=jax.ShapeDtypeStruct((TOP_K, ntok), I32),
        compiler_params=_cparams(("arbitrary",)),
    )(pad_start, top_e, rank)


BLOCKS_PER_STEP = 8


def _slot_tokens_kernel(be_ref, bs_ref, r_ref, base_ref, o_ref, *, tm):
    i = pl.program_id(0)
    n_ch = r_ref.shape[1]
    ch_io = lax.broadcasted_iota(I32, (n_ch, tm), 0)
    l_io = lax.broadcasted_iota(I32, (HEAD_W, tm), 0)
    j_io = lax.broadcasted_iota(I32, (1, tm), 1)
    for b in range(BLOCKS_PER_STEP):
        blk = i * BLOCKS_PER_STEP + b
        e = be_ref[blk]
        g = (blk - bs_ref[e]) * tm + j_io
        ch = jnp.sum((base_ref[e] <= g).astype(I32), axis=0, keepdims=True) - 1
        pick = jnp.where(ch_io == ch, 1.0, 0.0).astype(BF16)
        r = r_ref[e]
        hi = (r >> 7).astype(F32).astype(BF16)
        lo = (r & 127).astype(F32).astype(BF16)
        ranks = (lax.dot_general(hi, pick, TN_DIMS, preferred_element_type=F32) * 128.0
                 + lax.dot_general(lo, pick, TN_DIMS, preferred_element_type=F32))
        hit = ranks == (g + 1).astype(F32)
        lane = jnp.sum(jnp.where(hit, l_io, 0), axis=0, keepdims=True)
        o_ref[b] = ch * HEAD_W + lane


def _slot_tokens(r1, excl, block_e, blk_start, tm):
    ntok = r1.shape[1]
    n_ch = ntok // HEAD_W
    n_blocks = block_e.shape[0]
    r3 = r1.reshape(N_EXPERTS, n_ch, HEAD_W)
    base = excl[:, ::HEAD_W].reshape(N_EXPERTS, n_ch, 1)
    grid_spec = pltpu.PrefetchScalarGridSpec(
        num_scalar_prefetch=2,
        grid=(n_blocks // BLOCKS_PER_STEP,),
        in_specs=[pl.BlockSpec((N_EXPERTS, n_ch, HEAD_W), lambda i, be, bs: (0, 0, 0)),
                  pl.BlockSpec((N_EXPERTS, n_ch, 1), lambda i, be, bs: (0, 0, 0))],
        out_specs=pl.BlockSpec((BLOCKS_PER_STEP, 1, tm), lambda i, be, bs: (i, 0, 0)),
    )
    return pl.pallas_call(
        functools.partial(_slot_tokens_kernel, tm=tm),
        grid_spec=grid_spec,
        out_shape=jax.ShapeDtypeStruct((n_blocks, 1, tm), I32),
        compiler_params=_cparams(("arbitrary",)),
    )(block_e, blk_start, r3, base)


GATHER_UNROLL = 8


def _expert_kernel(be_ref, nu_ref, we_ref, idx_ref, idxn_ref, h_hbm, wg_ref, wu_ref, wd_ref, o_ref,
                   xs, wgb, wub, wdb, sem, *, tm):
    del we_ref
    i = pl.program_id(0)
    n_used = nu_ref[0]
    slot = i % 2

    def row_copy(tok, s, r):
        return pltpu.make_async_copy(h_hbm.at[tok], xs.at[s, pl.ds(r * ROW_TILE, ROW_TILE)], sem.at[s])

    def issue(idx_r, s):
        def body(g, carry):
            for u in range(GATHER_UNROLL):
                r = g * GATHER_UNROLL + u
                row_copy(idx_r[0, r], s, r).start(priority=u % 2)
            return carry
        lax.fori_loop(0, tm // GATHER_UNROLL, body, 0)

    @pl.when(i == 0)
    def _():
        issue(idx_ref, 0)

    @pl.when(i + 1 < n_used)
    def _():
        issue(idxn_ref, 1 - slot)

    @pl.when(i < n_used)
    def _():
        e = be_ref[i]
        e_prev = be_ref[jnp.maximum(i - 1, 0)]

        @pl.when((i == 0) | (e != e_prev))
        def _():
            wgb[...] = wg_ref[...].astype(BF16)
            wub[...] = wu_ref[...].astype(BF16)
            wdb[...] = wd_ref[...].astype(BF16)

        def wait_body(g, carry):
            for u in range(GATHER_UNROLL):
                row_copy(0, slot, g * GATHER_UNROLL + u).wait()
            return carry
        lax.fori_loop(0, tm // GATHER_UNROLL, wait_body, 0)

        xlo, xhi = _unpack_bf16_pair(jnp.concatenate(_load_token_major(xs.at[slot], 0, tm), axis=1))
        xlo = xlo.astype(BF16)
        xhi = xhi.astype(BF16)
        half = xlo.shape[1]

        def proj_in(w):
            return (jnp.dot(xlo, w[:half, :], preferred_element_type=F32)
                    + jnp.dot(xhi, w[half:, :], preferred_element_type=F32))

        a = (_silu(proj_in(wgb)) * proj_in(wub)).astype(BF16)
        y = jnp.dot(a, wdb[...], preferred_element_type=F32)
        _store_token_major(o_ref, _pack_bf16_pair(y[:, :half], y[:, half:]))

    @pl.when(i >= n_used)
    def _():
        o_ref[...] = jnp.zeros_like(o_ref)


def _routed_experts(h2p, idx3, block_e, n_used, w_expert, w_gate, w_up, w_down, tm):
    n_blocks = block_e.shape[0]
    d, de = w_gate.shape[-2:]
    h3 = h2p.reshape(-1, ROW_TILE, HEAD_W)

    def wspec(shape):
        return pl.BlockSpec((None, None) + shape, lambda i, be, nu, we: (0, we[i], 0, 0))

    grid_spec = pltpu.PrefetchScalarGridSpec(
        num_scalar_prefetch=3,
        grid=(n_blocks,),
        in_specs=[pl.BlockSpec((None, 1, tm), lambda i, be, nu, we: (i, 0, 0), memory_space=pltpu.SMEM),
                  pl.BlockSpec((None, 1, tm), lambda i, be, nu, we: (jnp.minimum(i + 1, n_blocks - 1), 0, 0),
                               memory_space=pltpu.SMEM),
                  pl.BlockSpec(memory_space=pl.ANY),
                  wspec((d, de)), wspec((d, de)), wspec((de, d))],
        out_specs=pl.BlockSpec((tm * ROW_TILE, HEAD_W), lambda i, be, nu, we: (i, 0)),
        scratch_shapes=[pltpu.VMEM((2, tm * ROW_TILE, HEAD_W), U32),
                        pltpu.VMEM((d, de), BF16), pltpu.VMEM((d, de), BF16), pltpu.VMEM((de, d), BF16),
                        pltpu.SemaphoreType.DMA((2,))],
    )
    return pl.pallas_call(
        functools.partial(_expert_kernel, tm=tm),
        grid_spec=grid_spec,
        out_shape=jax.ShapeDtypeStruct((n_blocks * tm * ROW_TILE, HEAD_W), U32),
        compiler_params=_cparams(("arbitrary",), disable_bounds_checks=True),
    )(block_e, n_used, w_expert, idx3, idx3, h3, w_gate, w_up, w_down)


def _combine_kernel(d_ref, dn_ref, ys_hbm, gw_ref, base_ref, g2_ref, o_ref, buf, sem, *, tm):
    i = pl.program_id(0)
    n = pl.num_programs(0)
    slot = i % 2

    def row_copy(src, s, r, t):
        return pltpu.make_async_copy(ys_hbm.at[src], buf.at[s, pl.ds((r * tm + t) * ROW_TILE, ROW_TILE)],
                                     sem.at[s])

    def issue(d_r, s):
        def body(t, carry):
            for r in range(TOP_K):
                row_copy(d_r[r, t], s, r, t).start(priority=r % 2)
            return carry
        lax.fori_loop(0, tm, body, 0)

    @pl.when(i == 0)
    def _():
        issue(d_ref, 0)

    @pl.when(i + 1 < n)
    def _():
        issue(dn_ref, 1 - slot)

    def wait_body(t, carry):
        for r in range(TOP_K):
            row_copy(0, slot, r, t).wait()
        return carry
    lax.fori_loop(0, tm, wait_body, 0)

    half = ROW_TILE * HEAD_W
    acc_lo = [jnp.zeros((tm, HEAD_W), F32)] * ROW_TILE
    acc_hi = [jnp.zeros((tm, HEAD_W), F32)] * ROW_TILE
    for r in range(TOP_K):
        w = jnp.broadcast_to(gw_ref[:, r:r + 1], (tm, HEAD_W))
        for j, piece in enumerate(_load_token_major(buf.at[slot], r * tm, tm)):
            lo, hi = _unpack_bf16_pair(piece)
            acc_lo[j] = acc_lo[j] + w * lo
            acc_hi[j] = acc_hi[j] + w * hi
    for j in range(ROW_TILE):
        lo_sl = slice(j * HEAD_W, (j + 1) * HEAD_W)
        hi_sl = slice(half + j * HEAD_W, half + (j + 1) * HEAD_W)
        o_ref[:, lo_sl] = base_ref[:, lo_sl] + g2_ref[:, lo_sl] * acc_lo[j]
        o_ref[:, hi_sl] = base_ref[:, hi_sl] + g2_ref[:, hi_sl] * acc_hi[j]


def _combine(ys, dest, gate_w, base, gate2, seq):
    ntok, d = base.shape
    y3 = ys.reshape(-1, ROW_TILE, HEAD_W)
    tm = min(COMBINE_TILE, seq)
    nt = ntok // tm
    per_b = seq // tm
    d3 = dest.reshape(TOP_K, nt, tm).transpose(1, 0, 2)
    return pl.pallas_call(
        functools.partial(_combine_kernel, tm=tm),
        grid=(nt,),
        in_specs=[pl.BlockSpec((None, TOP_K, tm), lambda i: (i, 0, 0), memory_space=pltpu.SMEM),
                  pl.BlockSpec((None, TOP_K, tm), lambda i: (jnp.minimum(i + 1, nt - 1), 0, 0),
                               memory_space=pltpu.SMEM),
                  pl.BlockSpec(memory_space=pl.ANY),
                  pl.BlockSpec((tm, TOP_K), lambda i: (i, 0)),
                  pl.BlockSpec((tm, d), lambda i: (i, 0)),
                  pl.BlockSpec((None, 1, d), lambda i: (i // per_b, 0, 0))],
        out_specs=pl.BlockSpec((tm, d), lambda i: (i, 0)),
        out_shape=jax.ShapeDtypeStruct((ntok, d), F32),
        scratch_shapes=[pltpu.VMEM((2, TOP_K * tm * ROW_TILE, HEAD_W), U32), pltpu.SemaphoreType.DMA((2,))],
        compiler_params=_cparams(("arbitrary",), disable_bounds_checks=True),
    )(d3, d3, y3, gate_w.T, base, gate2[:, None, :])


def kernel(x, c, ada_w, ada_b, norm1_g, w_in, lb_logits, hg_norm_g, q_norm_g, k_norm_g, lambda_q1, lambda_k1,
           lambda_q2, lambda_k2, da_norm_g, rel_bias, w_branch_a, w_branch_b, w_out, norm2_g, router_w,
           router_bias, w_exp_gate, w_exp_up, w_exp_down, w_sh_gate, w_sh_up, w_sh_down):
    bsz, seq, d = x.shape
    ntok = bsz * seq
    l = 0
    x2 = x.reshape(ntok, d)

    mod = _ada_mod(c, ada_w[l], ada_b[l])
    shift1, scale1, gate1, shift2, scale2, gate2 = jnp.split(mod, 6, axis=-1)

    proj = _in_projection(x2, scale1, shift1, norm1_g[l], w_in[l], seq)
    col_hg = 0
    col_q = col_hg + 4 * HG_HEADS * HEAD_W
    col_k = col_q + DA_HEADS * HEAD_W
    col_v = col_k + DA_HEADS * HEAD_W
    col_ga = col_v + DA_HEADS * HEAD_W
    col_gb = col_ga + d

    o_hg = _hgrn_branch(proj, lb_logits, hg_norm_g[l], bsz, seq, col_hg)
    qn, kn = _qk_norm(proj, q_norm_g[l], k_norm_g[l], col_q, col_k)
    lam_params = jnp.stack([lambda_q1[l], lambda_k1[l], lambda_q2[l], lambda_k2[l]])
    o_da = _diff_attention(qn, kn, proj, col_v, rel_bias, lam_params, da_norm_g[l], bsz, seq)

    wr_t = router_w[l].T
    wr_hi = wr_t.astype(BF16)
    wr_lo = (wr_t - wr_hi.astype(F32)).astype(BF16)
    base, h2p, logits_t = _merge_out(
        o_hg, o_da, proj, col_ga, col_gb, x2, gate1, scale2, shift2, gate2, norm2_g[l],
        w_branch_a[l].astype(BF16), w_branch_b[l].astype(BF16), w_out[l].astype(BF16), wr_hi, wr_lo,
        w_sh_gate[l].astype(BF16), w_sh_up[l].astype(BF16), w_sh_down[l].astype(BF16), seq)

    top_e, gate_w, rank, counts, rank1, excl = _route(logits_t, router_bias[l])

    tm_e = EXPERT_TILE
    n_blocks = (ntok * TOP_K) // tm_e + N_EXPERTS
    nblk = (counts[:, 0] + tm_e - 1) // tm_e
    blk_end = jnp.cumsum(nblk)
    blk_start = (blk_end - nblk).astype(I32)
    pad_start = blk_start * tm_e
    n_used = blk_end[-1:].astype(I32)
    block_e = jnp.minimum(jnp.sum(blk_end[None, :] <= jnp.arange(n_blocks, dtype=I32)[:, None], axis=1),
                          N_EXPERTS - 1).astype(I32)

    dest = _slots(pad_start, top_e, rank)
    slot_tok = _slot_tokens(rank1, excl, block_e, blk_start, tm_e)
    e_ids = jnp.arange(N_EXPERTS, dtype=I32)
    later_nonempty = (e_ids[None, :] > e_ids[:, None]) & (nblk[None, :] > 0)
    next_e = jnp.min(jnp.where(later_nonempty, e_ids[None, :], N_EXPERTS), axis=1)
    next_e = jnp.where(next_e < N_EXPERTS, next_e, e_ids)
    blk_ids = jnp.arange(n_blocks, dtype=I32)
    of_expert = block_e[:, None] == e_ids[None, :]
    is_first = jnp.any(of_expert & (blk_ids[:, None] == blk_start[None, :]), axis=1)
    w_expert = jnp.where(is_first, block_e, jnp.sum(jnp.where(of_expert, next_e[None, :], 0), axis=1)).astype(I32)
    ys = _routed_experts(h2p, slot_tok, block_e, n_used, w_expert, w_exp_gate, w_exp_up, w_exp_down, tm_e)
    out = _combine(ys, dest, gate_w, base, gate2, seq)
    return out.reshape(bsz, seq, d)
```

```python
import functools
import math

import numpy as np
import jax
import jax.numpy as jnp
from jax import lax
from jax.experimental import pallas as pl
from jax.experimental.pallas import tpu as pltpu

F32 = jnp.float32
BF16 = jnp.bfloat16
I32 = jnp.int32
U32 = jnp.uint32

HG_HEADS = 8
HG_DK = 128
HG_DV = 128
DA_HEADS = 8
DA_DH = 64
DA_DV = 128
RP_BUCKETS = 32
RP_MAX_EXACT = 16
RP_MAX_DIST = 128
N_EXPERTS = 64
N_GROUPS = 8
TOPK_GROUPS = 4
TOP_K = 8
ROUTE_SCALE = 2.5
EPS = 1e-6
LAMBDA_INIT = 0.8 - 0.6 * math.exp(-0.3 * 0)

HEAD_W = 128
VMEM_LIMIT_BYTES = 56 * 1024 * 1024
NEG_BIG = -1e30
LOG2E = math.log2(math.e)

ATTN_TILE = 512
HGRN_TILE = 128
MERGE_TILE = 256
EXPERT_TILE = 256
COMBINE_TILE = 128
ROUTE_TILE = 512

NT_DIMS = (((1,), (1,)), ((), ()))
TN_DIMS = (((0,), (0,)), ((), ()))


def _cparams(sem, **kw):
    return pltpu.CompilerParams(dimension_semantics=sem, vmem_limit_bytes=VMEM_LIMIT_BYTES, **kw)


def _silu(x):
    return x * jax.nn.sigmoid(x)


HI_MASK = np.uint32(0xFFFF0000)
BF16_HALF_ULP = np.uint32(0x8000)


def _pack_bf16_pair(a, b):
    ua = lax.bitcast_convert_type(a, U32) + BF16_HALF_ULP
    ub = lax.bitcast_convert_type(b, U32) + BF16_HALF_ULP
    return (ua >> 16) | (ub & HI_MASK)


def _unpack_bf16_pair(w):
    lo = lax.bitcast_convert_type(w << 16, F32)
    hi = lax.bitcast_convert_type(w & HI_MASK, F32)
    return lo, hi


ROW_TILE = 8


def _store_token_major(ref, x):
    rows = x.shape[0]
    for j in range(ROW_TILE):
        ref[pl.ds(j, rows, stride=ROW_TILE), :] = x[:, j * HEAD_W:(j + 1) * HEAD_W]


def _load_token_major(ref, first, rows):
    return [ref[pl.ds(first * ROW_TILE + j, rows, stride=ROW_TILE), :] for j in range(ROW_TILE)]


def _t5_bucket_starts():
    n = np.arange(0, RP_MAX_DIST + 1)
    nf = np.maximum(n, 1).astype(np.float32)
    large = RP_MAX_EXACT + (np.log(nf / np.float32(RP_MAX_EXACT)) / np.float32(math.log(RP_MAX_DIST / RP_MAX_EXACT))
                            * np.float32(RP_BUCKETS - RP_MAX_EXACT)).astype(np.int32)
    large = np.minimum(large, RP_BUCKETS - 1)
    bucket = np.where(n < RP_MAX_EXACT, n, large)
    assert np.all(np.diff(bucket) >= 0) and bucket[-1] == RP_BUCKETS - 1
    return [int(np.argmax(bucket >= j)) for j in range(RP_BUCKETS)]


BUCKET_START = _t5_bucket_starts()


def _ada_kernel(c_ref, w_ref, b_ref, o_ref):
    ca = _silu(c_ref[...]).astype(BF16)
    o_ref[...] = jnp.dot(ca, w_ref[...].astype(BF16), preferred_element_type=F32) + b_ref[...]


def _ada_mod(c, ada_w, ada_b):
    bsz, d = c.shape
    n = ada_w.shape[1]
    rows = 8
    cp = jnp.zeros((rows, d), F32).at[:bsz].set(c)
    tn = 1024
    out = pl.pallas_call(
        _ada_kernel,
        grid=(n // tn,),
        in_specs=[pl.BlockSpec((rows, d), lambda j: (0, 0)),
                  pl.BlockSpec((d, tn), lambda j: (0, j)),
                  pl.BlockSpec((1, tn), lambda j: (0, j))],
        out_specs=pl.BlockSpec((rows, tn), lambda j: (0, j)),
        out_shape=jax.ShapeDtypeStruct((rows, n), F32),
        compiler_params=_cparams(("arbitrary",)),
    )(cp, ada_w, ada_b.reshape(1, n))
    return out[:bsz]


def _inproj_kernel(x_ref, sc_ref, sh_ref, g_ref, w_ref, o_ref, h_scr):
    @pl.when(pl.program_id(1) == 0)
    def _():
        x = x_ref[...]
        ms = jnp.mean(x * x, axis=-1, keepdims=True)
        hn = x * lax.rsqrt(ms + EPS) * g_ref[...]
        h_scr[...] = (hn * (1.0 + sc_ref[...]) + sh_ref[...]).astype(BF16)

    o_ref[...] = jnp.dot(h_scr[...], w_ref[...].astype(BF16), preferred_element_type=F32).astype(o_ref.dtype)


def _in_projection(x2, scale, shift, g, w, seq):
    ntok, d = x2.shape
    n = w.shape[1]
    tm = min(1024, seq)
    tn = 1024
    per_b = seq // tm
    return pl.pallas_call(
        _inproj_kernel,
        grid=(ntok // tm, n // tn),
        in_specs=[pl.BlockSpec((tm, d), lambda i, j: (i, 0)),
                  pl.BlockSpec((None, 1, d), lambda i, j: (i // per_b, 0, 0)),
                  pl.BlockSpec((None, 1, d), lambda i, j: (i // per_b, 0, 0)),
                  pl.BlockSpec((1, d), lambda i, j: (0, 0)),
                  pl.BlockSpec((d, tn), lambda i, j: (0, j))],
        out_specs=pl.BlockSpec((tm, tn), lambda i, j: (i, j)),
        out_shape=jax.ShapeDtypeStruct((ntok, n), BF16),
        scratch_shapes=[pltpu.VMEM((tm, d), BF16)],
        compiler_params=_cparams(("arbitrary", "arbitrary")),
    )(x2, scale[:, None, :], shift[:, None, :], g.reshape(1, d), w)


def _hgrn_kernel(q_ref, f_ref, i_ref, g_ref, lbl_ref, ng_ref, o_ref, state_ref, *, tile):
    @pl.when(pl.program_id(1) == 0)
    def _():
        state_ref[...] = jnp.zeros_like(state_ref)

    n_levels = tile.bit_length() - 1
    row = lax.broadcasted_iota(I32, (tile, tile), 0)
    col = lax.broadcasted_iota(I32, (tile, tile), 1)
    lev = jnp.where(row >= col, 31 - lax.clz(row ^ col), -2)
    on_diag = lev == -1
    at_level = [lev == lvl for lvl in range(n_levels)]
    rowk = lax.broadcasted_iota(I32, (tile, HG_DK), 0)
    odd_at = [(rowk & (1 << lvl)) != 0 for lvl in range(n_levels)]

    for h in range(HG_HEADS):
        sl = slice(h * HEAD_W, (h + 1) * HEAD_W)
        q = _silu(q_ref[:, sl].astype(F32)) * (HG_DK ** -0.5)
        ll = lbl_ref[:, sl]
        el = jnp.exp(ll - jnp.max(ll, axis=0, keepdims=True))
        lb = el[0:1, :] / jnp.sum(el, axis=0, keepdims=True)
        fg = lb + (1.0 - lb) * jax.nn.sigmoid(f_ref[:, sl].astype(F32))
        k = 1.0 - fg
        g = jnp.log2(fg)
        v = i_ref[:, sl]

        scores = jnp.where(on_diag,
                           lax.dot_general(q.astype(BF16), k.astype(BF16), NT_DIMS, preferred_element_type=F32),
                           0.0)
        c = g
        e = g
        for lvl in range(n_levels):
            blk = 1 << lvl
            qd = (q * jnp.exp2(c)).astype(BF16)
            kd = (k * jnp.exp2(e - c)).astype(BF16)
            s_l = lax.dot_general(qd, kd, NT_DIMS, preferred_element_type=F32)
            scores = jnp.where(at_level[lvl], s_l, scores)
            odd = odd_at[lvl]
            e_prev = pltpu.roll(e, blk, axis=0)
            e_next = pltpu.roll(e, tile - blk, axis=0)
            c = c + jnp.where(odd, e_prev, 0.0)
            e = e + jnp.where(odd, e_prev, e_next)
        st = state_ref[h]
        qb = (q * jnp.exp2(c)).astype(BF16)
        o = (jnp.dot(scores.astype(BF16), v, preferred_element_type=F32)
             + lax.dot_general(qb, st.astype(BF16), NT_DIMS, preferred_element_type=F32))
        kd = (k * jnp.exp2(e - c)).astype(BF16)
        state_ref[h] = st * jnp.exp2(e[0:1, :]) + lax.dot_general(v, kd, TN_DIMS, preferred_element_type=F32)

        ms = jnp.mean(o * o, axis=-1, keepdims=True)
        gate = _silu(g_ref[:, sl].astype(F32))
        o_ref[:, sl] = (o * lax.rsqrt(ms + EPS) * ng_ref[...] * gate).astype(o_ref.dtype)


def _hgrn_branch(proj, lb_logits, norm_g, bsz, seq, col0):
    ntok = proj.shape[0]
    tile = min(HGRN_TILE, seq)
    nt = seq // tile
    width = HG_HEADS * HEAD_W
    cb = col0 // width

    def spec(off):
        return pl.BlockSpec((tile, width), lambda b, t, off=off: (b * nt + t, cb + off))

    return pl.pallas_call(
        functools.partial(_hgrn_kernel, tile=tile),
        grid=(bsz, nt),
        in_specs=[spec(0), spec(1), spec(2), spec(3),
                  pl.BlockSpec(lb_logits.shape, lambda b, t: (0, 0)),
                  pl.BlockSpec((1, HG_DV), lambda b, t: (0, 0))],
        out_specs=pl.BlockSpec((tile, width), lambda b, t: (b * nt + t, 0)),
        out_shape=jax.ShapeDtypeStruct((ntok, width), BF16),
        scratch_shapes=[pltpu.VMEM((HG_HEADS, HG_DV, HG_DK), F32)],
        compiler_params=_cparams(("arbitrary", "arbitrary")),
    )(proj, proj, proj, proj, lb_logits, norm_g.reshape(1, HG_DV))


def _qknorm_kernel(q_ref, k_ref, qg_ref, kg_ref, qo_ref, ko_ref):
    lane = lax.broadcasted_iota(I32, (q_ref.shape[0], HEAD_W), 1)
    lo = lane < DA_DH

    def norm(x_ref, g_ref, o_ref, scale):
        for h in range(DA_HEADS):
            sl = slice(h * HEAD_W, (h + 1) * HEAD_W)
            x = x_ref[:, sl].astype(F32)
            xx = x * x
            s0 = jnp.sum(jnp.where(lo, xx, 0.0), axis=-1, keepdims=True)
            s1 = jnp.sum(jnp.where(lo, 0.0, xx), axis=-1, keepdims=True)
            ms = jnp.where(lo, s0, s1) * (1.0 / DA_DH)
            o_ref[:, sl] = (x * lax.rsqrt(ms + EPS) * g_ref[...] * scale).astype(o_ref.dtype)

    norm(q_ref, qg_ref, qo_ref, DA_DH ** -0.5 * LOG2E)
    norm(k_ref, kg_ref, ko_ref, 1.0)


def _qk_norm(proj, q_g, k_g, col_q, col_k):
    ntok = proj.shape[0]
    width = DA_HEADS * HEAD_W
    tm = min(512, ntok)
    qg = jnp.tile(q_g, 2).reshape(1, HEAD_W)
    kg = jnp.tile(k_g, 2).reshape(1, HEAD_W)
    return pl.pallas_call(
        _qknorm_kernel,
        grid=(ntok // tm,),
        in_specs=[pl.BlockSpec((tm, width), lambda i: (i, col_q // width)),
                  pl.BlockSpec((tm, width), lambda i: (i, col_k // width)),
                  pl.BlockSpec((1, HEAD_W), lambda i: (0, 0)),
                  pl.BlockSpec((1, HEAD_W), lambda i: (0, 0))],
        out_specs=[pl.BlockSpec((tm, width), lambda i: (i, 0)),
                   pl.BlockSpec((tm, width), lambda i: (i, 0))],
        out_shape=[jax.ShapeDtypeStruct((ntok, width), BF16)] * 2,
        compiler_params=_cparams(("arbitrary",)),
    )(proj, proj, qg, kg)


def _attn_kernel(qi_ref, ki_ref, q_ref, k_ref, v_ref, rel_ref, lamp_ref, ng_ref, o_ref,
                 m_scr, acc_scr, bias_scr, *, tile):
    b = pl.program_id(0)
    p = pl.program_id(1)
    qi = qi_ref[p]
    ki = ki_ref[p]
    diff = qi - ki

    row = lax.broadcasted_iota(I32, (tile, tile), 0)
    col = lax.broadcasted_iota(I32, (tile, tile), 1)

    @pl.when((b == 0) & (p == 0))
    def _():
        def per_head(h, carry):
            for d in range(2):
                dist = row - col + d * tile
                bias = jnp.full((tile, tile), rel_ref[0, h], F32)
                for j in range(1, RP_BUCKETS):
                    bias = jnp.where(dist >= BUCKET_START[j], rel_ref[j, h], bias)
                bias_scr[d, h] = (bias - rel_ref[RP_BUCKETS - 1, h]) * LOG2E
            return carry
        lax.fori_loop(0, DA_HEADS, per_head, 0)

    @pl.when(ki == 0)
    def _():
        m_scr[...] = jnp.full_like(m_scr, NEG_BIG)
        acc_scr[...] = jnp.zeros_like(acc_scr)

    lane = lax.broadcasted_iota(I32, (tile, HEAD_W), 1)
    lo = lane < DA_DH
    ones = jnp.ones((tile, HEAD_W), BF16)
    reps = tile // HEAD_W

    def step(kind):
        for h in range(DA_HEADS):
            sl = slice(h * HEAD_W, (h + 1) * HEAD_W)
            q = q_ref[:, sl]
            k = k_ref[:, sl]
            v_aug = jnp.concatenate([v_ref[:, sl], ones], axis=1)
            for c in range(2):
                qc = jnp.where(lo, q, jnp.zeros_like(q)) if c == 0 else jnp.where(lo, jnp.zeros_like(q), q)
                s = lax.dot_general(qc, k, NT_DIMS, preferred_element_type=F32)
                if kind != 2:
                    s = s + bias_scr[kind, h]
                if kind == 0:
                    s = jnp.where(row >= col, s, NEG_BIG)
                idx = 2 * h + c
                m_old = m_scr[idx]
                m_cur = jnp.broadcast_to(jnp.max(s, axis=-1, keepdims=True), (tile, HEAD_W))
                m_new = jnp.maximum(m_old, m_cur)
                alpha = jnp.exp2(m_old - m_new)
                pr = jnp.exp2(s - jnp.concatenate([m_new] * reps, axis=1))
                pv = jnp.dot(pr.astype(BF16), v_aug, preferred_element_type=F32)
                acc_scr[idx] = jnp.concatenate([alpha, alpha], axis=1) * acc_scr[idx] + pv
                m_scr[idx] = m_new

    for kind, cond in ((0, diff == 0), (1, diff == 1), (2, diff > 1)):
        pl.when(cond)(functools.partial(step, kind))

    @pl.when(diff == 0)
    def _():
        lp = lamp_ref[...]
        lam = (jnp.exp(jnp.sum(lp[0:1] * lp[1:2], axis=-1, keepdims=True))
               - jnp.exp(jnp.sum(lp[2:3] * lp[3:4], axis=-1, keepdims=True)) + LAMBDA_INIT)
        for h in range(DA_HEADS):
            sl = slice(h * HEAD_W, (h + 1) * HEAD_W)
            a0 = acc_scr[2 * h]
            a1 = acc_scr[2 * h + 1]
            o = a0[:, :DA_DV] / a0[:, DA_DV:] - lam * (a1[:, :DA_DV] / a1[:, DA_DV:])
            ms = jnp.mean(o * o, axis=-1, keepdims=True)
            o_ref[:, sl] = (o * lax.rsqrt(ms + EPS) * ng_ref[...] * (1.0 - LAMBDA_INIT)).astype(o_ref.dtype)


def _diff_attention(qn, kn, proj, col_v, rel_bias, lam_params, norm_g, bsz, seq):
    ntok = qn.shape[0]
    tile = min(ATTN_TILE, seq)
    nq = seq // tile
    width = DA_HEADS * HEAD_W
    qi_list, ki_list = [], []
    for a in range(nq):
        for c in range(a + 1):
            qi_list.append(a)
            ki_list.append(c)
    qi_arr = jnp.asarray(qi_list, I32)
    ki_arr = jnp.asarray(ki_list, I32)
    cv = col_v // width
    grid_spec = pltpu.PrefetchScalarGridSpec(
        num_scalar_prefetch=2,
        grid=(bsz, len(qi_list)),
        in_specs=[pl.BlockSpec((tile, width), lambda b, p, qi, ki: (b * nq + qi[p], 0)),
                  pl.BlockSpec((tile, width), lambda b, p, qi, ki: (b * nq + ki[p], 0)),
                  pl.BlockSpec((tile, width), lambda b, p, qi, ki: (b * nq + ki[p], cv)),
                  pl.BlockSpec(memory_space=pltpu.SMEM),
                  pl.BlockSpec((4, DA_DH), lambda b, p, qi, ki: (0, 0)),
                  pl.BlockSpec((1, DA_DV), lambda b, p, qi, ki: (0, 0))],
        out_specs=pl.BlockSpec((tile, width), lambda b, p, qi, ki: (b * nq + qi[p], 0)),
        scratch_shapes=[pltpu.VMEM((2 * DA_HEADS, tile, HEAD_W), F32),
                        pltpu.VMEM((2 * DA_HEADS, tile, 2 * DA_DV), F32),
                        pltpu.VMEM((2, DA_HEADS, tile, tile), F32)],
    )
    return pl.pallas_call(
        functools.partial(_attn_kernel, tile=tile),
        grid_spec=grid_spec,
        out_shape=jax.ShapeDtypeStruct((ntok, width), BF16),
        compiler_params=_cparams(("arbitrary", "arbitrary")),
    )(qi_arr, ki_arr, qn, kn, proj, rel_bias, lam_params, norm_g.reshape(1, DA_DV))


def _merge_kernel(oa_ref, ob_ref, ga0_ref, ga1_ref, gb0_ref, gb1_ref, x_ref, g1_ref, sc_ref, sh_ref, g2_ref, n2_ref,
                  wa_ref, wb_ref, wo_ref, wrh_ref, wrl_ref, sg_ref, su_ref, sd_ref,
                  base_ref, hp_ref, lg_ref):
    ya = jnp.dot(oa_ref[...], wa_ref[...], preferred_element_type=F32)
    yb = jnp.dot(ob_ref[...], wb_ref[...], preferred_element_type=F32)
    hw = ga0_ref.shape[1]

    def gated(ga_ref, gb_ref, sl):
        return (jax.nn.sigmoid(ga_ref[...].astype(F32)) * ya[:, sl]
                + jax.nn.sigmoid(gb_ref[...].astype(F32)) * yb[:, sl]).astype(BF16)

    merged = jnp.concatenate([gated(ga0_ref, gb0_ref, slice(0, hw)),
                              gated(ga1_ref, gb1_ref, slice(hw, 2 * hw))], axis=1)
    y = jnp.dot(merged, wo_ref[...], preferred_element_type=F32)
    x1 = x_ref[...] + g1_ref[...] * y
    ms = jnp.mean(x1 * x1, axis=-1, keepdims=True)
    h2 = (x1 * lax.rsqrt(ms + EPS) * n2_ref[...]) * (1.0 + sc_ref[...]) + sh_ref[...]
    half = h2.shape[1] // 2
    _store_token_major(hp_ref, _pack_bf16_pair(h2[:, :half], h2[:, half:]))
    hh = h2.astype(BF16)
    a = (_silu(jnp.dot(hh, sg_ref[...], preferred_element_type=F32))
         * jnp.dot(hh, su_ref[...], preferred_element_type=F32)).astype(BF16)
    base_ref[...] = x1 + g2_ref[...] * jnp.dot(a, sd_ref[...], preferred_element_type=F32)
    hl = (h2 - hh.astype(F32)).astype(BF16)
    lg_ref[...] = (lax.dot_general(wrh_ref[...], hh, NT_DIMS, preferred_element_type=F32)
                   + lax.dot_general(wrh_ref[...], hl, NT_DIMS, preferred_element_type=F32)
                   + lax.dot_general(wrl_ref[...], hh, NT_DIMS, preferred_element_type=F32))


def _merge_out(o_hg, o_da, proj, col_ga, col_gb, x2, gate1, scale2, shift2, gate2, norm2_g,
               wa, wb, wo, wr_hi, wr_lo, sg, su, sd, seq):
    ntok, d = x2.shape
    tm = min(MERGE_TILE, seq)
    per_b = seq // tm
    wa_w = o_hg.shape[1]
    wb_w = o_da.shape[1]
    de = sg.shape[1]

    def const(shape):
        return pl.BlockSpec(shape, lambda i: (0,) * len(shape), pipeline_mode=pl.Buffered(1))

    def perb():
        return pl.BlockSpec((None, 1, d), lambda i: (i // per_b, 0, 0))

    return pl.pallas_call(
        _merge_kernel,
        grid=(ntok // tm,),
        in_specs=[pl.BlockSpec((tm, wa_w), lambda i: (i, 0)),
                  pl.BlockSpec((tm, wb_w), lambda i: (i, 0)),
                  pl.BlockSpec((tm, d // 2), lambda i: (i, col_ga // (d // 2))),
                  pl.BlockSpec((tm, d // 2), lambda i: (i, col_ga // (d // 2) + 1)),
                  pl.BlockSpec((tm, d // 2), lambda i: (i, col_gb // (d // 2))),
                  pl.BlockSpec((tm, d // 2), lambda i: (i, col_gb // (d // 2) + 1)),
                  pl.BlockSpec((tm, d), lambda i: (i, 0)),
                  perb(), perb(), perb(), perb(),
                  const((1, d)),
                  const((wa_w, d)), const((wb_w, d)), const((d, d)),
                  const((N_EXPERTS, d)), const((N_EXPERTS, d)),
                  const((d, de)), const((d, de)), const((de, d))],
        out_specs=[pl.BlockSpec((tm, d), lambda i: (i, 0)),
                   pl.BlockSpec((tm * ROW_TILE, HEAD_W), lambda i: (i, 0)),
                   pl.BlockSpec((N_EXPERTS, tm), lambda i: (0, i))],
        out_shape=[jax.ShapeDtypeStruct((ntok, d), F32),
                   jax.ShapeDtypeStruct((ntok * ROW_TILE, HEAD_W), U32),
                   jax.ShapeDtypeStruct((N_EXPERTS, ntok), F32)],
        compiler_params=_cparams(("arbitrary",)),
    )(o_hg, o_da, proj, proj, proj, proj, x2, gate1[:, None, :], scale2[:, None, :], shift2[:, None, :],
      gate2[:, None, :], norm2_g.reshape(1, d), wa, wb, wo, wr_hi, wr_lo, sg, su, sd)


def _route_kernel(lg_ref, rb_ref, te_ref, gw_ref, rk_ref, cnt_ref, r1_ref, ex_ref, carry_scr, *, tt):
    @pl.when(pl.program_id(0) == 0)
    def _():
        carry_scr[...] = jnp.zeros_like(carry_scr)

    per_g = N_EXPERTS // N_GROUPS
    scores = jax.nn.sigmoid(lg_ref[...])
    sel = scores + rb_ref[...]
    sel3 = sel.reshape(N_GROUPS, per_g, tt)
    j_io = lax.broadcasted_iota(I32, (N_GROUPS, per_g, tt), 1)
    m1 = jnp.max(sel3, axis=1, keepdims=True)
    i1 = jnp.min(jnp.where(sel3 == m1, j_io, per_g), axis=1, keepdims=True)
    m2 = jnp.max(jnp.where(j_io == i1, -jnp.inf, sel3), axis=1, keepdims=True)
    gs = (m1 + m2).reshape(N_GROUPS, tt)
    g_io = lax.broadcasted_iota(I32, (N_GROUPS, tt), 0)
    gmask = jnp.zeros((N_GROUPS, tt), jnp.bool_)
    for _ in range(TOPK_GROUPS):
        gm = jnp.max(gs, axis=0, keepdims=True)
        gi = jnp.min(jnp.where(gs == gm, g_io, N_GROUPS), axis=0, keepdims=True)
        hit = g_io == gi
        gmask = gmask | hit
        gs = jnp.where(hit, -jnp.inf, gs)
    emask = jnp.broadcast_to(gmask.reshape(N_GROUPS, 1, tt), (N_GROUPS, per_g, tt)).reshape(N_EXPERTS, tt)
    cand = jnp.where(emask, sel, -jnp.inf)
    e_io = lax.broadcasted_iota(I32, (N_EXPERTS, tt), 0)
    chosen = jnp.zeros((N_EXPERTS, tt), jnp.bool_)
    picks = []
    for _ in range(TOP_K):
        em = jnp.max(cand, axis=0, keepdims=True)
        ei = jnp.min(jnp.where(cand == em, e_io, N_EXPERTS), axis=0, keepdims=True)
        hit = e_io == ei
        chosen = chosen | hit
        cand = jnp.where(hit, -jnp.inf, cand)
        picks.append((ei, hit))
    gsel = jnp.where(chosen, scores, 0.0)
    wnorm = gsel / jnp.sum(gsel, axis=0, keepdims=True) * ROUTE_SCALE
    ch = jnp.where(chosen, 1.0, 0.0)
    tri = (lax.broadcasted_iota(I32, (tt, tt), 0) <= lax.broadcasted_iota(I32, (tt, tt), 1))
    incl = jnp.dot(ch.astype(BF16), jnp.where(tri, 1.0, 0.0).astype(BF16), preferred_element_type=F32)
    carry = carry_scr[...]
    excl = incl - ch + carry
    carry_new = carry + incl[:, tt - 1:tt]
    carry_scr[...] = carry_new
    r1_ref[...] = jnp.where(chosen, excl + 1.0, 0.0).astype(I32)
    ex_ref[...] = excl.astype(I32)
    for r, (ei, hit) in enumerate(picks):
        te_ref[r:r + 1, :] = ei
        gw_ref[r:r + 1, :] = jnp.sum(jnp.where(hit, wnorm, 0.0), axis=0, keepdims=True)
        rk_ref[r:r + 1, :] = jnp.sum(jnp.where(hit, excl, 0.0), axis=0, keepdims=True).astype(I32)
    cnt_ref[...] = jnp.broadcast_to(carry_new, cnt_ref.shape).astype(I32)


def _route(logits_t, router_bias):
    ntok = logits_t.shape[1]
    tt = min(ROUTE_TILE, ntok)
    return pl.pallas_call(
        functools.partial(_route_kernel, tt=tt),
        grid=(ntok // tt,),
        in_specs=[pl.BlockSpec((N_EXPERTS, tt), lambda i: (0, i)),
                  pl.BlockSpec((N_EXPERTS, 1), lambda i: (0, 0))],
        out_specs=[pl.BlockSpec((TOP_K, tt), lambda i: (0, i)),
                   pl.BlockSpec((TOP_K, tt), lambda i: (0, i)),
                   pl.BlockSpec((TOP_K, tt), lambda i: (0, i)),
                   pl.BlockSpec((N_EXPERTS, 128), lambda i: (0, 0)),
                   pl.BlockSpec((N_EXPERTS, tt), lambda i: (0, i)),
                   pl.BlockSpec((N_EXPERTS, tt), lambda i: (0, i))],
        out_shape=[jax.ShapeDtypeStruct((TOP_K, ntok), I32),
                   jax.ShapeDtypeStruct((TOP_K, ntok), F32),
                   jax.ShapeDtypeStruct((TOP_K, ntok), I32),
                   jax.ShapeDtypeStruct((N_EXPERTS, 128), I32),
                   jax.ShapeDtypeStruct((N_EXPERTS, ntok), I32),
                   jax.ShapeDtypeStruct((N_EXPERTS, ntok), I32)],
        scratch_shapes=[pltpu.VMEM((N_EXPERTS, 1), F32)],
        compiler_params=_cparams(("arbitrary",)),
    )(logits_t, router_bias.reshape(N_EXPERTS, 1))


def _slots_kernel(ps_ref, te_ref, rk_ref, d_ref):
    te = te_ref[...]
    dest = rk_ref[...]
    for e in range(N_EXPERTS):
        dest = dest + jnp.where(te == e, ps_ref[e], 0)
    d_ref[...] = dest


def _slots(pad_start, top_e, rank):
    ntok = top_e.shape[1]
    tt = min(2048, ntok)
    return pl.pallas_call(
        _slots_kernel,
        grid=(ntok // tt,),
        in_specs=[pl.BlockSpec(memory_space=pltpu.SMEM),
                  pl.BlockSpec((TOP_K, tt), lambda i: (0, i)),
                  pl.BlockSpec((TOP_K, tt), lambda i: (0, i))],
        out_specs=pl.BlockSpec((TOP_K, tt), lambda i: (0, i)),
        out_shape=jax.ShapeDtypeStruct((TOP_K, ntok), I32),
        compiler_params=_cparams(("arbitrary",)),
    )(pad_start, top_e, rank)


BLOCKS_PER_STEP = 8


def _slot_tokens_kernel(be_ref, bs_ref, r_ref, base_ref, o_ref, *, tm):
    i = pl.program_id(0)
    n_ch = r_ref.shape[1]
    ch_io = lax.broadcasted_iota(I32, (n_ch, tm), 0)
    l_io = lax.broadcasted_iota(I32, (HEAD_W, tm), 0)
    j_io = lax.broadcasted_iota(I32, (1, tm), 1)
    for b in range(BLOCKS_PER_STEP):
        blk = i * BLOCKS_PER_STEP + b
        e = be_ref[blk]
        g = (blk - bs_ref[e]) * tm + j_io
        ch = jnp.sum((base_ref[e] <= g).astype(I32), axis=0, keepdims=True) - 1
        pick = jnp.where(ch_io == ch, 1.0, 0.0).astype(BF16)
        r = r_ref[e]
        hi = (r >> 7).astype(F32).astype(BF16)
        lo = (r & 127).astype(F32).astype(BF16)
        ranks = (lax.dot_general(hi, pick, TN_DIMS, preferred_element_type=F32) * 128.0
                 + lax.dot_general(lo, pick, TN_DIMS, preferred_element_type=F32))
        hit = ranks == (g + 1).astype(F32)
        lane = jnp.sum(jnp.where(hit, l_io, 0), axis=0, keepdims=True)
        o_ref[b] = ch * HEAD_W + lane


def _slot_tokens(r1, excl, block_e, blk_start, tm):
    ntok = r1.shape[1]
    n_ch = ntok // HEAD_W
    n_blocks = block_e.shape[0]
    r3 = r1.reshape(N_EXPERTS, n_ch, HEAD_W)
    base = excl[:, ::HEAD_W].reshape(N_EXPERTS, n_ch, 1)
    grid_spec = pltpu.PrefetchScalarGridSpec(
        num_scalar_prefetch=2,
        grid=(n_blocks // BLOCKS_PER_STEP,),
        in_specs=[pl.BlockSpec((N_EXPERTS, n_ch, HEAD_W), lambda i, be, bs: (0, 0, 0)),
                  pl.BlockSpec((N_EXPERTS, n_ch, 1), lambda i, be, bs: (0, 0, 0))],
        out_specs=pl.BlockSpec((BLOCKS_PER_STEP, 1, tm), lambda i, be, bs: (i, 0, 0)),
    )
    return pl.pallas_call(
        functools.partial(_slot_tokens_kernel, tm=tm),
        grid_spec=grid_spec,
        out_shape=jax.ShapeDtypeStruct((n_blocks, 1, tm), I32),
        compiler_params=_cparams(("arbitrary",)),
    )(block_e, blk_start, r3, base)


GATHER_UNROLL = 8


def _expert_kernel(be_ref, nu_ref, we_ref, idx_ref, idxn_ref, h_hbm, wg_ref, wu_ref, wd_ref, o_ref,
                   xs, wgb, wub, wdb, sem, *, tm):
    del we_ref
    i = pl.program_id(0)
    n_used = nu_ref[0]
    slot = i % 2

    def row_copy(tok, s, r):
        return pltpu.make_async_copy(h_hbm.at[tok], xs.at[s, pl.ds(r * ROW_TILE, ROW_TILE)], sem.at[s])

    def issue(idx_r, s):
        def body(g, carry):
            for u in range(GATHER_UNROLL):
                r = g * GATHER_UNROLL + u
                row_copy(idx_r[0, r], s, r).start(priority=u % 2)
            return carry
        lax.fori_loop(0, tm // GATHER_UNROLL, body, 0)

    @pl.when(i == 0)
    def _():
        issue(idx_ref, 0)

    @pl.when(i + 1 < n_used)
    def _():
        issue(idxn_ref, 1 - slot)

    @pl.when(i < n_used)
    def _():
        e = be_ref[i]
        e_prev = be_ref[jnp.maximum(i - 1, 0)]

        @pl.when((i == 0) | (e != e_prev))
        def _():
            wgb[...] = wg_ref[...].astype(BF16)
            wub[...] = wu_ref[...].astype(BF16)
            wdb[...] = wd_ref[...].astype(BF16)

        def wait_body(g, carry):
            for u in range(GATHER_UNROLL):
                row_copy(0, slot, g * GATHER_UNROLL + u).wait()
            return carry
        lax.fori_loop(0, tm // GATHER_UNROLL, wait_body, 0)

        xlo, xhi = _unpack_bf16_pair(jnp.concatenate(_load_token_major(xs.at[slot], 0, tm), axis=1))
        xlo = xlo.astype(BF16)
        xhi = xhi.astype(BF16)
        half = xlo.shape[1]

        def proj_in(w):
            return (jnp.dot(xlo, w[:half, :], preferred_element_type=F32)
                    + jnp.dot(xhi, w[half:, :], preferred_element_type=F32))

        a = (_silu(proj_in(wgb)) * proj_in(wub)).astype(BF16)
        y = jnp.dot(a, wdb[...], preferred_element_type=F32)
        _store_token_major(o_ref, _pack_bf16_pair(y[:, :half], y[:, half:]))

    @pl.when(i >= n_used)
    def _():
        o_ref[...] = jnp.zeros_like(o_ref)


def _routed_experts(h2p, idx3, block_e, n_used, w_expert, w_gate, w_up, w_down, tm):
    n_blocks = block_e.shape[0]
    d, de = w_gate.shape[-2:]
    h3 = h2p.reshape(-1, ROW_TILE, HEAD_W)

    def wspec(shape):
        return pl.BlockSpec((None, None) + shape, lambda i, be, nu, we: (0, we[i], 0, 0))

    grid_spec = pltpu.PrefetchScalarGridSpec(
        num_scalar_prefetch=3,
        grid=(n_blocks,),
        in_specs=[pl.BlockSpec((None, 1, tm), lambda i, be, nu, we: (i, 0, 0), memory_space=pltpu.SMEM),
                  pl.BlockSpec((None, 1, tm), lambda i, be, nu, we: (jnp.minimum(i + 1, n_blocks - 1), 0, 0),
                               memory_space=pltpu.SMEM),
                  pl.BlockSpec(memory_space=pl.ANY),
                  wspec((d, de)), wspec((d, de)), wspec((de, d))],
        out_specs=pl.BlockSpec((tm * ROW_TILE, HEAD_W), lambda i, be, nu, we: (i, 0)),
        scratch_shapes=[pltpu.VMEM((2, tm * ROW_TILE, HEAD_W), U32),
                        pltpu.VMEM((d, de), BF16), pltpu.VMEM((d, de), BF16), pltpu.VMEM((de, d), BF16),
                        pltpu.SemaphoreType.DMA((2,))],
    )
    return pl.pallas_call(
        functools.partial(_expert_kernel, tm=tm),
        grid_spec=grid_spec,
        out_shape=jax.ShapeDtypeStruct((n_blocks * tm * ROW_TILE, HEAD_W), U32),
        compiler_params=_cparams(("arbitrary",), disable_bounds_checks=True),
    )(block_e, n_used, w_expert, idx3, idx3, h3, w_gate, w_up, w_down)


def _combine_kernel(d_ref, dn_ref, ys_hbm, gw_ref, base_ref, g2_ref, o_ref, buf, sem, *, tm):
    i = pl.program_id(0)
    n = pl.num_programs(0)
    slot = i % 2

    def row_copy(src, s, r, t):
        return pltpu.make_async_copy(ys_hbm.at[src], buf.at[s, pl.ds((r * tm + t) * ROW_TILE, ROW_TILE)],
                                     sem.at[s])

    def issue(d_r, s):
        def body(t, carry):
            for r in range(TOP_K):
                row_copy(d_r[r, t], s, r, t).start(priority=r % 2)
            return carry
        lax.fori_loop(0, tm, body, 0)

    @pl.when(i == 0)
    def _():
        issue(d_ref, 0)

    @pl.when(i + 1 < n)
    def _():
        issue(dn_ref, 1 - slot)

    def wait_body(t, carry):
        for r in range(TOP_K):
            row_copy(0, slot, r, t).wait()
        return carry
    lax.fori_loop(0, tm, wait_body, 0)

    half = ROW_TILE * HEAD_W
    acc_lo = [jnp.zeros((tm, HEAD_W), F32)] * ROW_TILE
    acc_hi = [jnp.zeros((tm, HEAD_W), F32)] * ROW_TILE
    for r in range(TOP_K):
        w = jnp.broadcast_to(gw_ref[:, r:r + 1], (tm, HEAD_W))
        for j, piece in enumerate(_load_token_major(buf.at[slot], r * tm, tm)):
            lo, hi = _unpack_bf16_pair(piece)
            acc_lo[j] = acc_lo[j] + w * lo
            acc_hi[j] = acc_hi[j] + w * hi
    for j in range(ROW_TILE):
        lo_sl = slice(j * HEAD_W, (j + 1) * HEAD_W)
        hi_sl = slice(half + j * HEAD_W, half + (j + 1) * HEAD_W)
        o_ref[:, lo_sl] = base_ref[:, lo_sl] + g2_ref[:, lo_sl] * acc_lo[j]
        o_ref[:, hi_sl] = base_ref[:, hi_sl] + g2_ref[:, hi_sl] * acc_hi[j]


def _combine(ys, dest, gate_w, base, gate2, seq):
    ntok, d = base.shape
    y3 = ys.reshape(-1, ROW_TILE, HEAD_W)
    tm = min(COMBINE_TILE, seq)
    nt = ntok // tm
    per_b = seq // tm
    d3 = dest.reshape(TOP_K, nt, tm).transpose(1, 0, 2)
    return pl.pallas_call(
        functools.partial(_combine_kernel, tm=tm),
        grid=(nt,),
        in_specs=[pl.BlockSpec((None, TOP_K, tm), lambda i: (i, 0, 0), memory_space=pltpu.SMEM),
                  pl.BlockSpec((None, TOP_K, tm), lambda i: (jnp.minimum(i + 1, nt - 1), 0, 0),
                               memory_space=pltpu.SMEM),
                  pl.BlockSpec(memory_space=pl.ANY),
                  pl.BlockSpec((tm, TOP_K), lambda i: (i, 0)),
                  pl.BlockSpec((tm, d), lambda i: (i, 0)),
                  pl.BlockSpec((None, 1, d), lambda i: (i // per_b, 0, 0))],
        out_specs=pl.BlockSpec((tm, d), lambda i: (i, 0)),
        out_shape=jax.ShapeDtypeStruct((ntok, d), F32),
        scratch_shapes=[pltpu.VMEM((2, TOP_K * tm * ROW_TILE, HEAD_W), U32), pltpu.SemaphoreType.DMA((2,))],
        compiler_params=_cparams(("arbitrary",), disable_bounds_checks=True),
    )(d3, d3, y3, gate_w.T, base, gate2[:, None, :])


def kernel(x, c, ada_w, ada_b, norm1_g, w_in, lb_logits, hg_norm_g, q_norm_g, k_norm_g, lambda_q1, lambda_k1,
           lambda_q2, lambda_k2, da_norm_g, rel_bias, w_branch_a, w_branch_b, w_out, norm2_g, router_w,
           router_bias, w_exp_gate, w_exp_up, w_exp_down, w_sh_gate, w_sh_up, w_sh_down):
    bsz, seq, d = x.shape
    ntok = bsz * seq
    l = 0
    x2 = x.reshape(ntok, d)

    mod = _ada_mod(c, ada_w[l], ada_b[l])
    shift1, scale1, gate1, shift2, scale2, gate2 = jnp.split(mod, 6, axis=-1)

    proj = _in_projection(x2, scale1, shift1, norm1_g[l], w_in[l], seq)
    col_hg = 0
    col_q = col_hg + 4 * HG_HEADS * HEAD_W
    col_k = col_q + DA_HEADS * HEAD_W
    col_v = col_k + DA_HEADS * HEAD_W
    col_ga = col_v + DA_HEADS * HEAD_W
    col_gb = col_ga + d

    o_hg = _hgrn_branch(proj, lb_logits, hg_norm_g[l], bsz, seq, col_hg)
    qn, kn = _qk_norm(proj, q_norm_g[l], k_norm_g[l], col_q, col_k)
    lam_params = jnp.stack([lambda_q1[l], lambda_k1[l], lambda_q2[l], lambda_k2[l]])
    o_da = _diff_attention(qn, kn, proj, col_v, rel_bias, lam_params, da_norm_g[l], bsz, seq)

    wr_t = router_w[l].T
    wr_hi = wr_t.astype(BF16)
    wr_lo = (wr_t - wr_hi.astype(F32)).astype(BF16)
    base, h2p, logits_t = _merge_out(
        o_hg, o_da, proj, col_ga, col_gb, x2, gate1, scale2, shift2, gate2, norm2_g[l],
        w_branch_a[l].astype(BF16), w_branch_b[l].astype(BF16), w_out[l].astype(BF16), wr_hi, wr_lo,
        w_sh_gate[l].astype(BF16), w_sh_up[l].astype(BF16), w_sh_down[l].astype(BF16), seq)

    top_e, gate_w, rank, counts, rank1, excl = _route(logits_t, router_bias[l])

    tm_e = EXPERT_TILE
    n_blocks = (ntok * TOP_K) // tm_e + N_EXPERTS
    nblk = (counts[:, 0] + tm_e - 1) // tm_e
    blk_end = jnp.cumsum(nblk)
    blk_start = (blk_end - nblk).astype(I32)
    pad_start = blk_start * tm_e
    n_used = blk_end[-1:].astype(I32)
    block_e = jnp.minimum(jnp.sum(blk_end[None, :] <= jnp.arange(n_blocks, dtype=I32)[:, None], axis=1),
                          N_EXPERTS - 1).astype(I32)

    dest = _slots(pad_start, top_e, rank)
    slot_tok = _slot_tokens(rank1, excl, block_e, blk_start, tm_e)
    e_ids = jnp.arange(N_EXPERTS, dtype=I32)
    later_nonempty = (e_ids[None, :] > e_ids[:, None]) & (nblk[None, :] > 0)
    next_e = jnp.min(jnp.where(later_nonempty, e_ids[None, :], N_EXPERTS), axis=1)
    next_e = jnp.where(next_e < N_EXPERTS, next_e, e_ids)
    blk_ids = jnp.arange(n_blocks, dtype=I32)
    of_expert = block_e[:, None] == e_ids[None, :]
    is_first = jnp.any(of_expert & (blk_ids[:, None] == blk_start[None, :]), axis=1)
    w_expert = jnp.where(is_first, block_e, jnp.sum(jnp.where(of_expert, next_e[None, :], 0), axis=1)).astype(I32)
    ys = _routed_experts(h2p, slot_tok, block_e, n_used, w_expert, w_exp_gate, w_exp_up, w_exp_down, tm_e)
    out = _combine(ys, dest, gate_w, base, gate2, seq)
    return out.reshape(bsz, seq, d)
```

```python
import functools
import math

import numpy as np
import jax
import jax.numpy as jnp
from jax import lax
from jax.experimental import pallas as pl
from jax.experimental.pallas import tpu as pltpu

F32 = jnp.float32
BF16 = jnp.bfloat16
I32 = jnp.int32
U32 = jnp.uint32

HG_HEADS = 8
HG_DK = 128
HG_DV = 128
DA_HEADS = 8
DA_DH = 64
DA_DV = 128
RP_BUCKETS = 32
RP_MAX_EXACT = 16
RP_MAX_DIST = 128
N_EXPERTS = 64
N_GROUPS = 8
TOPK_GROUPS = 4
TOP_K = 8
ROUTE_SCALE = 2.5
EPS = 1e-6
LAMBDA_INIT = 0.8 - 0.6 * math.exp(-0.3 * 0)

HEAD_W = 128
VMEM_LIMIT_BYTES = 56 * 1024 * 1024
NEG_BIG = -1e30
LOG2E = math.log2(math.e)

ATTN_TILE = 512
HGRN_TILE = 256
MERGE_TILE = 256
EXPERT_TILE = 256
COMBINE_TILE = 128
ROUTE_TILE = 512

NT_DIMS = (((1,), (1,)), ((), ()))
TN_DIMS = (((0,), (0,)), ((), ()))


def _cparams(sem, **kw):
    return pltpu.CompilerParams(dimension_semantics=sem, vmem_limit_bytes=VMEM_LIMIT_BYTES, **kw)


def _silu(x):
    return x * jax.nn.sigmoid(x)


HI_MASK = np.uint32(0xFFFF0000)
BF16_HALF_ULP = np.uint32(0x8000)


def _pack_bf16_pair(a, b):
    ua = lax.bitcast_convert_type(a, U32) + BF16_HALF_ULP
    ub = lax.bitcast_convert_type(b, U32) + BF16_HALF_ULP
    return (ua >> 16) | (ub & HI_MASK)


def _unpack_bf16_pair(w):
    lo = lax.bitcast_convert_type(w << 16, F32)
    hi = lax.bitcast_convert_type(w & HI_MASK, F32)
    return lo, hi


ROW_TILE = 8


def _store_token_major(ref, x):
    rows = x.shape[0]
    for j in range(ROW_TILE):
        ref[pl.ds(j, rows, stride=ROW_TILE), :] = x[:, j * HEAD_W:(j + 1) * HEAD_W]


def _load_token_major(ref, first, rows):
    return [ref[pl.ds(first * ROW_TILE + j, rows, stride=ROW_TILE), :] for j in range(ROW_TILE)]


def _t5_bucket_starts():
    n = np.arange(0, RP_MAX_DIST + 1)
    nf = np.maximum(n, 1).astype(np.float32)
    large = RP_MAX_EXACT + (np.log(nf / np.float32(RP_MAX_EXACT)) / np.float32(math.log(RP_MAX_DIST / RP_MAX_EXACT))
                            * np.float32(RP_BUCKETS - RP_MAX_EXACT)).astype(np.int32)
    large = np.minimum(large, RP_BUCKETS - 1)
    bucket = np.where(n < RP_MAX_EXACT, n, large)
    assert np.all(np.diff(bucket) >= 0) and bucket[-1] == RP_BUCKETS - 1
    return [int(np.argmax(bucket >= j)) for j in range(RP_BUCKETS)]


BUCKET_START = _t5_bucket_starts()


def _ada_kernel(c_ref, w_ref, b_ref, o_ref):
    ca = _silu(c_ref[...]).astype(BF16)
    o_ref[...] = jnp.dot(ca, w_ref[...].astype(BF16), preferred_element_type=F32) + b_ref[...]


def _ada_mod(c, ada_w, ada_b):
    bsz, d = c.shape
    n = ada_w.shape[1]
    rows = 8
    cp = jnp.zeros((rows, d), F32).at[:bsz].set(c)
    tn = 1024
    out = pl.pallas_call(
        _ada_kernel,
        grid=(n // tn,),
        in_specs=[pl.BlockSpec((rows, d), lambda j: (0, 0)),
                  pl.BlockSpec((d, tn), lambda j: (0, j)),
                  pl.BlockSpec((1, tn), lambda j: (0, j))],
        out_specs=pl.BlockSpec((rows, tn), lambda j: (0, j)),
        out_shape=jax.ShapeDtypeStruct((rows, n), F32),
        compiler_params=_cparams(("arbitrary",)),
    )(cp, ada_w, ada_b.reshape(1, n))
    return out[:bsz]


def _inproj_kernel(x_ref, sc_ref, sh_ref, g_ref, w_ref, o_ref, h_scr):
    @pl.when(pl.program_id(1) == 0)
    def _():
        x = x_ref[...]
        ms = jnp.mean(x * x, axis=-1, keepdims=True)
        hn = x * lax.rsqrt(ms + EPS) * g_ref[...]
        h_scr[...] = (hn * (1.0 + sc_ref[...]) + sh_ref[...]).astype(BF16)

    o_ref[...] = jnp.dot(h_scr[...], w_ref[...].astype(BF16), preferred_element_type=F32).astype(o_ref.dtype)


def _in_projection(x2, scale, shift, g, w, seq):
    ntok, d = x2.shape
    n = w.shape[1]
    tm = min(1024, seq)
    tn = 1024
    per_b = seq // tm
    return pl.pallas_call(
        _inproj_kernel,
        grid=(ntok // tm, n // tn),
        in_specs=[pl.BlockSpec((tm, d), lambda i, j: (i, 0)),
                  pl.BlockSpec((None, 1, d), lambda i, j: (i // per_b, 0, 0)),
                  pl.BlockSpec((None, 1, d), lambda i, j: (i // per_b, 0, 0)),
                  pl.BlockSpec((1, d), lambda i, j: (0, 0)),
                  pl.BlockSpec((d, tn), lambda i, j: (0, j))],
        out_specs=pl.BlockSpec((tm, tn), lambda i, j: (i, j)),
        out_shape=jax.ShapeDtypeStruct((ntok, n), BF16),
        scratch_shapes=[pltpu.VMEM((tm, d), BF16)],
        compiler_params=_cparams(("arbitrary", "arbitrary")),
    )(x2, scale[:, None, :], shift[:, None, :], g.reshape(1, d), w)


def _hgrn_kernel(q_ref, f_ref, i_ref, g_ref, lbl_ref, ng_ref, o_ref, state_ref, *, tile):
    @pl.when(pl.program_id(1) == 0)
    def _():
        state_ref[...] = jnp.zeros_like(state_ref)

    n_levels = tile.bit_length() - 1
    row = lax.broadcasted_iota(I32, (tile, tile), 0)
    col = lax.broadcasted_iota(I32, (tile, tile), 1)
    lev = jnp.where(row >= col, 31 - lax.clz(row ^ col), -2)
    on_diag = lev == -1
    at_level = [lev == lvl for lvl in range(n_levels)]
    rowk = lax.broadcasted_iota(I32, (tile, HG_DK), 0)
    odd_at = [(rowk & (1 << lvl)) != 0 for lvl in range(n_levels)]

    for h in range(HG_HEADS):
        sl = slice(h * HEAD_W, (h + 1) * HEAD_W)
        q = _silu(q_ref[:, sl].astype(F32)) * (HG_DK ** -0.5)
        ll = lbl_ref[:, sl]
        el = jnp.exp(ll - jnp.max(ll, axis=0, keepdims=True))
        lb = el[0:1, :] / jnp.sum(el, axis=0, keepdims=True)
        fg = lb + (1.0 - lb) * jax.nn.sigmoid(f_ref[:, sl].astype(F32))
        k = 1.0 - fg
        g = jnp.log2(fg)
        v = i_ref[:, sl]

        scores = jnp.where(on_diag,
                           lax.dot_general(q.astype(BF16), k.astype(BF16), NT_DIMS, preferred_element_type=F32),
                           0.0)
        c = g
        e = g
        for lvl in range(n_levels):
            blk = 1 << lvl
            qd = (q * jnp.exp2(c)).astype(BF16)
            kd = (k * jnp.exp2(e - c)).astype(BF16)
            s_l = lax.dot_general(qd, kd, NT_DIMS, preferred_element_type=F32)
            scores = jnp.where(at_level[lvl], s_l, scores)
            odd = odd_at[lvl]
            e_prev = pltpu.roll(e, blk, axis=0)
            e_next = pltpu.roll(e, tile - blk, axis=0)
            c = c + jnp.where(odd, e_prev, 0.0)
            e = e + jnp.where(odd, e_prev, e_next)
        st = state_ref[h]
        qb = (q * jnp.exp2(c)).astype(BF16)
        o = (jnp.dot(scores.astype(BF16), v, preferred_element_type=F32)
             + lax.dot_general(qb, st.astype(BF16), NT_DIMS, preferred_element_type=F32))
        kd = (k * jnp.exp2(e - c)).astype(BF16)
        state_ref[h] = st * jnp.exp2(e[0:1, :]) + lax.dot_general(v, kd, TN_DIMS, preferred_element_type=F32)

        ms = jnp.mean(o * o, axis=-1, keepdims=True)
        gate = _silu(g_ref[:, sl].astype(F32))
        o_ref[:, sl] = (o * lax.rsqrt(ms + EPS) * ng_ref[...] * gate).astype(o_ref.dtype)


def _hgrn_branch(proj, lb_logits, norm_g, bsz, seq, col0):
    ntok = proj.shape[0]
    tile = min(HGRN_TILE, seq)
    nt = seq // tile
    width = HG_HEADS * HEAD_W
    cb = col0 // width

    def spec(off):
        return pl.BlockSpec((tile, width), lambda b, t, off=off: (b * nt + t, cb + off))

    return pl.pallas_call(
        functools.partial(_hgrn_kernel, tile=tile),
        grid=(bsz, nt),
        in_specs=[spec(0), spec(1), spec(2), spec(3),
                  pl.BlockSpec(lb_logits.shape, lambda b, t: (0, 0)),
                  pl.BlockSpec((1, HG_DV), lambda b, t: (0, 0))],
        out_specs=pl.BlockSpec((tile, width), lambda b, t: (b * nt + t, 0)),
        out_shape=jax.ShapeDtypeStruct((ntok, width), BF16),
        scratch_shapes=[pltpu.VMEM((HG_HEADS, HG_DV, HG_DK), F32)],
        compiler_params=_cparams(("arbitrary", "arbitrary")),
    )(proj, proj, proj, proj, lb_logits, norm_g.reshape(1, HG_DV))


def _qknorm_kernel(q_ref, k_ref, qg_ref, kg_ref, qo_ref, ko_ref):
    lane = lax.broadcasted_iota(I32, (q_ref.shape[0], HEAD_W), 1)
    lo = lane < DA_DH

    def norm(x_ref, g_ref, o_ref, scale):
        for h in range(DA_HEADS):
            sl = slice(h * HEAD_W, (h + 1) * HEAD_W)
            x = x_ref[:, sl].astype(F32)
            xx = x * x
            s0 = jnp.sum(jnp.where(lo, xx, 0.0), axis=-1, keepdims=True)
            s1 = jnp.sum(jnp.where(lo, 0.0, xx), axis=-1, keepdims=True)
            ms = jnp.where(lo, s0, s1) * (1.0 / DA_DH)
            o_ref[:, sl] = (x * lax.rsqrt(ms + EPS) * g_ref[...] * scale).astype(o_ref.dtype)

    norm(q_ref, qg_ref, qo_ref, DA_DH ** -0.5 * LOG2E)
    norm(k_ref, kg_ref, ko_ref, 1.0)


def _qk_norm(proj, q_g, k_g, col_q, col_k):
    ntok = proj.shape[0]
    width = DA_HEADS * HEAD_W
    tm = min(512, ntok)
    qg = jnp.tile(q_g, 2).reshape(1, HEAD_W)
    kg = jnp.tile(k_g, 2).reshape(1, HEAD_W)
    return pl.pallas_call(
        _qknorm_kernel,
        grid=(ntok // tm,),
        in_specs=[pl.BlockSpec((tm, width), lambda i: (i, col_q // width)),
                  pl.BlockSpec((tm, width), lambda i: (i, col_k // width)),
                  pl.BlockSpec((1, HEAD_W), lambda i: (0, 0)),
                  pl.BlockSpec((1, HEAD_W), lambda i: (0, 0))],
        out_specs=[pl.BlockSpec((tm, width), lambda i: (i, 0)),
                   pl.BlockSpec((tm, width), lambda i: (i, 0))],
        out_shape=[jax.ShapeDtypeStruct((ntok, width), BF16)] * 2,
        compiler_params=_cparams(("arbitrary",)),
    )(proj, proj, qg, kg)


def _attn_kernel(qi_ref, ki_ref, q_ref, k_ref, v_ref, rel_ref, lamp_ref, ng_ref, o_ref,
                 m_scr, acc_scr, bias_scr, *, tile):
    b = pl.program_id(0)
    p = pl.program_id(1)
    qi = qi_ref[p]
    ki = ki_ref[p]
    diff = qi - ki

    row = lax.broadcasted_iota(I32, (tile, tile), 0)
    col = lax.broadcasted_iota(I32, (tile, tile), 1)

    @pl.when((b == 0) & (p == 0))
    def _():
        def per_head(h, carry):
            for d in range(2):
                dist = row - col + d * tile
                bias = jnp.full((tile, tile), rel_ref[0, h], F32)
                for j in range(1, RP_BUCKETS):
                    bias = jnp.where(dist >= BUCKET_START[j], rel_ref[j, h], bias)
                bias_scr[d, h] = (bias - rel_ref[RP_BUCKETS - 1, h]) * LOG2E
            return carry
        lax.fori_loop(0, DA_HEADS, per_head, 0)

    @pl.when(ki == 0)
    def _():
        m_scr[...] = jnp.full_like(m_scr, NEG_BIG)
        acc_scr[...] = jnp.zeros_like(acc_scr)

    lane = lax.broadcasted_iota(I32, (tile, HEAD_W), 1)
    lo = lane < DA_DH
    ones = jnp.ones((tile, HEAD_W), BF16)
    reps = tile // HEAD_W

    def step(kind):
        for h in range(DA_HEADS):
            sl = slice(h * HEAD_W, (h + 1) * HEAD_W)
            q = q_ref[:, sl]
            k = k_ref[:, sl]
            v_aug = jnp.concatenate([v_ref[:, sl], ones], axis=1)
            for c in range(2):
                qc = jnp.where(lo, q, jnp.zeros_like(q)) if c == 0 else jnp.where(lo, jnp.zeros_like(q), q)
                s = lax.dot_general(qc, k, NT_DIMS, preferred_element_type=F32)
                if kind != 2:
                    s = s + bias_scr[kind, h]
                if kind == 0:
                    s = jnp.where(row >= col, s, NEG_BIG)
                idx = 2 * h + c
                m_old = m_scr[idx]
                m_cur = jnp.broadcast_to(jnp.max(s, axis=-1, keepdims=True), (tile, HEAD_W))
                m_new = jnp.maximum(m_old, m_cur)
                alpha = jnp.exp2(m_old - m_new)
                pr = jnp.exp2(s - jnp.concatenate([m_new] * reps, axis=1))
                pv = jnp.dot(pr.astype(BF16), v_aug, preferred_element_type=F32)
                acc_scr[idx] = jnp.concatenate([alpha, alpha], axis=1) * acc_scr[idx] + pv
                m_scr[idx] = m_new

    for kind, cond in ((0, diff == 0), (1, diff == 1), (2, diff > 1)):
        pl.when(cond)(functools.partial(step, kind))

    @pl.when(diff == 0)
    def _():
        lp = lamp_ref[...]
        lam = (jnp.exp(jnp.sum(lp[0:1] * lp[1:2], axis=-1, keepdims=True))
               - jnp.exp(jnp.sum(lp[2:3] * lp[3:4], axis=-1, keepdims=True)) + LAMBDA_INIT)
        for h in range(DA_HEADS):
            sl = slice(h * HEAD_W, (h + 1) * HEAD_W)
            a0 = acc_scr[2 * h]
            a1 = acc_scr[2 * h + 1]
            o = a0[:, :DA_DV] / a0[:, DA_DV:] - lam * (a1[:, :DA_DV] / a1[:, DA_DV:])
            ms = jnp.mean(o * o, axis=-1, keepdims=True)
            o_ref[:, sl] = (o * lax.rsqrt(ms + EPS) * ng_ref[...] * (1.0 - LAMBDA_INIT)).astype(o_ref.dtype)


def _diff_attention(qn, kn, proj, col_v, rel_bias, lam_params, norm_g, bsz, seq):
    ntok = qn.shape[0]
    tile = min(ATTN_TILE, seq)
    nq = seq // tile
    width = DA_HEADS * HEAD_W
    qi_list, ki_list = [], []
    for a in range(nq):
        for c in range(a + 1):
            qi_list.append(a)
            ki_list.append(c)
    qi_arr = jnp.asarray(qi_list, I32)
    ki_arr = jnp.asarray(ki_list, I32)
    cv = col_v // width
    grid_spec = pltpu.PrefetchScalarGridSpec(
        num_scalar_prefetch=2,
        grid=(bsz, len(qi_list)),
        in_specs=[pl.BlockSpec((tile, width), lambda b, p, qi, ki: (b * nq + qi[p], 0)),
                  pl.BlockSpec((tile, width), lambda b, p, qi, ki: (b * nq + ki[p], 0)),
                  pl.BlockSpec((tile, width), lambda b, p, qi, ki: (b * nq + ki[p], cv)),
                  pl.BlockSpec(memory_space=pltpu.SMEM),
                  pl.BlockSpec((4, DA_DH), lambda b, p, qi, ki: (0, 0)),
                  pl.BlockSpec((1, DA_DV), lambda b, p, qi, ki: (0, 0))],
        out_specs=pl.BlockSpec((tile, width), lambda b, p, qi, ki: (b * nq + qi[p], 0)),
        scratch_shapes=[pltpu.VMEM((2 * DA_HEADS, tile, HEAD_W), F32),
                        pltpu.VMEM((2 * DA_HEADS, tile, 2 * DA_DV), F32),
                        pltpu.VMEM((2, DA_HEADS, tile, tile), F32)],
    )
    return pl.pallas_call(
        functools.partial(_attn_kernel, tile=tile),
        grid_spec=grid_spec,
        out_shape=jax.ShapeDtypeStruct((ntok, width), BF16),
        compiler_params=_cparams(("arbitrary", "arbitrary")),
    )(qi_arr, ki_arr, qn, kn, proj, rel_bias, lam_params, norm_g.reshape(1, DA_DV))


def _merge_kernel(oa_ref, ob_ref, ga0_ref, ga1_ref, gb0_ref, gb1_ref, x_ref, g1_ref, sc_ref, sh_ref, g2_ref, n2_ref,
                  wa_ref, wb_ref, wo_ref, wrh_ref, wrl_ref, sg_ref, su_ref, sd_ref,
                  base_ref, hp_ref, lg_ref):
    ya = jnp.dot(oa_ref[...], wa_ref[...], preferred_element_type=F32)
    yb = jnp.dot(ob_ref[...], wb_ref[...], preferred_element_type=F32)
    hw = ga0_ref.shape[1]

    def gated(ga_ref, gb_ref, sl):
        return (jax.nn.sigmoid(ga_ref[...].astype(F32)) * ya[:, sl]
                + jax.nn.sigmoid(gb_ref[...].astype(F32)) * yb[:, sl]).astype(BF16)

    merged = jnp.concatenate([gated(ga0_ref, gb0_ref, slice(0, hw)),
                              gated(ga1_ref, gb1_ref, slice(hw, 2 * hw))], axis=1)
    y = jnp.dot(merged, wo_ref[...], preferred_element_type=F32)
    x1 = x_ref[...] + g1_ref[...] * y
    ms = jnp.mean(x1 * x1, axis=-1, keepdims=True)
    h2 = (x1 * lax.rsqrt(ms + EPS) * n2_ref[...]) * (1.0 + sc_ref[...]) + sh_ref[...]
    half = h2.shape[1] // 2
    _store_token_major(hp_ref, _pack_bf16_pair(h2[:, :half], h2[:, half:]))
    hh = h2.astype(BF16)
    a = (_silu(jnp.dot(hh, sg_ref[...], preferred_element_type=F32))
         * jnp.dot(hh, su_ref[...], preferred_element_type=F32)).astype(BF16)
    base_ref[...] = x1 + g2_ref[...] * jnp.dot(a, sd_ref[...], preferred_element_type=F32)
    hl = (h2 - hh.astype(F32)).astype(BF16)
    lg_ref[...] = (lax.dot_general(wrh_ref[...], hh, NT_DIMS, preferred_element_type=F32)
                   + lax.dot_general(wrh_ref[...], hl, NT_DIMS, preferred_element_type=F32)
                   + lax.dot_general(wrl_ref[...], hh, NT_DIMS, preferred_element_type=F32))


def _merge_out(o_hg, o_da, proj, col_ga, col_gb, x2, gate1, scale2, shift2, gate2, norm2_g,
               wa, wb, wo, wr_hi, wr_lo, sg, su, sd, seq):
    ntok, d = x2.shape
    tm = min(MERGE_TILE, seq)
    per_b = seq // tm
    wa_w = o_hg.shape[1]
    wb_w = o_da.shape[1]
    de = sg.shape[1]

    def const(shape):
        return pl.BlockSpec(shape, lambda i: (0,) * len(shape), pipeline_mode=pl.Buffered(1))

    def perb():
        return pl.BlockSpec((None, 1, d), lambda i: (i // per_b, 0, 0))

    return pl.pallas_call(
        _merge_kernel,
        grid=(ntok // tm,),
        in_specs=[pl.BlockSpec((tm, wa_w), lambda i: (i, 0)),
                  pl.BlockSpec((tm, wb_w), lambda i: (i, 0)),
                  pl.BlockSpec((tm, d // 2), lambda i: (i, col_ga // (d // 2))),
                  pl.BlockSpec((tm, d // 2), lambda i: (i, col_ga // (d // 2) + 1)),
                  pl.BlockSpec((tm, d // 2), lambda i: (i, col_gb // (d // 2))),
                  pl.BlockSpec((tm, d // 2), lambda i: (i, col_gb // (d // 2) + 1)),
                  pl.BlockSpec((tm, d), lambda i: (i, 0)),
                  perb(), perb(), perb(), perb(),
                  const((1, d)),
                  const((wa_w, d)), const((wb_w, d)), const((d, d)),
                  const((N_EXPERTS, d)), const((N_EXPERTS, d)),
                  const((d, de)), const((d, de)), const((de, d))],
        out_specs=[pl.BlockSpec((tm, d), lambda i: (i, 0)),
                   pl.BlockSpec((tm * ROW_TILE, HEAD_W), lambda i: (i, 0)),
                   pl.BlockSpec((N_EXPERTS, tm), lambda i: (0, i))],
        out_shape=[jax.ShapeDtypeStruct((ntok, d), F32),
                   jax.ShapeDtypeStruct((ntok * ROW_TILE, HEAD_W), U32),
                   jax.ShapeDtypeStruct((N_EXPERTS, ntok), F32)],
        compiler_params=_cparams(("arbitrary",)),
    )(o_hg, o_da, proj, proj, proj, proj, x2, gate1[:, None, :], scale2[:, None, :], shift2[:, None, :],
      gate2[:, None, :], norm2_g.reshape(1, d), wa, wb, wo, wr_hi, wr_lo, sg, su, sd)


def _route_kernel(lg_ref, rb_ref, te_ref, gw_ref, rk_ref, cnt_ref, r1_ref, ex_ref, carry_scr, *, tt):
    @pl.when(pl.program_id(0) == 0)
    def _():
        carry_scr[...] = jnp.zeros_like(carry_scr)

    per_g = N_EXPERTS // N_GROUPS
    scores = jax.nn.sigmoid(lg_ref[...])
    sel = scores + rb_ref[...]
    sel3 = sel.reshape(N_GROUPS, per_g, tt)
    j_io = lax.broadcasted_iota(I32, (N_GROUPS, per_g, tt), 1)
    m1 = jnp.max(sel3, axis=1, keepdims=True)
    i1 = jnp.min(jnp.where(sel3 == m1, j_io, per_g), axis=1, keepdims=True)
    m2 = jnp.max(jnp.where(j_io == i1, -jnp.inf, sel3), axis=1, keepdims=True)
    gs = (m1 + m2).reshape(N_GROUPS, tt)
    g_io = lax.broadcasted_iota(I32, (N_GROUPS, tt), 0)
    gmask = jnp.zeros((N_GROUPS, tt), jnp.bool_)
    for _ in range(TOPK_GROUPS):
        gm = jnp.max(gs, axis=0, keepdims=True)
        gi = jnp.min(jnp.where(gs == gm, g_io, N_GROUPS), axis=0, keepdims=True)
        hit = g_io == gi
        gmask = gmask | hit
        gs = jnp.where(hit, -jnp.inf, gs)
    emask = jnp.broadcast_to(gmask.reshape(N_GROUPS, 1, tt), (N_GROUPS, per_g, tt)).reshape(N_EXPERTS, tt)
    cand = jnp.where(emask, sel, -jnp.inf)
    e_io = lax.broadcasted_iota(I32, (N_EXPERTS, tt), 0)
    chosen = jnp.zeros((N_EXPERTS, tt), jnp.bool_)
    picks = []
    for _ in range(TOP_K):
        em = jnp.max(cand, axis=0, keepdims=True)
        ei = jnp.min(jnp.where(cand == em, e_io, N_EXPERTS), axis=0, keepdims=True)
        hit = e_io == ei
        chosen = chosen | hit
        cand = jnp.where(hit, -jnp.inf, cand)
        picks.append((ei, hit))
    gsel = jnp.where(chosen, scores, 0.0)
    wnorm = gsel / jnp.sum(gsel, axis=0, keepdims=True) * ROUTE_SCALE
    ch = jnp.where(chosen, 1.0, 0.0)
    tri = (lax.broadcasted_iota(I32, (tt, tt), 0) <= lax.broadcasted_iota(I32, (tt, tt), 1))
    incl = jnp.dot(ch.astype(BF16), jnp.where(tri, 1.0, 0.0).astype(BF16), preferred_element_type=F32)
    carry = carry_scr[...]
    excl = incl - ch + carry
    carry_new = carry + incl[:, tt - 1:tt]
    carry_scr[...] = carry_new
    r1_ref[...] = jnp.where(chosen, excl + 1.0, 0.0).astype(I32)
    ex_ref[...] = excl.astype(I32)
    for r, (ei, hit) in enumerate(picks):
        te_ref[r:r + 1, :] = ei
        gw_ref[r:r + 1, :] = jnp.sum(jnp.where(hit, wnorm, 0.0), axis=0, keepdims=True)
        rk_ref[r:r + 1, :] = jnp.sum(jnp.where(hit, excl, 0.0), axis=0, keepdims=True).astype(I32)
    cnt_ref[...] = jnp.broadcast_to(carry_new, cnt_ref.shape).astype(I32)


def _route(logits_t, router_bias):
    ntok = logits_t.shape[1]
    tt = min(ROUTE_TILE, ntok)
    return pl.pallas_call(
        functools.partial(_route_kernel, tt=tt),
        grid=(ntok // tt,),
        in_specs=[pl.BlockSpec((N_EXPERTS, tt), lambda i: (0, i)),
                  pl.BlockSpec((N_EXPERTS, 1), lambda i: (0, 0))],
        out_specs=[pl.BlockSpec((TOP_K, tt), lambda i: (0, i)),
                   pl.BlockSpec((TOP_K, tt), lambda i: (0, i)),
                   pl.BlockSpec((TOP_K, tt), lambda i: (0, i)),
                   pl.BlockSpec((N_EXPERTS, 128), lambda i: (0, 0)),
                   pl.BlockSpec((N_EXPERTS, tt), lambda i: (0, i)),
                   pl.BlockSpec((N_EXPERTS, tt), lambda i: (0, i))],
        out_shape=[jax.ShapeDtypeStruct((TOP_K, ntok), I32),
                   jax.ShapeDtypeStruct((TOP_K, ntok), F32),
                   jax.ShapeDtypeStruct((TOP_K, ntok), I32),
                   jax.ShapeDtypeStruct((N_EXPERTS, 128), I32),
                   jax.ShapeDtypeStruct((N_EXPERTS, ntok), I32),
                   jax.ShapeDtypeStruct((N_EXPERTS, ntok), I32)],
        scratch_shapes=[pltpu.VMEM((N_EXPERTS, 1), F32)],
        compiler_params=_cparams(("arbitrary",)),
    )(logits_t, router_bias.reshape(N_EXPERTS, 1))


def _slots_kernel(ps_ref, te_ref, rk_ref, d_ref):
    te = te_ref[...]
    dest = rk_ref[...]
    for e in range(N_EXPERTS):
        dest = dest + jnp.where(te == e, ps_ref[e], 0)
    d_ref[...] = dest


def _slots(pad_start, top_e, rank):
    ntok = top_e.shape[1]
    tt = min(2048, ntok)
    return pl.pallas_call(
        _slots_kernel,
        grid=(ntok // tt,),
        in_specs=[pl.BlockSpec(memory_space=pltpu.SMEM),
                  pl.BlockSpec((TOP_K, tt), lambda i: (0, i)),
                  pl.BlockSpec((TOP_K, tt), lambda i: (0, i))],
        out_specs=pl.BlockSpec((TOP_K, tt), lambda i: (0, i)),
        out_shape=jax.ShapeDtypeStruct((TOP_K, ntok), I32),
        compiler_params=_cparams(("arbitrary",)),
    )(pad_start, top_e, rank)


BLOCKS_PER_STEP = 8


def _slot_tokens_kernel(be_ref, bs_ref, r_ref, base_ref, o_ref, *, tm):
    i = pl.program_id(0)
    n_ch = r_ref.shape[1]
    ch_io = lax.broadcasted_iota(I32, (n_ch, tm), 0)
    l_io = lax.broadcasted_iota(I32, (HEAD_W, tm), 0)
    j_io = lax.broadcasted_iota(I32, (1, tm), 1)
    for b in range(BLOCKS_PER_STEP):
        blk = i * BLOCKS_PER_STEP + b
        e = be_ref[blk]
        g = (blk - bs_ref[e]) * tm + j_io
        ch = jnp.sum((base_ref[e] <= g).astype(I32), axis=0, keepdims=True) - 1
        pick = jnp.where(ch_io == ch, 1.0, 0.0).astype(BF16)
        r = r_ref[e]
        hi = (r >> 7).astype(F32).astype(BF16)
        lo = (r & 127).astype(F32).astype(BF16)
        ranks = (lax.dot_general(hi, pick, TN_DIMS, preferred_element_type=F32) * 128.0
                 + lax.dot_general(lo, pick, TN_DIMS, preferred_element_type=F32))
        hit = ranks == (g + 1).astype(F32)
        lane = jnp.sum(jnp.where(hit, l_io, 0), axis=0, keepdims=True)
        o_ref[b] = ch * HEAD_W + lane


def _slot_tokens(r1, excl, block_e, blk_start, tm):
    ntok = r1.shape[1]
    n_ch = ntok // HEAD_W
    n_blocks = block_e.shape[0]
    r3 = r1.reshape(N_EXPERTS, n_ch, HEAD_W)
    base = excl[:, ::HEAD_W].reshape(N_EXPERTS, n_ch, 1)
    grid_spec = pltpu.PrefetchScalarGridSpec(
        num_scalar_prefetch=2,
        grid=(n_blocks // BLOCKS_PER_STEP,),
        in_specs=[pl.BlockSpec((N_EXPERTS, n_ch, HEAD_W), lambda i, be, bs: (0, 0, 0)),
                  pl.BlockSpec((N_EXPERTS, n_ch, 1), lambda i, be, bs: (0, 0, 0))],
        out_specs=pl.BlockSpec((BLOCKS_PER_STEP, 1, tm), lambda i, be, bs: (i, 0, 0)),
    )
    return pl.pallas_call(
        functools.partial(_slot_tokens_kernel, tm=tm),
        grid_spec=grid_spec,
        out_shape=jax.ShapeDtypeStruct((n_blocks, 1, tm), I32),
        compiler_params=_cparams(("arbitrary",)),
    )(block_e, blk_start, r3, base)


GATHER_UNROLL = 8


def _expert_kernel(be_ref, nu_ref, we_ref, idx_ref, idxn_ref, h_hbm, wg_ref, wu_ref, wd_ref, o_ref,
                   xs, wgb, wub, wdb, sem, *, tm):
    del we_ref
    i = pl.program_id(0)
    n_used = nu_ref[0]
    slot = i % 2

    def row_copy(tok, s, r):
        return pltpu.make_async_copy(h_hbm.at[tok], xs.at[s, pl.ds(r * ROW_TILE, ROW_TILE)], sem.at[s])

    def issue(idx_r, s):
        def body(g, carry):
            for u in range(GATHER_UNROLL):
                r = g * GATHER_UNROLL + u
                row_copy(idx_r[0, r], s, r).start(priority=u % 2)
            return carry
        lax.fori_loop(0, tm // GATHER_UNROLL, body, 0)

    @pl.when(i == 0)
    def _():
        issue(idx_ref, 0)

    @pl.when(i + 1 < n_used)
    def _():
        issue(idxn_ref, 1 - slot)

    @pl.when(i < n_used)
    def _():
        e = be_ref[i]
        e_prev = be_ref[jnp.maximum(i - 1, 0)]

        @pl.when((i == 0) | (e != e_prev))
        def _():
            wgb[...] = wg_ref[...].astype(BF16)
            wub[...] = wu_ref[...].astype(BF16)
            wdb[...] = wd_ref[...].astype(BF16)

        def wait_body(g, carry):
            for u in range(GATHER_UNROLL):
                row_copy(0, slot, g * GATHER_UNROLL + u).wait()
            return carry
        lax.fori_loop(0, tm // GATHER_UNROLL, wait_body, 0)

        xlo, xhi = _unpack_bf16_pair(jnp.concatenate(_load_token_major(xs.at[slot], 0, tm), axis=1))
        xlo = xlo.astype(BF16)
        xhi = xhi.astype(BF16)
        half = xlo.shape[1]

        def proj_in(w):
            return (jnp.dot(xlo, w[:half, :], preferred_element_type=F32)
                    + jnp.dot(xhi, w[half:, :], preferred_element_type=F32))

        a = (_silu(proj_in(wgb)) * proj_in(wub)).astype(BF16)
        y = jnp.dot(a, wdb[...], preferred_element_type=F32)
        _store_token_major(o_ref, _pack_bf16_pair(y[:, :half], y[:, half:]))

    @pl.when(i >= n_used)
    def _():
        o_ref[...] = jnp.zeros_like(o_ref)


def _routed_experts(h2p, idx3, block_e, n_used, w_expert, w_gate, w_up, w_down, tm):
    n_blocks = block_e.shape[0]
    d, de = w_gate.shape[-2:]
    h3 = h2p.reshape(-1, ROW_TILE, HEAD_W)

    def wspec(shape):
        return pl.BlockSpec((None, None) + shape, lambda i, be, nu, we: (0, we[i], 0, 0))

    grid_spec = pltpu.PrefetchScalarGridSpec(
        num_scalar_prefetch=3,
        grid=(n_blocks,),
        in_specs=[pl.BlockSpec((None, 1, tm), lambda i, be, nu, we: (i, 0, 0), memory_space=pltpu.SMEM),
                  pl.BlockSpec((None, 1, tm), lambda i, be, nu, we: (jnp.minimum(i + 1, n_blocks - 1), 0, 0),
                               memory_space=pltpu.SMEM),
                  pl.BlockSpec(memory_space=pl.ANY),
                  wspec((d, de)), wspec((d, de)), wspec((de, d))],
        out_specs=pl.BlockSpec((tm * ROW_TILE, HEAD_W), lambda i, be, nu, we: (i, 0)),
        scratch_shapes=[pltpu.VMEM((2, tm * ROW_TILE, HEAD_W), U32),
                        pltpu.VMEM((d, de), BF16), pltpu.VMEM((d, de), BF16), pltpu.VMEM((de, d), BF16),
                        pltpu.SemaphoreType.DMA((2,))],
    )
    return pl.pallas_call(
        functools.partial(_expert_kernel, tm=tm),
        grid_spec=grid_spec,
        out_shape=jax.ShapeDtypeStruct((n_blocks * tm * ROW_TILE, HEAD_W), U32),
        compiler_params=_cparams(("arbitrary",), disable_bounds_checks=True),
    )(block_e, n_used, w_expert, idx3, idx3, h3, w_gate, w_up, w_down)


def _combine_kernel(d_ref, dn_ref, ys_hbm, gw_ref, base_ref, g2_ref, o_ref, buf, sem, *, tm):
    i = pl.program_id(0)
    n = pl.num_programs(0)
    slot = i % 2

    def row_copy(src, s, r, t):
        return pltpu.make_async_copy(ys_hbm.at[src], buf.at[s, pl.ds((r * tm + t) * ROW_TILE, ROW_TILE)],
                                     sem.at[s])

    def issue_token(d_r, s, t):
        for r in range(TOP_K):
            row_copy(d_r[r, t], s, r, t).start(priority=r % 2)

    def group(g, d_next):
        t0 = pl.multiple_of(g * ROW_TILE, ROW_TILE)
        rows = pl.ds(t0, ROW_TILE)
        acc_lo = [jnp.zeros((ROW_TILE, HEAD_W), F32)] * ROW_TILE
        acc_hi = [jnp.zeros((ROW_TILE, HEAD_W), F32)] * ROW_TILE
        for r in range(TOP_K):
            if d_next is not None:
                issue_token(d_next, 1 - slot, g * ROW_TILE + r)
            w = jnp.broadcast_to(gw_ref[rows, r:r + 1], (ROW_TILE, HEAD_W))
            for j, piece in enumerate(_load_token_major(buf.at[slot], r * tm + t0, ROW_TILE)):
                lo, hi = _unpack_bf16_pair(piece)
                acc_lo[j] = acc_lo[j] + w * lo
                acc_hi[j] = acc_hi[j] + w * hi
        half = ROW_TILE * HEAD_W
        for j in range(ROW_TILE):
            lo_sl = slice(j * HEAD_W, (j + 1) * HEAD_W)
            hi_sl = slice(half + j * HEAD_W, half + (j + 1) * HEAD_W)
            o_ref[rows, lo_sl] = base_ref[rows, lo_sl] + g2_ref[:, lo_sl] * acc_lo[j]
            o_ref[rows, hi_sl] = base_ref[rows, hi_sl] + g2_ref[:, hi_sl] * acc_hi[j]

    assert TOP_K == ROW_TILE
    n_groups = tm // ROW_TILE

    @pl.when(i == 0)
    def _():
        def body(t, carry):
            issue_token(d_ref, 0, t)
            return carry
        lax.fori_loop(0, tm, body, 0)

    def wait_body(t, carry):
        for r in range(TOP_K):
            row_copy(0, slot, r, t).wait()
        return carry
    lax.fori_loop(0, tm, wait_body, 0)

    @pl.when(i + 1 < n)
    def _():
        def body(g, carry):
            group(g, dn_ref)
            return carry
        lax.fori_loop(0, n_groups, body, 0)

    @pl.when(i + 1 == n)
    def _():
        def body(g, carry):
            group(g, None)
            return carry
        lax.fori_loop(0, n_groups, body, 0)


def _combine(ys, dest, gate_w, base, gate2, seq):
    ntok, d = base.shape
    y3 = ys.reshape(-1, ROW_TILE, HEAD_W)
    tm = min(COMBINE_TILE, seq)
    nt = ntok // tm
    per_b = seq // tm
    d3 = dest.reshape(TOP_K, nt, tm).transpose(1, 0, 2)
    return pl.pallas_call(
        functools.partial(_combine_kernel, tm=tm),
        grid=(nt,),
        in_specs=[pl.BlockSpec((None, TOP_K, tm), lambda i: (i, 0, 0), memory_space=pltpu.SMEM),
                  pl.BlockSpec((None, TOP_K, tm), lambda i: (jnp.minimum(i + 1, nt - 1), 0, 0),
                               memory_space=pltpu.SMEM),
                  pl.BlockSpec(memory_space=pl.ANY),
                  pl.BlockSpec((tm, TOP_K), lambda i: (i, 0)),
                  pl.BlockSpec((tm, d), lambda i: (i, 0)),
                  pl.BlockSpec((None, 1, d), lambda i: (i // per_b, 0, 0))],
        out_specs=pl.BlockSpec((tm, d), lambda i: (i, 0)),
        out_shape=jax.ShapeDtypeStruct((ntok, d), F32),
        scratch_shapes=[pltpu.VMEM((2, TOP_K * tm * ROW_TILE, HEAD_W), U32), pltpu.SemaphoreType.DMA((2,))],
        compiler_params=_cparams(("arbitrary",), disable_bounds_checks=True),
    )(d3, d3, y3, gate_w.T, base, gate2[:, None, :])


def kernel(x, c, ada_w, ada_b, norm1_g, w_in, lb_logits, hg_norm_g, q_norm_g, k_norm_g, lambda_q1, lambda_k1,
           lambda_q2, lambda_k2, da_norm_g, rel_bias, w_branch_a, w_branch_b, w_out, norm2_g, router_w,
           router_bias, w_exp_gate, w_exp_up, w_exp_down, w_sh_gate, w_sh_up, w_sh_down):
    bsz, seq, d = x.shape
    ntok = bsz * seq
    l = 0
    x2 = x.reshape(ntok, d)

    mod = _ada_mod(c, ada_w[l], ada_b[l])
    shift1, scale1, gate1, shift2, scale2, gate2 = jnp.split(mod, 6, axis=-1)

    proj = _in_projection(x2, scale1, shift1, norm1_g[l], w_in[l], seq)
    col_hg = 0
    col_q = col_hg + 4 * HG_HEADS * HEAD_W
    col_k = col_q + DA_HEADS * HEAD_W
    col_v = col_k + DA_HEADS * HEAD_W
    col_ga = col_v + DA_HEADS * HEAD_W
    col_gb = col_ga + d

    o_hg = _hgrn_branch(proj, lb_logits, hg_norm_g[l], bsz, seq, col_hg)
    qn, kn = _qk_norm(proj, q_norm_g[l], k_norm_g[l], col_q, col_k)
    lam_params = jnp.stack([lambda_q1[l], lambda_k1[l], lambda_q2[l], lambda_k2[l]])
    o_da = _diff_attention(qn, kn, proj, col_v, rel_bias, lam_params, da_norm_g[l], bsz, seq)

    wr_t = router_w[l].T
    wr_hi = wr_t.astype(BF16)
    wr_lo = (wr_t - wr_hi.astype(F32)).astype(BF16)
    base, h2p, logits_t = _merge_out(
        o_hg, o_da, proj, col_ga, col_gb, x2, gate1, scale2, shift2, gate2, norm2_g[l],
        w_branch_a[l].astype(BF16), w_branch_b[l].astype(BF16), w_out[l].astype(BF16), wr_hi, wr_lo,
        w_sh_gate[l].astype(BF16), w_sh_up[l].astype(BF16), w_sh_down[l].astype(BF16), seq)

    top_e, gate_w, rank, counts, rank1, excl = _route(logits_t, router_bias[l])

    tm_e = EXPERT_TILE
    n_blocks = (ntok * TOP_K) // tm_e + N_EXPERTS
    nblk = (counts[:, 0] + tm_e - 1) // tm_e
    blk_end = jnp.cumsum(nblk)
    blk_start = (blk_end - nblk).astype(I32)
    pad_start = blk_start * tm_e
    n_used = blk_end[-1:].astype(I32)
    block_e = jnp.minimum(jnp.sum(blk_end[None, :] <= jnp.arange(n_blocks, dtype=I32)[:, None], axis=1),
                          N_EXPERTS - 1).astype(I32)

    dest = _slots(pad_start, top_e, rank)
    slot_tok = _slot_tokens(rank1, excl, block_e, blk_start, tm_e)
    e_ids = jnp.arange(N_EXPERTS, dtype=I32)
    later_nonempty = (e_ids[None, :] > e_ids[:, None]) & (nblk[None, :] > 0)
    next_e = jnp.min(jnp.where(later_nonempty, e_ids[None, :], N_EXPERTS), axis=1)
    next_e = jnp.where(next_e < N_EXPERTS, next_e, e_ids)
    blk_ids = jnp.arange(n_blocks, dtype=I32)
    of_expert = block_e[:, None] == e_ids[None, :]
    is_first = jnp.any(of_expert & (blk_ids[:, None] == blk_start[None, :]), axis=1)
    w_expert = jnp.where(is_first, block_e, jnp.sum(jnp.where(of_expert, next_e[None, :], 0), axis=1)).astype(I32)
    ys = _routed_experts(h2p, slot_tok, block_e, n_used, w_expert, w_exp_gate, w_exp_up, w_exp_down, tm_e)
    out = _combine(ys, dest, gate_w, base, gate2, seq)
    return out.reshape(bsz, seq, d)
```

```python
import functools
import math

import numpy as np
import jax
import jax.numpy as jnp
from jax import lax
from jax.experimental import pallas as pl
from jax.experimental.pallas import tpu as pltpu

F32 = jnp.float32
BF16 = jnp.bfloat16
I32 = jnp.int32
U32 = jnp.uint32

HG_HEADS = 8
HG_DK = 128
HG_DV = 128
DA_HEADS = 8
DA_DH = 64
DA_DV = 128
RP_BUCKETS = 32
RP_MAX_EXACT = 16
RP_MAX_DIST = 128
N_EXPERTS = 64
N_GROUPS = 8
TOPK_GROUPS = 4
TOP_K = 8
ROUTE_SCALE = 2.5
EPS = 1e-6
LAMBDA_INIT = 0.8 - 0.6 * math.exp(-0.3 * 0)

HEAD_W = 128
VMEM_LIMIT_BYTES = 56 * 1024 * 1024
NEG_BIG = -1e30
LOG2E = math.log2(math.e)

ATTN_TILE = 512
HGRN_TILE = 256
MERGE_TILE = 256
EXPERT_TILE = 256
COMBINE_TILE = 128
ROUTE_TILE = 512

NT_DIMS = (((1,), (1,)), ((), ()))
TN_DIMS = (((0,), (0,)), ((), ()))


def _cparams(sem, **kw):
    return pltpu.CompilerParams(dimension_semantics=sem, vmem_limit_bytes=VMEM_LIMIT_BYTES, **kw)


def _silu(x):
    return x * jax.nn.sigmoid(x)


HI_MASK = np.uint32(0xFFFF0000)
BF16_HALF_ULP = np.uint32(0x8000)


def _pack_bf16_pair(a, b):
    ua = lax.bitcast_convert_type(a, U32) + BF16_HALF_ULP
    ub = lax.bitcast_convert_type(b, U32) + BF16_HALF_ULP
    return (ua >> 16) | (ub & HI_MASK)


def _unpack_bf16_pair(w):
    lo = lax.bitcast_convert_type(w << 16, F32)
    hi = lax.bitcast_convert_type(w & HI_MASK, F32)
    return lo, hi


ROW_TILE = 8


def _store_token_major(ref, x):
    rows = x.shape[0]
    for j in range(ROW_TILE):
        ref[pl.ds(j, rows, stride=ROW_TILE), :] = x[:, j * HEAD_W:(j + 1) * HEAD_W]


def _load_token_major(ref, first, rows):
    return [ref[pl.ds(first * ROW_TILE + j, rows, stride=ROW_TILE), :] for j in range(ROW_TILE)]


def _t5_bucket_starts():
    n = np.arange(0, RP_MAX_DIST + 1)
    nf = np.maximum(n, 1).astype(np.float32)
    large = RP_MAX_EXACT + (np.log(nf / np.float32(RP_MAX_EXACT)) / np.float32(math.log(RP_MAX_DIST / RP_MAX_EXACT))
                            * np.float32(RP_BUCKETS - RP_MAX_EXACT)).astype(np.int32)
    large = np.minimum(large, RP_BUCKETS - 1)
    bucket = np.where(n < RP_MAX_EXACT, n, large)
    assert np.all(np.diff(bucket) >= 0) and bucket[-1] == RP_BUCKETS - 1
    return [int(np.argmax(bucket >= j)) for j in range(RP_BUCKETS)]


BUCKET_START = _t5_bucket_starts()


def _ada_kernel(c_ref, w_ref, b_ref, o_ref):
    ca = _silu(c_ref[...]).astype(BF16)
    o_ref[...] = jnp.dot(ca, w_ref[...].astype(BF16), preferred_element_type=F32) + b_ref[...]


def _ada_mod(c, ada_w, ada_b):
    bsz, d = c.shape
    n = ada_w.shape[1]
    rows = 8
    cp = jnp.zeros((rows, d), F32).at[:bsz].set(c)
    tn = 1024
    out = pl.pallas_call(
        _ada_kernel,
        grid=(n // tn,),
        in_specs=[pl.BlockSpec((rows, d), lambda j: (0, 0)),
                  pl.BlockSpec((d, tn), lambda j: (0, j)),
                  pl.BlockSpec((1, tn), lambda j: (0, j))],
        out_specs=pl.BlockSpec((rows, tn), lambda j: (0, j)),
        out_shape=jax.ShapeDtypeStruct((rows, n), F32),
        compiler_params=_cparams(("arbitrary",)),
    )(cp, ada_w, ada_b.reshape(1, n))
    return out[:bsz]


def _inproj_kernel(x_ref, sc_ref, sh_ref, g_ref, w_ref, o_ref, h_scr):
    @pl.when(pl.program_id(1) == 0)
    def _():
        x = x_ref[...]
        ms = jnp.mean(x * x, axis=-1, keepdims=True)
        hn = x * lax.rsqrt(ms + EPS) * g_ref[...]
        h_scr[...] = (hn * (1.0 + sc_ref[...]) + sh_ref[...]).astype(BF16)

    o_ref[...] = jnp.dot(h_scr[...], w_ref[...].astype(BF16), preferred_element_type=F32).astype(o_ref.dtype)


def _in_projection(x2, scale, shift, g, w, seq):
    ntok, d = x2.shape
    n = w.shape[1]
    tm = min(1024, seq)
    tn = 1024
    per_b = seq // tm
    return pl.pallas_call(
        _inproj_kernel,
        grid=(ntok // tm, n // tn),
        in_specs=[pl.BlockSpec((tm, d), lambda i, j: (i, 0)),
                  pl.BlockSpec((None, 1, d), lambda i, j: (i // per_b, 0, 0)),
                  pl.BlockSpec((None, 1, d), lambda i, j: (i // per_b, 0, 0)),
                  pl.BlockSpec((1, d), lambda i, j: (0, 0)),
                  pl.BlockSpec((d, tn), lambda i, j: (0, j))],
        out_specs=pl.BlockSpec((tm, tn), lambda i, j: (i, j)),
        out_shape=jax.ShapeDtypeStruct((ntok, n), BF16),
        scratch_shapes=[pltpu.VMEM((tm, d), BF16)],
        compiler_params=_cparams(("arbitrary", "arbitrary")),
    )(x2, scale[:, None, :], shift[:, None, :], g.reshape(1, d), w)


def _hgrn_kernel(q_ref, f_ref, i_ref, g_ref, lbl_ref, ng_ref, o_ref, state_ref, *, tile):
    @pl.when(pl.program_id(1) == 0)
    def _():
        state_ref[...] = jnp.zeros_like(state_ref)

    n_levels = tile.bit_length() - 1
    row = lax.broadcasted_iota(I32, (tile, tile), 0)
    col = lax.broadcasted_iota(I32, (tile, tile), 1)
    lev = jnp.where(row >= col, 31 - lax.clz(row ^ col), -2)
    on_diag = lev == -1
    at_level = [lev == lvl for lvl in range(n_levels)]
    rowk = lax.broadcasted_iota(I32, (tile, HG_DK), 0)
    odd_at = [(rowk & (1 << lvl)) != 0 for lvl in range(n_levels)]

    for h in range(HG_HEADS):
        sl = slice(h * HEAD_W, (h + 1) * HEAD_W)
        q = _silu(q_ref[:, sl].astype(F32)) * (HG_DK ** -0.5)
        ll = lbl_ref[:, sl]
        el = jnp.exp(ll - jnp.max(ll, axis=0, keepdims=True))
        lb = el[0:1, :] / jnp.sum(el, axis=0, keepdims=True)
        fg = lb + (1.0 - lb) * jax.nn.sigmoid(f_ref[:, sl].astype(F32))
        k = 1.0 - fg
        g = jnp.log2(fg)
        v = i_ref[:, sl]

        scores = jnp.where(on_diag,
                           lax.dot_general(q.astype(BF16), k.astype(BF16), NT_DIMS, preferred_element_type=F32),
                           0.0)
        c = g
        e = g
        for lvl in range(n_levels):
            blk = 1 << lvl
            qd = (q * jnp.exp2(c)).astype(BF16)
            kd = (k * jnp.exp2(e - c)).astype(BF16)
            s_l = lax.dot_general(qd, kd, NT_DIMS, preferred_element_type=F32)
            scores = jnp.where(at_level[lvl], s_l, scores)
            odd = odd_at[lvl]
            e_prev = pltpu.roll(e, blk, axis=0)
            e_next = pltpu.roll(e, tile - blk, axis=0)
            c = c + jnp.where(odd, e_prev, 0.0)
            e = e + jnp.where(odd, e_prev, e_next)
        st = state_ref[h]
        qb = (q * jnp.exp2(c)).astype(BF16)
        o = (jnp.dot(scores.astype(BF16), v, preferred_element_type=F32)
             + lax.dot_general(qb, st.astype(BF16), NT_DIMS, preferred_element_type=F32))
        kd = (k * jnp.exp2(e - c)).astype(BF16)
        state_ref[h] = st * jnp.exp2(e[0:1, :]) + lax.dot_general(v, kd, TN_DIMS, preferred_element_type=F32)

        ms = jnp.mean(o * o, axis=-1, keepdims=True)
        gate = _silu(g_ref[:, sl].astype(F32))
        o_ref[:, sl] = (o * lax.rsqrt(ms + EPS) * ng_ref[...] * gate).astype(o_ref.dtype)


def _hgrn_branch(proj, lb_logits, norm_g, bsz, seq, col0):
    ntok = proj.shape[0]
    tile = min(HGRN_TILE, seq)
    nt = seq // tile
    width = HG_HEADS * HEAD_W
    cb = col0 // width

    def spec(off):
        return pl.BlockSpec((tile, width), lambda b, t, off=off: (b * nt + t, cb + off))

    return pl.pallas_call(
        functools.partial(_hgrn_kernel, tile=tile),
        grid=(bsz, nt),
        in_specs=[spec(0), spec(1), spec(2), spec(3),
                  pl.BlockSpec(lb_logits.shape, lambda b, t: (0, 0)),
                  pl.BlockSpec((1, HG_DV), lambda b, t: (0, 0))],
        out_specs=pl.BlockSpec((tile, width), lambda b, t: (b * nt + t, 0)),
        out_shape=jax.ShapeDtypeStruct((ntok, width), BF16),
        scratch_shapes=[pltpu.VMEM((HG_HEADS, HG_DV, HG_DK), F32)],
        compiler_params=_cparams(("arbitrary", "arbitrary")),
    )(proj, proj, proj, proj, lb_logits, norm_g.reshape(1, HG_DV))


def _qknorm_kernel(q_ref, k_ref, qg_ref, kg_ref, qo_ref, ko_ref):
    lane = lax.broadcasted_iota(I32, (q_ref.shape[0], HEAD_W), 1)
    lo = lane < DA_DH

    def norm(x_ref, g_ref, o_ref, scale):
        for h in range(DA_HEADS):
            sl = slice(h * HEAD_W, (h + 1) * HEAD_W)
            x = x_ref[:, sl].astype(F32)
            xx = x * x
            s0 = jnp.sum(jnp.where(lo, xx, 0.0), axis=-1, keepdims=True)
            s1 = jnp.sum(jnp.where(lo, 0.0, xx), axis=-1, keepdims=True)
            ms = jnp.where(lo, s0, s1) * (1.0 / DA_DH)
            o_ref[:, sl] = (x * lax.rsqrt(ms + EPS) * g_ref[...] * scale).astype(o_ref.dtype)

    norm(q_ref, qg_ref, qo_ref, DA_DH ** -0.5 * LOG2E)
    norm(k_ref, kg_ref, ko_ref, 1.0)


def _qk_norm(proj, q_g, k_g, col_q, col_k):
    ntok = proj.shape[0]
    width = DA_HEADS * HEAD_W
    tm = min(512, ntok)
    qg = jnp.tile(q_g, 2).reshape(1, HEAD_W)
    kg = jnp.tile(k_g, 2).reshape(1, HEAD_W)
    return pl.pallas_call(
        _qknorm_kernel,
        grid=(ntok // tm,),
        in_specs=[pl.BlockSpec((tm, width), lambda i: (i, col_q // width)),
                  pl.BlockSpec((tm, width), lambda i: (i, col_k // width)),
                  pl.BlockSpec((1, HEAD_W), lambda i: (0, 0)),
                  pl.BlockSpec((1, HEAD_W), lambda i: (0, 0))],
        out_specs=[pl.BlockSpec((tm, width), lambda i: (i, 0)),
                   pl.BlockSpec((tm, width), lambda i: (i, 0))],
        out_shape=[jax.ShapeDtypeStruct((ntok, width), BF16)] * 2,
        compiler_params=_cparams(("arbitrary",)),
    )(proj, proj, qg, kg)


def _attn_kernel(qi_ref, ki_ref, q_ref, k_ref, v_ref, rel_ref, lamp_ref, ng_ref, o_ref,
                 m_scr, acc_scr, bias_scr, *, tile):
    b = pl.program_id(0)
    p = pl.program_id(1)
    qi = qi_ref[p]
    ki = ki_ref[p]
    diff = qi - ki

    row = lax.broadcasted_iota(I32, (tile, tile), 0)
    col = lax.broadcasted_iota(I32, (tile, tile), 1)

    @pl.when((b == 0) & (p == 0))
    def _():
        def per_head(h, carry):
            for d in range(2):
                dist = row - col + d * tile
                bias = jnp.full((tile, tile), rel_ref[0, h], F32)
                for j in range(1, RP_BUCKETS):
                    bias = jnp.where(dist >= BUCKET_START[j], rel_ref[j, h], bias)
                bias_scr[d, h] = (bias - rel_ref[RP_BUCKETS - 1, h]) * LOG2E
            return carry
        lax.fori_loop(0, DA_HEADS, per_head, 0)

    @pl.when(ki == 0)
    def _():
        m_scr[...] = jnp.full_like(m_scr, NEG_BIG)
        acc_scr[...] = jnp.zeros_like(acc_scr)

    lane = lax.broadcasted_iota(I32, (tile, HEAD_W), 1)
    lo = lane < DA_DH
    ones = jnp.ones((tile, HEAD_W), BF16)
    reps = tile // HEAD_W

    def step(kind):
        for h in range(DA_HEADS):
            sl = slice(h * HEAD_W, (h + 1) * HEAD_W)
            q = q_ref[:, sl]
            k = k_ref[:, sl]
            v_aug = jnp.concatenate([v_ref[:, sl], ones], axis=1)
            for c in range(2):
                qc = jnp.where(lo, q, jnp.zeros_like(q)) if c == 0 else jnp.where(lo, jnp.zeros_like(q), q)
                s = lax.dot_general(qc, k, NT_DIMS, preferred_element_type=F32)
                if kind != 2:
                    s = s + bias_scr[kind, h]
                if kind == 0:
                    s = jnp.where(row >= col, s, NEG_BIG)
                idx = 2 * h + c
                m_old = m_scr[idx]
                m_cur = jnp.broadcast_to(jnp.max(s, axis=-1, keepdims=True), (tile, HEAD_W))
                m_new = jnp.maximum(m_old, m_cur)
                alpha = jnp.exp2(m_old - m_new)
                pr = jnp.exp2(s - jnp.concatenate([m_new] * reps, axis=1))
                pv = jnp.dot(pr.astype(BF16), v_aug, preferred_element_type=F32)
                acc_scr[idx] = jnp.concatenate([alpha, alpha], axis=1) * acc_scr[idx] + pv
                m_scr[idx] = m_new

    for kind, cond in ((0, diff == 0), (1, diff == 1), (2, diff > 1)):
        pl.when(cond)(functools.partial(step, kind))

    @pl.when(diff == 0)
    def _():
        lp = lamp_ref[...]
        lam = (jnp.exp(jnp.sum(lp[0:1] * lp[1:2], axis=-1, keepdims=True))
               - jnp.exp(jnp.sum(lp[2:3] * lp[3:4], axis=-1, keepdims=True)) + LAMBDA_INIT)
        for h in range(DA_HEADS):
            sl = slice(h * HEAD_W, (h + 1) * HEAD_W)
            a0 = acc_scr[2 * h]
            a1 = acc_scr[2 * h + 1]
            o = a0[:, :DA_DV] / a0[:, DA_DV:] - lam * (a1[:, :DA_DV] / a1[:, DA_DV:])
            ms = jnp.mean(o * o, axis=-1, keepdims=True)
            o_ref[:, sl] = (o * lax.rsqrt(ms + EPS) * ng_ref[...] * (1.0 - LAMBDA_INIT)).astype(o_ref.dtype)


def _diff_attention(qn, kn, proj, col_v, rel_bias, lam_params, norm_g, bsz, seq):
    ntok = qn.shape[0]
    tile = min(ATTN_TILE, seq)
    nq = seq // tile
    width = DA_HEADS * HEAD_W
    qi_list, ki_list = [], []
    for a in range(nq):
        for c in range(a + 1):
            qi_list.append(a)
            ki_list.append(c)
    qi_arr = jnp.asarray(qi_list, I32)
    ki_arr = jnp.asarray(ki_list, I32)
    cv = col_v // width
    grid_spec = pltpu.PrefetchScalarGridSpec(
        num_scalar_prefetch=2,
        grid=(bsz, len(qi_list)),
        in_specs=[pl.BlockSpec((tile, width), lambda b, p, qi, ki: (b * nq + qi[p], 0)),
                  pl.BlockSpec((tile, width), lambda b, p, qi, ki: (b * nq + ki[p], 0)),
                  pl.BlockSpec((tile, width), lambda b, p, qi, ki: (b * nq + ki[p], cv)),
                  pl.BlockSpec(memory_space=pltpu.SMEM),
                  pl.BlockSpec((4, DA_DH), lambda b, p, qi, ki: (0, 0)),
                  pl.BlockSpec((1, DA_DV), lambda b, p, qi, ki: (0, 0))],
        out_specs=pl.BlockSpec((tile, width), lambda b, p, qi, ki: (b * nq + qi[p], 0)),
        scratch_shapes=[pltpu.VMEM((2 * DA_HEADS, tile, HEAD_W), F32),
                        pltpu.VMEM((2 * DA_HEADS, tile, 2 * DA_DV), F32),
                        pltpu.VMEM((2, DA_HEADS, tile, tile), F32)],
    )
    return pl.pallas_call(
        functools.partial(_attn_kernel, tile=tile),
        grid_spec=grid_spec,
        out_shape=jax.ShapeDtypeStruct((ntok, width), BF16),
        compiler_params=_cparams(("arbitrary", "arbitrary")),
    )(qi_arr, ki_arr, qn, kn, proj, rel_bias, lam_params, norm_g.reshape(1, DA_DV))


def _merge_kernel(oa_ref, ob_ref, ga0_ref, ga1_ref, gb0_ref, gb1_ref, x_ref, g1_ref, sc_ref, sh_ref, g2_ref, n2_ref,
                  wa_ref, wb_ref, wo_ref, wrh_ref, wrl_ref, sg_ref, su_ref, sd_ref,
                  base_ref, hp_ref, lg_ref):
    ya = jnp.dot(oa_ref[...], wa_ref[...], preferred_element_type=F32)
    yb = jnp.dot(ob_ref[...], wb_ref[...], preferred_element_type=F32)
    hw = ga0_ref.shape[1]

    def gated(ga_ref, gb_ref, sl):
        return (jax.nn.sigmoid(ga_ref[...].astype(F32)) * ya[:, sl]
                + jax.nn.sigmoid(gb_ref[...].astype(F32)) * yb[:, sl]).astype(BF16)

    merged = jnp.concatenate([gated(ga0_ref, gb0_ref, slice(0, hw)),
                              gated(ga1_ref, gb1_ref, slice(hw, 2 * hw))], axis=1)
    y = jnp.dot(merged, wo_ref[...], preferred_element_type=F32)
    x1 = x_ref[...] + g1_ref[...] * y
    ms = jnp.mean(x1 * x1, axis=-1, keepdims=True)
    h2 = (x1 * lax.rsqrt(ms + EPS) * n2_ref[...]) * (1.0 + sc_ref[...]) + sh_ref[...]
    half = h2.shape[1] // 2
    _store_token_major(hp_ref, _pack_bf16_pair(h2[:, :half], h2[:, half:]))
    hh = h2.astype(BF16)
    a = (_silu(jnp.dot(hh, sg_ref[...], preferred_element_type=F32))
         * jnp.dot(hh, su_ref[...], preferred_element_type=F32)).astype(BF16)
    base_ref[...] = x1 + g2_ref[...] * jnp.dot(a, sd_ref[...], preferred_element_type=F32)
    hl = (h2 - hh.astype(F32)).astype(BF16)
    lg_ref[...] = (lax.dot_general(wrh_ref[...], hh, NT_DIMS, preferred_element_type=F32)
                   + lax.dot_general(wrh_ref[...], hl, NT_DIMS, preferred_element_type=F32)
                   + lax.dot_general(wrl_ref[...], hh, NT_DIMS, preferred_element_type=F32))


def _merge_out(o_hg, o_da, proj, col_ga, col_gb, x2, gate1, scale2, shift2, gate2, norm2_g,
               wa, wb, wo, wr_hi, wr_lo, sg, su, sd, seq):
    ntok, d = x2.shape
    tm = min(MERGE_TILE, seq)
    per_b = seq // tm
    wa_w = o_hg.shape[1]
    wb_w = o_da.shape[1]
    de = sg.shape[1]

    def const(shape):
        return pl.BlockSpec(shape, lambda i: (0,) * len(shape), pipeline_mode=pl.Buffered(1))

    def perb():
        return pl.BlockSpec((None, 1, d), lambda i: (i // per_b, 0, 0))

    return pl.pallas_call(
        _merge_kernel,
        grid=(ntok // tm,),
        in_specs=[pl.BlockSpec((tm, wa_w), lambda i: (i, 0)),
                  pl.BlockSpec((tm, wb_w), lambda i: (i, 0)),
                  pl.BlockSpec((tm, d // 2), lambda i: (i, col_ga // (d // 2))),
                  pl.BlockSpec((tm, d // 2), lambda i: (i, col_ga // (d // 2) + 1)),
                  pl.BlockSpec((tm, d // 2), lambda i: (i, col_gb // (d // 2))),
                  pl.BlockSpec((tm, d // 2), lambda i: (i, col_gb // (d // 2) + 1)),
                  pl.BlockSpec((tm, d), lambda i: (i, 0)),
                  perb(), perb(), perb(), perb(),
                  const((1, d)),
                  const((wa_w, d)), const((wb_w, d)), const((d, d)),
                  const((N_EXPERTS, d)), const((N_EXPERTS, d)),
                  const((d, de)), const((d, de)), const((de, d))],
        out_specs=[pl.BlockSpec((tm, d), lambda i: (i, 0)),
                   pl.BlockSpec((tm * ROW_TILE, HEAD_W), lambda i: (i, 0)),
                   pl.BlockSpec((N_EXPERTS, tm), lambda i: (0, i))],
        out_shape=[jax.ShapeDtypeStruct((ntok, d), F32),
                   jax.ShapeDtypeStruct((ntok * ROW_TILE, HEAD_W), U32),
                   jax.ShapeDtypeStruct((N_EXPERTS, ntok), F32)],
        compiler_params=_cparams(("arbitrary",)),
    )(o_hg, o_da, proj, proj, proj, proj, x2, gate1[:, None, :], scale2[:, None, :], shift2[:, None, :],
      gate2[:, None, :], norm2_g.reshape(1, d), wa, wb, wo, wr_hi, wr_lo, sg, su, sd)


def _route_kernel(lg_ref, rb_ref, te_ref, gw_ref, rk_ref, cnt_ref, r1_ref, ex_ref, carry_scr, *, tt):
    @pl.when(pl.program_id(0) == 0)
    def _():
        carry_scr[...] = jnp.zeros_like(carry_scr)

    per_g = N_EXPERTS // N_GROUPS
    scores = jax.nn.sigmoid(lg_ref[...])
    sel = scores + rb_ref[...]
    sel3 = sel.reshape(N_GROUPS, per_g, tt)
    j_io = lax.broadcasted_iota(I32, (N_GROUPS, per_g, tt), 1)
    m1 = jnp.max(sel3, axis=1, keepdims=True)
    i1 = jnp.min(jnp.where(sel3 == m1, j_io, per_g), axis=1, keepdims=True)
    m2 = jnp.max(jnp.where(j_io == i1, -jnp.inf, sel3), axis=1, keepdims=True)
    gs = (m1 + m2).reshape(N_GROUPS, tt)
    g_io = lax.broadcasted_iota(I32, (N_GROUPS, tt), 0)
    gmask = jnp.zeros((N_GROUPS, tt), jnp.bool_)
    for _ in range(TOPK_GROUPS):
        gm = jnp.max(gs, axis=0, keepdims=True)
        gi = jnp.min(jnp.where(gs == gm, g_io, N_GROUPS), axis=0, keepdims=True)
        hit = g_io == gi
        gmask = gmask | hit
        gs = jnp.where(hit, -jnp.inf, gs)
    emask = jnp.broadcast_to(gmask.reshape(N_GROUPS, 1, tt), (N_GROUPS, per_g, tt)).reshape(N_EXPERTS, tt)
    cand = jnp.where(emask, sel, -jnp.inf)
    e_io = lax.broadcasted_iota(I32, (N_EXPERTS, tt), 0)
    chosen = jnp.zeros((N_EXPERTS, tt), jnp.bool_)
    picks = []
    for _ in range(TOP_K):
        em = jnp.max(cand, axis=0, keepdims=True)
        ei = jnp.min(jnp.where(cand == em, e_io, N_EXPERTS), axis=0, keepdims=True)
        hit = e_io == ei
        chosen = chosen | hit
        cand = jnp.where(hit, -jnp.inf, cand)
        picks.append((ei, hit))
    gsel = jnp.where(chosen, scores, 0.0)
    wnorm = gsel / jnp.sum(gsel, axis=0, keepdims=True) * ROUTE_SCALE
    ch = jnp.where(chosen, 1.0, 0.0)
    tri = (lax.broadcasted_iota(I32, (tt, tt), 0) <= lax.broadcasted_iota(I32, (tt, tt), 1))
    incl = jnp.dot(ch.astype(BF16), jnp.where(tri, 1.0, 0.0).astype(BF16), preferred_element_type=F32)
    carry = carry_scr[...]
    excl = incl - ch + carry
    carry_new = carry + incl[:, tt - 1:tt]
    carry_scr[...] = carry_new
    r1_ref[...] = jnp.where(chosen, excl + 1.0, 0.0).astype(I32)
    ex_ref[...] = excl.astype(I32)
    for r, (ei, hit) in enumerate(picks):
        te_ref[r:r + 1, :] = ei
        gw_ref[r:r + 1, :] = jnp.sum(jnp.where(hit, wnorm, 0.0), axis=0, keepdims=True)
        rk_ref[r:r + 1, :] = jnp.sum(jnp.where(hit, excl, 0.0), axis=0, keepdims=True).astype(I32)
    cnt_ref[...] = jnp.broadcast_to(carry_new, cnt_ref.shape).astype(I32)


def _route(logits_t, router_bias):
    ntok = logits_t.shape[1]
    tt = min(ROUTE_TILE, ntok)
    return pl.pallas_call(
        functools.partial(_route_kernel, tt=tt),
        grid=(ntok // tt,),
        in_specs=[pl.BlockSpec((N_EXPERTS, tt), lambda i: (0, i)),
                  pl.BlockSpec((N_EXPERTS, 1), lambda i: (0, 0))],
        out_specs=[pl.BlockSpec((TOP_K, tt), lambda i: (0, i)),
                   pl.BlockSpec((TOP_K, tt), lambda i: (0, i)),
                   pl.BlockSpec((TOP_K, tt), lambda i: (0, i)),
                   pl.BlockSpec((N_EXPERTS, 128), lambda i: (0, 0)),
                   pl.BlockSpec((N_EXPERTS, tt), lambda i: (0, i)),
                   pl.BlockSpec((N_EXPERTS, tt), lambda i: (0, i))],
        out_shape=[jax.ShapeDtypeStruct((TOP_K, ntok), I32),
                   jax.ShapeDtypeStruct((TOP_K, ntok), F32),
                   jax.ShapeDtypeStruct((TOP_K, ntok), I32),
                   jax.ShapeDtypeStruct((N_EXPERTS, 128), I32),
                   jax.ShapeDtypeStruct((N_EXPERTS, ntok), I32),
                   jax.ShapeDtypeStruct((N_EXPERTS, ntok), I32)],
        scratch_shapes=[pltpu.VMEM((N_EXPERTS, 1), F32)],
        compiler_params=_cparams(("arbitrary",)),
    )(logits_t, router_bias.reshape(N_EXPERTS, 1))


def _slots_kernel(ps_ref, te_ref, rk_ref, d_ref):
    te = te_ref[...]
    dest = rk_ref[...]
    for e in range(N_EXPERTS):
        dest = dest + jnp.where(te == e, ps_ref[e], 0)
    d_ref[...] = dest


def _slots(pad_start, top_e, rank):
    ntok = top_e.shape[1]
    tt = min(2048, ntok)
    return pl.pallas_call(
        _slots_kernel,
        grid=(ntok // tt,),
        in_specs=[pl.BlockSpec(memory_space=pltpu.SMEM),
                  pl.BlockSpec((TOP_K, tt), lambda i: (0, i)),
                  pl.BlockSpec((TOP_K, tt), lambda i: (0, i))],
        out_specs=pl.BlockSpec((TOP_K, tt), lambda i: (0, i)),
        out_shape=jax.ShapeDtypeStruct((TOP_K, ntok), I32),
        compiler_params=_cparams(("arbitrary",)),
    )(pad_start, top_e, rank)


BLOCKS_PER_STEP = 8


def _slot_tokens_kernel(be_ref, bs_ref, r_ref, base_ref, o_ref, *, tm):
    i = pl.program_id(0)
    n_ch = r_ref.shape[1]
    ch_io = lax.broadcasted_iota(I32, (n_ch, tm), 0)
    l_io = lax.broadcasted_iota(I32, (HEAD_W, tm), 0)
    j_io = lax.broadcasted_iota(I32, (1, tm), 1)
    for b in range(BLOCKS_PER_STEP):
        blk = i * BLOCKS_PER_STEP + b
        e = be_ref[blk]
        g = (blk - bs_ref[e]) * tm + j_io
        ch = jnp.sum((base_ref[e] <= g).astype(I32), axis=0, keepdims=True) - 1
        pick = jnp.where(ch_io == ch, 1.0, 0.0).astype(BF16)
        r = r_ref[e]
        hi = (r >> 7).astype(F32).astype(BF16)
        lo = (r & 127).astype(F32).astype(BF16)
        ranks = (lax.dot_general(hi, pick, TN_DIMS, preferred_element_type=F32) * 128.0
                 + lax.dot_general(lo, pick, TN_DIMS, preferred_element_type=F32))
        hit = ranks == (g + 1).astype(F32)
        lane = jnp.sum(jnp.where(hit, l_io, 0), axis=0, keepdims=True)
        o_ref[b] = ch * HEAD_W + lane


def _slot_tokens(r1, excl, block_e, blk_start, tm):
    ntok = r1.shape[1]
    n_ch = ntok // HEAD_W
    n_blocks = block_e.shape[0]
    r3 = r1.reshape(N_EXPERTS, n_ch, HEAD_W)
    base = excl[:, ::HEAD_W].reshape(N_EXPERTS, n_ch, 1)
    grid_spec = pltpu.PrefetchScalarGridSpec(
        num_scalar_prefetch=2,
        grid=(n_blocks // BLOCKS_PER_STEP,),
        in_specs=[pl.BlockSpec((N_EXPERTS, n_ch, HEAD_W), lambda i, be, bs: (0, 0, 0)),
                  pl.BlockSpec((N_EXPERTS, n_ch, 1), lambda i, be, bs: (0, 0, 0))],
        out_specs=pl.BlockSpec((BLOCKS_PER_STEP, 1, tm), lambda i, be, bs: (i, 0, 0)),
    )
    return pl.pallas_call(
        functools.partial(_slot_tokens_kernel, tm=tm),
        grid_spec=grid_spec,
        out_shape=jax.ShapeDtypeStruct((n_blocks, 1, tm), I32),
        compiler_params=_cparams(("arbitrary",)),
    )(block_e, blk_start, r3, base)


GATHER_UNROLL = 8


def _expert_kernel(be_ref, nu_ref, we_ref, idx_ref, idxn_ref, h_hbm, wg_ref, wu_ref, wd_ref, o_ref,
                   xs, wgb, wub, wdb, sem, *, tm):
    del we_ref
    i = pl.program_id(0)
    n_used = nu_ref[0]
    slot = i % 2

    def row_copy(tok, s, r):
        return pltpu.make_async_copy(h_hbm.at[tok], xs.at[s, pl.ds(r * ROW_TILE, ROW_TILE)], sem.at[s])

    def issue(idx_r, s):
        def body(g, carry):
            for u in range(GATHER_UNROLL):
                r = g * GATHER_UNROLL + u
                row_copy(idx_r[0, r], s, r).start(priority=u % 2)
            return carry
        lax.fori_loop(0, tm // GATHER_UNROLL, body, 0)

    def drain(s):
        def wait_body(g, carry):
            for u in range(GATHER_UNROLL):
                row_copy(0, s, g * GATHER_UNROLL + u).wait()
            return carry
        lax.fori_loop(0, tm // GATHER_UNROLL, wait_body, 0)

    @pl.when(i == 0)
    def _():
        issue(idx_ref, 0)

    @pl.when(i < n_used)
    def _():
        e = be_ref[i]
        e_prev = be_ref[jnp.maximum(i - 1, 0)]

        @pl.when((i == 0) | (e != e_prev))
        def _():
            wgb[...] = wg_ref[...].astype(BF16)
            wub[...] = wu_ref[...].astype(BF16)
            wdb[...] = wd_ref[...].astype(BF16)

        drain(slot)
        xlo, xhi = _unpack_bf16_pair(jnp.concatenate(_load_token_major(xs.at[slot], 0, tm), axis=1))
        xlo = xlo.astype(BF16)
        xhi = xhi.astype(BF16)
        half = xlo.shape[1]

        def proj_in(w):
            return (jnp.dot(xlo, w[:half, :], preferred_element_type=F32)
                    + jnp.dot(xhi, w[half:, :], preferred_element_type=F32))

        a = (_silu(proj_in(wgb)) * proj_in(wub)).astype(BF16)
        y = jnp.dot(a, wdb[...], preferred_element_type=F32)
        packed = _pack_bf16_pair(y[:, :half], y[:, half:])
        per_chunk = tm // ROW_TILE
        for j in range(ROW_TILE):
            for u in range(per_chunk):
                r = j * per_chunk + u
                row_copy(idxn_ref[0, r], 1 - slot, r).start(priority=u % 2)
            o_ref[pl.ds(j, tm, stride=ROW_TILE), :] = packed[:, j * HEAD_W:(j + 1) * HEAD_W]

    @pl.when(i == n_used)
    def _():
        drain(slot)

    @pl.when(i >= n_used)
    def _():
        o_ref[...] = jnp.zeros_like(o_ref)


def _routed_experts(h2p, idx3, block_e, n_used, w_expert, w_gate, w_up, w_down, tm):
    n_blocks = block_e.shape[0]
    d, de = w_gate.shape[-2:]
    h3 = h2p.reshape(-1, ROW_TILE, HEAD_W)

    last = n_blocks - 1

    def wspec(shape):
        return pl.BlockSpec((None, None) + shape, lambda i, be, nu, we: (0, we[jnp.minimum(i, last)], 0, 0))

    grid_spec = pltpu.PrefetchScalarGridSpec(
        num_scalar_prefetch=3,
        grid=(n_blocks + 1,),
        in_specs=[pl.BlockSpec((None, 1, tm), lambda i, be, nu, we: (jnp.minimum(i, last), 0, 0),
                               memory_space=pltpu.SMEM),
                  pl.BlockSpec((None, 1, tm), lambda i, be, nu, we: (jnp.minimum(i + 1, last), 0, 0),
                               memory_space=pltpu.SMEM),
                  pl.BlockSpec(memory_space=pl.ANY),
                  wspec((d, de)), wspec((d, de)), wspec((de, d))],
        out_specs=pl.BlockSpec((tm * ROW_TILE, HEAD_W), lambda i, be, nu, we: (i, 0)),
        scratch_shapes=[pltpu.VMEM((2, tm * ROW_TILE, HEAD_W), U32),
                        pltpu.VMEM((d, de), BF16), pltpu.VMEM((d, de), BF16), pltpu.VMEM((de, d), BF16),
                        pltpu.SemaphoreType.DMA((2,))],
    )
    return pl.pallas_call(
        functools.partial(_expert_kernel, tm=tm),
        grid_spec=grid_spec,
        out_shape=jax.ShapeDtypeStruct(((n_blocks + 1) * tm * ROW_TILE, HEAD_W), U32),
        compiler_params=_cparams(("arbitrary",), disable_bounds_checks=True),
    )(block_e, n_used, w_expert, idx3, idx3, h3, w_gate, w_up, w_down)


def _combine_kernel(d_ref, dn_ref, ys_hbm, gw_ref, base_ref, g2_ref, o_ref, buf, sem, *, tm):
    i = pl.program_id(0)
    n = pl.num_programs(0)
    slot = i % 2

    def row_copy(src, s, r, t):
        return pltpu.make_async_copy(ys_hbm.at[src], buf.at[s, pl.ds((r * tm + t) * ROW_TILE, ROW_TILE)],
                                     sem.at[s])

    def issue_token(d_r, s, t):
        for r in range(TOP_K):
            row_copy(d_r[r, t], s, r, t).start(priority=r % 2)

    def group(g, d_next):
        t0 = pl.multiple_of(g * ROW_TILE, ROW_TILE)
        rows = pl.ds(t0, ROW_TILE)
        acc_lo = [jnp.zeros((ROW_TILE, HEAD_W), F32)] * ROW_TILE
        acc_hi = [jnp.zeros((ROW_TILE, HEAD_W), F32)] * ROW_TILE
        for r in range(TOP_K):
            if d_next is not None:
                issue_token(d_next, 1 - slot, g * ROW_TILE + r)
            w = jnp.broadcast_to(gw_ref[rows, r:r + 1], (ROW_TILE, HEAD_W))
            for j, piece in enumerate(_load_token_major(buf.at[slot], r * tm + t0, ROW_TILE)):
                lo, hi = _unpack_bf16_pair(piece)
                acc_lo[j] = acc_lo[j] + w * lo
                acc_hi[j] = acc_hi[j] + w * hi
        half = ROW_TILE * HEAD_W
        for j in range(ROW_TILE):
            lo_sl = slice(j * HEAD_W, (j + 1) * HEAD_W)
            hi_sl = slice(half + j * HEAD_W, half + (j + 1) * HEAD_W)
            o_ref[rows, lo_sl] = base_ref[rows, lo_sl] + g2_ref[:, lo_sl] * acc_lo[j]
            o_ref[rows, hi_sl] = base_ref[rows, hi_sl] + g2_ref[:, hi_sl] * acc_hi[j]

    assert TOP_K == ROW_TILE
    n_groups = tm // ROW_TILE

    @pl.when(i == 0)
    def _():
        def body(t, carry):
            issue_token(d_ref, 0, t)
            return carry
        lax.fori_loop(0, tm, body, 0)

    def wait_body(t, carry):
        for r in range(TOP_K):
            row_copy(0, slot, r, t).wait()
        return carry
    lax.fori_loop(0, tm, wait_body, 0)

    @pl.when(i + 1 < n)
    def _():
        def body(g, carry):
            group(g, dn_ref)
            return carry
        lax.fori_loop(0, n_groups, body, 0)

    @pl.when(i + 1 == n)
    def _():
        def body(g, carry):
            group(g, None)
            return carry
        lax.fori_loop(0, n_groups, body, 0)


def _combine(ys, dest, gate_w, base, gate2, seq):
    ntok, d = base.shape
    y3 = ys.reshape(-1, ROW_TILE, HEAD_W)
    tm = min(COMBINE_TILE, seq)
    nt = ntok // tm
    per_b = seq // tm
    d3 = dest.reshape(TOP_K, nt, tm).transpose(1, 0, 2)
    return pl.pallas_call(
        functools.partial(_combine_kernel, tm=tm),
        grid=(nt,),
        in_specs=[pl.BlockSpec((None, TOP_K, tm), lambda i: (i, 0, 0), memory_space=pltpu.SMEM),
                  pl.BlockSpec((None, TOP_K, tm), lambda i: (jnp.minimum(i + 1, nt - 1), 0, 0),
                               memory_space=pltpu.SMEM),
                  pl.BlockSpec(memory_space=pl.ANY),
                  pl.BlockSpec((tm, TOP_K), lambda i: (i, 0)),
                  pl.BlockSpec((tm, d), lambda i: (i, 0)),
                  pl.BlockSpec((None, 1, d), lambda i: (i // per_b, 0, 0))],
        out_specs=pl.BlockSpec((tm, d), lambda i: (i, 0)),
        out_shape=jax.ShapeDtypeStruct((ntok, d), F32),
        scratch_shapes=[pltpu.VMEM((2, TOP_K * tm * ROW_TILE, HEAD_W), U32), pltpu.SemaphoreType.DMA((2,))],
        compiler_params=_cparams(("arbitrary",), disable_bounds_checks=True),
    )(d3, d3, y3, gate_w.T, base, gate2[:, None, :])


def kernel(x, c, ada_w, ada_b, norm1_g, w_in, lb_logits, hg_norm_g, q_norm_g, k_norm_g, lambda_q1, lambda_k1,
           lambda_q2, lambda_k2, da_norm_g, rel_bias, w_branch_a, w_branch_b, w_out, norm2_g, router_w,
           router_bias, w_exp_gate, w_exp_up, w_exp_down, w_sh_gate, w_sh_up, w_sh_down):
    bsz, seq, d = x.shape
    ntok = bsz * seq
    l = 0
    x2 = x.reshape(ntok, d)

    mod = _ada_mod(c, ada_w[l], ada_b[l])
    shift1, scale1, gate1, shift2, scale2, gate2 = jnp.split(mod, 6, axis=-1)

    proj = _in_projection(x2, scale1, shift1, norm1_g[l], w_in[l], seq)
    col_hg = 0
    col_q = col_hg + 4 * HG_HEADS * HEAD_W
    col_k = col_q + DA_HEADS * HEAD_W
    col_v = col_k + DA_HEADS * HEAD_W
    col_ga = col_v + DA_HEADS * HEAD_W
    col_gb = col_ga + d

    o_hg = _hgrn_branch(proj, lb_logits, hg_norm_g[l], bsz, seq, col_hg)
    qn, kn = _qk_norm(proj, q_norm_g[l], k_norm_g[l], col_q, col_k)
    lam_params = jnp.stack([lambda_q1[l], lambda_k1[l], lambda_q2[l], lambda_k2[l]])
    o_da = _diff_attention(qn, kn, proj, col_v, rel_bias, lam_params, da_norm_g[l], bsz, seq)

    wr_t = router_w[l].T
    wr_hi = wr_t.astype(BF16)
    wr_lo = (wr_t - wr_hi.astype(F32)).astype(BF16)
    base, h2p, logits_t = _merge_out(
        o_hg, o_da, proj, col_ga, col_gb, x2, gate1, scale2, shift2, gate2, norm2_g[l],
        w_branch_a[l].astype(BF16), w_branch_b[l].astype(BF16), w_out[l].astype(BF16), wr_hi, wr_lo,
        w_sh_gate[l].astype(BF16), w_sh_up[l].astype(BF16), w_sh_down[l].astype(BF16), seq)

    top_e, gate_w, rank, counts, rank1, excl = _route(logits_t, router_bias[l])

    tm_e = EXPERT_TILE
    n_blocks = (ntok * TOP_K) // tm_e + N_EXPERTS
    nblk = (counts[:, 0] + tm_e - 1) // tm_e
    blk_end = jnp.cumsum(nblk)
    blk_start = (blk_end - nblk).astype(I32)
    pad_start = blk_start * tm_e
    n_used = blk_end[-1:].astype(I32)
    block_e = jnp.minimum(jnp.sum(blk_end[None, :] <= jnp.arange(n_blocks, dtype=I32)[:, None], axis=1),
                          N_EXPERTS - 1).astype(I32)

    dest = _slots(pad_start, top_e, rank)
    slot_tok = _slot_tokens(rank1, excl, block_e, blk_start, tm_e)
    e_ids = jnp.arange(N_EXPERTS, dtype=I32)
    later_nonempty = (e_ids[None, :] > e_ids[:, None]) & (nblk[None, :] > 0)
    next_e = jnp.min(jnp.where(later_nonempty, e_ids[None, :], N_EXPERTS), axis=1)
    next_e = jnp.where(next_e < N_EXPERTS, next_e, e_ids)
    blk_ids = jnp.arange(n_blocks, dtype=I32)
    of_expert = block_e[:, None] == e_ids[None, :]
    is_first = jnp.any(of_expert & (blk_ids[:, None] == blk_start[None, :]), axis=1)
    w_expert = jnp.where(is_first, block_e, jnp.sum(jnp.where(of_expert, next_e[None, :], 0), axis=1)).astype(I32)
    ys = _routed_experts(h2p, slot_tok, block_e, n_used, w_expert, w_exp_gate, w_exp_up, w_exp_down, tm_e)
    out = _combine(ys, dest, gate_w, base, gate2, seq)
    return out.reshape(bsz, seq, d)
```
